```python
import math
import jax, jax.numpy as jnp
from jax import lax
import numpy as np

D_MODEL = 1024
BATCH = 8
SEQ = 4096
DEPTH = 1

PLE_DIM = 256
SSM_WIDTH = D_MODEL // 2
SSM_GROUP = 16
SSM_GROUPS = SSM_WIDTH // SSM_GROUP
SSM_STATE = 64
ATTN_WIDTH = D_MODEL - SSM_WIDTH
N_HEADS = 4
HEAD_DIM = 64
V_HEAD_DIM = 2 * HEAD_DIM
ROT_DIM = HEAD_DIM // 4
ROPE_THETA = 500000.0
Q_BLOCK = 128
EPS = 1e-6
IN_COLS = 2 * SSM_WIDTH + 4 * ATTN_WIDTH
DT_MIN = 0.001
DT_MAX = 0.1

kernel_name = "hymba_s5_diffattn_ple_block"


def rmsnorm(x, w):
    xf = x.astype(jnp.float32)
    y = xf * lax.rsqrt(jnp.mean(xf * xf, axis=-1, keepdims=True) + EPS)
    return (y * w.astype(jnp.float32)).astype(x.dtype)


def partial_rope(t, cos, sin):
    half = ROT_DIM // 2
    t1 = t[..., :half].astype(jnp.float32)
    t2 = t[..., half:ROT_DIM].astype(jnp.float32)
    rot = jnp.concatenate([t1 * cos - t2 * sin, t2 * cos + t1 * sin], axis=-1)
    return jnp.concatenate([rot.astype(t.dtype), t[..., ROT_DIM:]], axis=-1)


def s5_branch(u, lam_re, lam_im, log_dt, b_re, b_im, c_re, c_im, d_skip, glu_w, glu_b):
    f32 = jnp.float32
    bsz, s_len, _ = u.shape
    uf = u.astype(f32)
    ug = uf.reshape(bsz, s_len, SSM_GROUPS, SSM_GROUP)
    lr = lam_re.astype(f32)
    li = lam_im.astype(f32)
    dt = jnp.exp(log_dt.astype(f32))[:, None]
    mag = jnp.exp(lr * dt)
    ab_re = mag * jnp.cos(li * dt)
    ab_im = mag * jnp.sin(li * dt)
    nr = ab_re - 1.0
    ni = ab_im
    den = lr * lr + li * li
    coef_re = (nr * lr + ni * li) / den
    coef_im = (ni * lr - nr * li) / den
    br = b_re.astype(f32)
    bi = b_im.astype(f32)
    bb_re = coef_re[..., None] * br - coef_im[..., None] * bi
    bb_im = coef_re[..., None] * bi + coef_im[..., None] * br
    bu_re = jnp.einsum('bsgp,gnp->bsgn', ug, bb_re)
    bu_im = jnp.einsum('bsgp,gnp->bsgn', ug, bb_im)
    a_re = jnp.broadcast_to(ab_re, bu_re.shape)
    a_im = jnp.broadcast_to(ab_im, bu_im.shape)

    def combine(e1, e2):
        a1r, a1i, b1r, b1i = e1
        a2r, a2i, b2r, b2i = e2
        ar = a1r * a2r - a1i * a2i
        ai = a1r * a2i + a1i * a2r
        nbr = a2r * b1r - a2i * b1i + b2r
        nbi = a2r * b1i + a2i * b1r + b2i
        return (ar, ai, nbr, nbi)

    _, _, xr, xi = lax.associative_scan(combine, (a_re, a_im, bu_re, bu_im), axis=1)
    y = (jnp.einsum('bsgn,gpn->bsgp', xr, c_re.astype(f32))
         - jnp.einsum('bsgn,gpn->bsgp', xi, c_im.astype(f32)))
    y = y.reshape(bsz, s_len, SSM_WIDTH) + d_skip.astype(f32) * uf
    y = jax.nn.gelu(y)
    y = y * jax.nn.sigmoid(y @ glu_w.astype(f32) + glu_b.astype(f32))
    return y


def diff_attention(q, k, v, positions, q_norm_w, k_norm_w, lq1, lk1, lq2, lk2,
                   subln_w, lambda_init):
    f32 = jnp.float32
    bsz, s_len, _ = q.shape
    n_blocks = s_len // Q_BLOCK
    q = q.reshape(bsz, s_len, N_HEADS, 2, HEAD_DIM)
    k = k.reshape(bsz, s_len, N_HEADS, 2, HEAD_DIM)
    v = v.reshape(bsz, s_len, N_HEADS, V_HEAD_DIM)
    inv_freq = ROPE_THETA ** (-jnp.arange(0, ROT_DIM, 2, dtype=f32) / ROT_DIM)
    ang = positions.astype(f32)[..., None] * inv_freq
    cos = jnp.cos(ang)[:, :, None, None, :]
    sin = jnp.sin(ang)[:, :, None, None, :]
    q = partial_rope(rmsnorm(q, q_norm_w), cos, sin).astype(f32) * (HEAD_DIM ** -0.5)
    k = partial_rope(rmsnorm(k, k_norm_w), cos, sin).astype(f32)
    lam = (jnp.exp(jnp.sum(lq1.astype(f32) * lk1.astype(f32)))
           - jnp.exp(jnp.sum(lq2.astype(f32) * lk2.astype(f32))) + lambda_init)

    qb = q.reshape(bsz, n_blocks, Q_BLOCK, N_HEADS, 2, HEAD_DIM).transpose(1, 0, 3, 4, 2, 5)
    kt = k.transpose(0, 2, 3, 1, 4)
    vt = v.astype(f32).transpose(0, 2, 1, 3)
    key_idx = jnp.arange(s_len)
    starts = jnp.arange(n_blocks) * Q_BLOCK

    def block(args):
        qblk, start = args
        sc = jnp.einsum('bhcqd,bhckd->bhcqk', qblk, kt)
        q_idx = start + jnp.arange(Q_BLOCK)
        mask = key_idx[None, :] <= q_idx[:, None]
        sc = jnp.where(mask, sc, jnp.finfo(f32).min)
        pr = jax.nn.softmax(sc, axis=-1)
        w = pr[:, :, 0] - lam * pr[:, :, 1]
        return jnp.einsum('bhqk,bhkd->bhqd', w, vt)

    out = lax.map(block, (qb, starts))
    out = out.transpose(1, 0, 3, 2, 4).reshape(bsz, s_len, N_HEADS, V_HEAD_DIM)
    out = rmsnorm(out, subln_w) * (1.0 - lambda_init)
    return out.reshape(bsz, s_len, ATTN_WIDTH)


def setup_inputs(seed: int = 0) -> dict:
    key = jax.random.key(seed)
    ks = jax.random.split(key, 24)
    f32 = jnp.float32
    nrm = lambda k, shape, scale: (jax.random.normal(k, shape, f32) * scale)
    x = jax.random.normal(ks[0], (BATCH, SEQ, D_MODEL), f32)
    p = jax.random.normal(ks[1], (DEPTH, BATCH, SEQ, PLE_DIM), f32)
    positions = jnp.broadcast_to(jnp.arange(SEQ, dtype=jnp.int32)[None, :], (BATCH, SEQ))
    norm_w = 1.0 + nrm(ks[2], (DEPTH, D_MODEL), 0.02)
    w_in = nrm(ks[3], (DEPTH, D_MODEL, IN_COLS), D_MODEL ** -0.5)
    n_idx = jnp.arange(SSM_STATE, dtype=f32)
    ssm_lambda_re = -0.5 + nrm(ks[4], (DEPTH, SSM_GROUPS, SSM_STATE), 0.01)
    ssm_lambda_im = math.pi * n_idx + nrm(ks[5], (DEPTH, SSM_GROUPS, SSM_STATE), 0.01)
    ssm_log_dt = jax.random.uniform(ks[6], (DEPTH, SSM_GROUPS), f32,
                                    math.log(DT_MIN), math.log(DT_MAX))
    ssm_b_re = nrm(ks[7], (DEPTH, SSM_GROUPS, SSM_STATE, SSM_GROUP), (2 * SSM_GROUP) ** -0.5)
    ssm_b_im = nrm(ks[8], (DEPTH, SSM_GROUPS, SSM_STATE, SSM_GROUP), (2 * SSM_GROUP) ** -0.5)
    ssm_c_re = nrm(ks[9], (DEPTH, SSM_GROUPS, SSM_GROUP, SSM_STATE), SSM_STATE ** -0.5)
    ssm_c_im = nrm(ks[10], (DEPTH, SSM_GROUPS, SSM_GROUP, SSM_STATE), SSM_STATE ** -0.5)
    ssm_d = nrm(ks[11], (DEPTH, SSM_WIDTH), 1.0)
    glu_w = nrm(ks[12], (DEPTH, SSM_WIDTH, SSM_WIDTH), SSM_WIDTH ** -0.5)
    glu_b = nrm(ks[13], (DEPTH, SSM_WIDTH), 0.01)
    q_norm_w = 1.0 + nrm(ks[14], (DEPTH, HEAD_DIM), 0.02)
    k_norm_w = 1.0 + nrm(ks[15], (DEPTH, HEAD_DIM), 0.02)
    lambda_q1 = nrm(ks[16], (DEPTH, HEAD_DIM), 0.1)
    lambda_k1 = nrm(ks[17], (DEPTH, HEAD_DIM), 0.1)
    lambda_q2 = nrm(ks[18], (DEPTH, HEAD_DIM), 0.1)
    lambda_k2 = nrm(ks[19], (DEPTH, HEAD_DIM), 0.1)
    subln_w = 1.0 + nrm(ks[20], (DEPTH, V_HEAD_DIM), 0.02)
    w_out = nrm(ks[21], (DEPTH, SSM_WIDTH + ATTN_WIDTH, D_MODEL), (SSM_WIDTH + ATTN_WIDTH) ** -0.5)
    ple_w_proj = nrm(ks[22], (DEPTH, PLE_DIM, D_MODEL), 0.5 * PLE_DIM ** -0.5)
    ple_w_gate = nrm(ks[23], (DEPTH, D_MODEL, D_MODEL), D_MODEL ** -0.5)
    return {
        "x": x, "p": p, "positions": positions, "norm_w": norm_w, "w_in": w_in,
        "ssm_lambda_re": ssm_lambda_re, "ssm_lambda_im": ssm_lambda_im,
        "ssm_log_dt": ssm_log_dt, "ssm_b_re": ssm_b_re, "ssm_b_im": ssm_b_im,
        "ssm_c_re": ssm_c_re, "ssm_c_im": ssm_c_im, "ssm_d": ssm_d,
        "glu_w": glu_w, "glu_b": glu_b, "q_norm_w": q_norm_w, "k_norm_w": k_norm_w,
        "lambda_q1": lambda_q1, "lambda_k1": lambda_k1, "lambda_q2": lambda_q2,
        "lambda_k2": lambda_k2, "subln_w": subln_w, "w_out": w_out,
        "ple_w_proj": ple_w_proj, "ple_w_gate": ple_w_gate,
    }


def reference(x, p, positions, norm_w, w_in, ssm_lambda_re, ssm_lambda_im, ssm_log_dt,
              ssm_b_re, ssm_b_im, ssm_c_re, ssm_c_im, ssm_d, glu_w, glu_b,
              q_norm_w, k_norm_w, lambda_q1, lambda_k1, lambda_q2, lambda_k2,
              subln_w, w_out, ple_w_proj, ple_w_gate):
    splits = [SSM_WIDTH, 2 * SSM_WIDTH, 2 * SSM_WIDTH + ATTN_WIDTH,
              2 * SSM_WIDTH + 2 * ATTN_WIDTH, 2 * SSM_WIDTH + 3 * ATTN_WIDTH]
    for i in range(DEPTH):
        lambda_init = 0.8 - 0.6 * math.exp(-0.3 * i)
        h = rmsnorm(x, norm_w[i])
        proj = h @ w_in[i]
        u, z_s, q, k, v, z_a = jnp.split(proj, splits, axis=-1)
        y_s = s5_branch(u, ssm_lambda_re[i], ssm_lambda_im[i], ssm_log_dt[i],
                        ssm_b_re[i], ssm_b_im[i], ssm_c_re[i], ssm_c_im[i],
                        ssm_d[i], glu_w[i], glu_b[i])
        y_s = (y_s * jax.nn.silu(z_s.astype(jnp.float32))).astype(x.dtype)
        y_a = diff_attention(q, k, v, positions, q_norm_w[i], k_norm_w[i],
                             lambda_q1[i], lambda_k1[i], lambda_q2[i], lambda_k2[i],
                             subln_w[i], lambda_init)
        y_a = (y_a * jax.nn.silu(z_a.astype(jnp.float32))).astype(x.dtype)
        x = x + jnp.concatenate([y_s, y_a], axis=-1) @ w_out[i]
        gate = jax.nn.sigmoid((x @ ple_w_gate[i]).astype(jnp.float32)).astype(x.dtype)
        x = x + gate * (p[i] @ ple_w_proj[i])
    return x
```

```python
import functools
import math

import jax
import jax.numpy as jnp
from jax import lax
from jax.experimental import pallas as pl
from jax.experimental.pallas import tpu as pltpu

F32 = jnp.float32
BF16 = jnp.bfloat16

D_MODEL = 1024
PLE_DIM = 256
SSM_WIDTH = 512
SSM_GROUP = 16
SSM_GROUPS = 32
SSM_STATE = 64
ATTN_WIDTH = 512
N_HEADS = 4
HEAD_DIM = 64
V_HEAD_DIM = 128
ROT_DIM = 16
ROT_HALF = 8
ROPE_THETA = 500000.0
EPS = 1e-6
LOG2E = math.log2(math.e)

V7X_LANES = 128
V7X_SUBLANES = 8
V7X_VMEM_BYTES = 64 * 1024 * 1024
VMEM_LIMIT_BYTES = V7X_VMEM_BYTES * 7 // 8

TOKEN_TILE = 512
SSM_STEPS = 64
GROUPS_PER_BLOCK = 8
N_SSM_BLOCKS = SSM_GROUPS // GROUPS_PER_BLOCK
BLOCK_STATE = GROUPS_PER_BLOCK * SSM_STATE
MASK_VALUE = -1e30


def _compiler_params(n_grid_axes):
    return pltpu.CompilerParams(
        dimension_semantics=("arbitrary",) * n_grid_axes,
        vmem_limit_bytes=VMEM_LIMIT_BYTES,
    )


def _ssm_params_kernel(lr_ref, li_ref, logdt_ref, br_ref, bi_ref,
                       ar_ref, ai_ref, bbr_ref, bbi_ref):
    lr = lr_ref[...]
    li = li_ref[...]
    dt = jnp.exp(logdt_ref[...])
    mag = jnp.exp(lr * dt)
    ab_re = mag * jnp.cos(li * dt)
    ab_im = mag * jnp.sin(li * dt)
    nr = ab_re - 1.0
    ni = ab_im
    den = lr * lr + li * li
    coef_re = (nr * lr + ni * li) / den
    coef_im = (ni * lr - nr * li) / den
    br = br_ref[...]
    bi = bi_ref[...]
    ar_ref[...] = ab_re
    ai_ref[...] = ab_im
    bbr_ref[...] = coef_re * br - coef_im * bi
    bbi_ref[...] = coef_re * bi + coef_im * br


def _ssm_params(lam_re, lam_im, log_dt, b_re, b_im):
    g, n, p = SSM_GROUPS, SSM_STATE, SSM_GROUP
    gn = jax.ShapeDtypeStruct((g, 1, n), F32)
    gpn = jax.ShapeDtypeStruct((g, p, n), F32)
    return pl.pallas_call(
        _ssm_params_kernel,
        out_shape=(gn, gn, gpn, gpn),
        name="ssm_params",
    )(lam_re.reshape(g, 1, n), lam_im.reshape(g, 1, n), log_dt.reshape(g, 1, 1),
      jnp.swapaxes(b_re, 1, 2), jnp.swapaxes(b_im, 1, 2))


def _in_proj_kernel(x_ref, nw_ref, wnat_ref, wtr_ref, pos_ref, invf_ref, qnw_ref, knw_ref,
                    u_ref, zs_ref, qt_ref, k_ref, vt_ref, zat_ref):
    x = x_ref[...]
    ms = jnp.mean(x * x, axis=-1, keepdims=True)
    h = (x * lax.rsqrt(ms + EPS) * nw_ref[...]).astype(BF16)
    nat = jnp.dot(h, wnat_ref[...], preferred_element_type=F32)
    u_ref[...] = nat[:, :SSM_WIDTH].astype(BF16)
    zs_ref[...] = nat[:, SSM_WIDTH:].astype(BF16)
    tr = lax.dot_general(wtr_ref[...], h, (((1,), (1,)), ((), ())),
                         preferred_element_type=F32)
    tm = x.shape[0]
    ang = pos_ref[...].astype(F32) * invf_ref[...]
    cos = jnp.cos(ang)
    sin = jnp.sin(ang)

    def norm_rope(t, w_ref):
        t3 = t.reshape(2 * N_HEADS, HEAD_DIM, tm)
        ms3 = jnp.mean(t3 * t3, axis=1, keepdims=True)
        t3 = t3 * lax.rsqrt(ms3 + EPS) * w_ref[...]
        t1 = t3[:, :ROT_HALF, :]
        t2 = t3[:, ROT_HALF:ROT_DIM, :]
        r1 = t1 * cos - t2 * sin
        r2 = t2 * cos + t1 * sin
        t3 = jnp.concatenate([r1, r2, t3[:, ROT_DIM:, :]], axis=1)
        return t3.reshape(ATTN_WIDTH, tm)

    q = norm_rope(tr[0:ATTN_WIDTH], qnw_ref) * (HEAD_DIM ** -0.5 * LOG2E)
    qt_ref[...] = q.reshape(N_HEADS, 2 * HEAD_DIM, tm).astype(BF16)
    k = norm_rope(tr[ATTN_WIDTH:2 * ATTN_WIDTH], knw_ref)
    k_ref[...] = k.T.astype(BF16)
    vt_ref[...] = tr[2 * ATTN_WIDTH:3 * ATTN_WIDTH].reshape(N_HEADS, V_HEAD_DIM, tm).astype(BF16)
    zat_ref[...] = tr[3 * ATTN_WIDTH:].reshape(N_HEADS, V_HEAD_DIM, tm).astype(BF16)


def _in_proj(x, positions, norm_w, w_in, q_norm_w, k_norm_w):
    b, s, d = x.shape
    tm = TOKEN_TILE
    nt = s // tm
    w_nat = w_in[:, :2 * SSM_WIDTH].astype(BF16)
    w_tr = w_in[:, 2 * SSM_WIDTH:].T.astype(BF16)
    inv_freq = ROPE_THETA ** (-jnp.arange(0, ROT_DIM, 2, dtype=F32) / ROT_DIM)
    head_major = jax.ShapeDtypeStruct((b, N_HEADS, nt, V_HEAD_DIM, tm), BF16)
    head_spec = pl.BlockSpec((None, N_HEADS, None, V_HEAD_DIM, tm), lambda bi, ti: (bi, 0, ti, 0, 0))
    time_major = jax.ShapeDtypeStruct((s, b * SSM_WIDTH), BF16)
    time_spec = pl.BlockSpec((tm, SSM_WIDTH), lambda bi, ti: (ti, bi))
    const2 = lambda bi, ti: (0, 0)
    const3 = lambda bi, ti: (0, 0, 0)
    return pl.pallas_call(
        _in_proj_kernel,
        grid=(b, nt),
        in_specs=[
            pl.BlockSpec((None, tm, d), lambda bi, ti: (bi, ti, 0)),
            pl.BlockSpec((1, d), const2),
            pl.BlockSpec((d, 2 * SSM_WIDTH), const2),
            pl.BlockSpec((4 * ATTN_WIDTH, d), const2),
            pl.BlockSpec((None, 1, tm), lambda bi, ti: (bi, 0, ti)),
            pl.BlockSpec((ROT_HALF, 1), const2),
            pl.BlockSpec((1, HEAD_DIM, 1), const3),
            pl.BlockSpec((1, HEAD_DIM, 1), const3),
        ],
        out_specs=[
            time_spec, time_spec,
            head_spec,
            pl.BlockSpec((None, tm, ATTN_WIDTH), lambda bi, ti: (bi, ti, 0)),
            head_spec, head_spec,
        ],
        out_shape=(time_major, time_major, head_major,
                   jax.ShapeDtypeStruct((b, s, ATTN_WIDTH), BF16), head_major, head_major),
        compiler_params=_compiler_params(2),
        name="in_proj",
    )(x, norm_w.reshape(1, d), w_nat, w_tr, positions.reshape(b, 1, s),
      inv_freq.reshape(ROT_HALF, 1), q_norm_w.reshape(1, HEAD_DIM, 1), k_norm_w.reshape(1, HEAD_DIM, 1))


def _ssm_kernel(u_ref, zs_ref, bmat_ref, cmat_ref, ar_ref, ai_ref, d_ref, gw_ref, gb_ref,
                y_ref, bu_ref, xs_ref, state_ref):
    rows = u_ref.shape[0]
    blk = 2 * BLOCK_STATE

    @pl.when(pl.program_id(0) == 0)
    def _():
        state_ref[...] = jnp.zeros_like(state_ref)

    for m in range(N_SSM_BLOCKS):
        bu_ref[:, m * blk:(m + 1) * blk] = jnp.dot(
            u_ref[:, m * V7X_LANES:(m + 1) * V7X_LANES], bmat_ref[m], preferred_element_type=F32)

    step_rows = 2 * V7X_SUBLANES
    for m in range(N_SSM_BLOCKS):
        re = slice(m * blk, m * blk + BLOCK_STATE)
        im = slice(m * blk + BLOCK_STATE, (m + 1) * blk)
        a_re = jnp.broadcast_to(ar_ref[:, m * BLOCK_STATE:(m + 1) * BLOCK_STATE], (V7X_SUBLANES, BLOCK_STATE))
        a_im = jnp.broadcast_to(ai_ref[:, m * BLOCK_STATE:(m + 1) * BLOCK_STATE], (V7X_SUBLANES, BLOCK_STATE))

        def body(it, carry, re=re, im=im, a_re=a_re, a_im=a_im):
            x_re, x_im = carry
            r0 = pl.multiple_of(it * step_rows, step_rows)
            outs_re, outs_im = [], []
            for half in range(2):
                rr = pl.ds(r0 + half * V7X_SUBLANES, V7X_SUBLANES)
                n_re = a_re * x_re - a_im * x_im + bu_ref[rr, re]
                n_im = a_re * x_im + a_im * x_re + bu_ref[rr, im]
                x_re, x_im = n_re, n_im
                outs_re.append(n_re)
                outs_im.append(n_im)
            xs_ref[pl.ds(r0, step_rows), re] = jnp.concatenate(outs_re, axis=0).astype(BF16)
            xs_ref[pl.ds(r0, step_rows), im] = jnp.concatenate(outs_im, axis=0).astype(BF16)
            return x_re, x_im

        x_re, x_im = lax.fori_loop(0, rows // step_rows, body, (state_ref[:, re], state_ref[:, im]))
        state_ref[:, re] = x_re
        state_ref[:, im] = x_im

    ys = [jnp.dot(xs_ref[:, m * blk:(m + 1) * blk], cmat_ref[m], preferred_element_type=F32)
          for m in range(N_SSM_BLOCKS)]
    y = jnp.concatenate(ys, axis=-1) + d_ref[...] * u_ref[...].astype(F32)
    y = jax.nn.gelu(y)
    gate = jnp.dot(y.astype(BF16), gw_ref[...], preferred_element_type=F32) + gb_ref[...]
    y = y * jax.nn.sigmoid(gate)
    y_ref[...] = (y * jax.nn.silu(zs_ref[...].astype(F32))).astype(BF16)


def _ssm_branch(u_tm, zs_tm, a_re, a_im, bb_re_t, bb_im_t, c_re, c_im, d_skip, glu_w, glu_b):
    n_rows = u_tm.shape[0]
    rows = SSM_STEPS * V7X_SUBLANES
    gpb, nb, p, n = GROUPS_PER_BLOCK, N_SSM_BLOCKS, SSM_GROUP, SSM_STATE
    eye = jnp.eye(gpb, dtype=F32)
    bb = jnp.stack([bb_re_t, bb_im_t]).reshape(2, nb, gpb, p, n)
    bmat = jnp.einsum('rmgpn,gh->mgprhn', bb, eye).reshape(nb, gpb * p, 2 * BLOCK_STATE).astype(BF16)
    cc = jnp.stack([c_re, -c_im]).reshape(2, nb, gpb, p, n)
    cmat = jnp.einsum('rmgpn,gh->mrgnhp', cc, eye).reshape(nb, 2 * BLOCK_STATE, gpb * p).astype(BF16)
    row_spec = pl.BlockSpec((rows, SSM_WIDTH), lambda i: (i, 0))
    const2 = lambda i: (0, 0)
    const3 = lambda i: (0, 0, 0)
    n_state = SSM_GROUPS * SSM_STATE
    return pl.pallas_call(
        _ssm_kernel,
        grid=(n_rows // rows,),
        in_specs=[
            row_spec, row_spec,
            pl.BlockSpec((nb, gpb * p, 2 * BLOCK_STATE), const3),
            pl.BlockSpec((nb, 2 * BLOCK_STATE, gpb * p), const3),
            pl.BlockSpec((1, n_state), const2),
            pl.BlockSpec((1, n_state), const2),
            pl.BlockSpec((1, SSM_WIDTH), const2),
            pl.BlockSpec((SSM_WIDTH, SSM_WIDTH), const2),
            pl.BlockSpec((1, SSM_WIDTH), const2),
        ],
        out_specs=row_spec,
        out_shape=jax.ShapeDtypeStruct((n_rows, SSM_WIDTH), BF16),
        scratch_shapes=[
            pltpu.VMEM((rows, 2 * n_state), F32),
            pltpu.VMEM((rows, 2 * n_state), BF16),
            pltpu.VMEM((V7X_SUBLANES, 2 * n_state), F32),
        ],
        compiler_params=_compiler_params(1),
        name="ssm_branch",
    )(u_tm, zs_tm, bmat, cmat, a_re.reshape(1, n_state), a_im.reshape(1, n_state),
      d_skip.reshape(1, SSM_WIDTH), glu_w.astype(BF16), glu_b.reshape(1, SSM_WIDTH))


def _attn_kernel(lambda_init, qt_ref, k_ref, vt_ref, zat_ref, lq1_ref, lk1_ref, lq2_ref, lk2_ref,
                 subw_ref, y_ref, acc_ref, m_ref, l_ref):
    qi = pl.program_id(2)
    tq = qt_ref.shape[-1]
    tk = tq
    qt = qt_ref[...]
    zeros = jnp.zeros((HEAD_DIM, tq), BF16)
    wq = (jnp.concatenate([qt[:HEAD_DIM], zeros], axis=0),
          jnp.concatenate([zeros, qt[HEAD_DIM:]], axis=0))

    m_ref[...] = jnp.full_like(m_ref, MASK_VALUE)
    l_ref[...] = jnp.zeros_like(l_ref)
    acc_ref[...] = jnp.zeros_like(acc_ref)

    def block(j, masked):
        kb = k_ref[pl.ds(pl.multiple_of(j * tk, tk), tk), :]
        vb = vt_ref[j]
        for c in range(2):
            s = jnp.dot(kb, wq[c], preferred_element_type=F32)
            if masked:
                key_idx = lax.broadcasted_iota(jnp.int32, (tk, tq), 0)
                qry_idx = lax.broadcasted_iota(jnp.int32, (tk, tq), 1)
                s = jnp.where(key_idx <= qry_idx, s, MASK_VALUE)
            m_old = m_ref[c]
            m_new = jnp.maximum(m_old, jnp.max(s, axis=0, keepdims=True))
            alpha = jnp.exp2(m_old - m_new)
            p = jnp.exp2(s - m_new)
            l_ref[c] = alpha * l_ref[c] + jnp.sum(p, axis=0, keepdims=True)
            acc_ref[c] = alpha * acc_ref[c] + jnp.dot(vb, p.astype(BF16), preferred_element_type=F32)
            m_ref[c] = m_new

    def full_block(j, carry):
        block(j, masked=False)
        return carry

    lax.fori_loop(0, qi, full_block, 0)
    block(qi, masked=True)

    lam = (jnp.exp(jnp.sum(lq1_ref[...] * lk1_ref[...], axis=-1, keepdims=True))
           - jnp.exp(jnp.sum(lq2_ref[...] * lk2_ref[...], axis=-1, keepdims=True)) + lambda_init)
    out = acc_ref[0] / l_ref[0] - lam * (acc_ref[1] / l_ref[1])
    ms = jnp.mean(out * out, axis=0, keepdims=True)
    out = out * lax.rsqrt(ms + EPS) * subw_ref[...] * (1.0 - lambda_init)
    out = out * jax.nn.silu(zat_ref[...].astype(F32))
    y_ref[...] = out.T.astype(BF16)


def _attention(qt, k, vt, zat, lq1, lk1, lq2, lk2, subln_w, lambda_init):
    b, _, nt, _, tq = qt.shape
    s = nt * tq
    row64 = pl.BlockSpec((1, HEAD_DIM), lambda bi, hi, qi: (0, 0))
    q_spec = pl.BlockSpec((None, None, None, V_HEAD_DIM, tq), lambda bi, hi, qi: (bi, hi, qi, 0, 0))
    return pl.pallas_call(
        functools.partial(_attn_kernel, lambda_init),
        grid=(b, N_HEADS, nt),
        in_specs=[
            q_spec,
            pl.BlockSpec((None, s, V_HEAD_DIM), lambda bi, hi, qi: (bi, 0, hi)),
            pl.BlockSpec((None, None, nt, V_HEAD_DIM, tq), lambda bi, hi, qi: (bi, hi, 0, 0, 0)),
            q_spec,
            row64, row64, row64, row64,
            pl.BlockSpec((V_HEAD_DIM, 1), lambda bi, hi, qi: (0, 0)),
        ],
        out_specs=pl.BlockSpec((None, tq, V_HEAD_DIM), lambda bi, hi, qi: (bi, qi, hi)),
        out_shape=jax.ShapeDtypeStruct((b, s, ATTN_WIDTH), BF16),
        scratch_shapes=[
            pltpu.VMEM((2, V_HEAD_DIM, tq), F32),
            pltpu.VMEM((2, 1, tq), F32),
            pltpu.VMEM((2, 1, tq), F32),
        ],
        compiler_params=_compiler_params(3),
        name="diff_attention",
    )(qt, k, vt, zat, lq1.reshape(1, HEAD_DIM), lk1.reshape(1, HEAD_DIM),
      lq2.reshape(1, HEAD_DIM), lk2.reshape(1, HEAD_DIM), subln_w.reshape(V_HEAD_DIM, 1))


def _out_kernel(x_ref, ys_ref, ya_ref, p_ref, wos_ref, woa_ref, wg_ref, wp_ref, o_ref):
    x2 = (x_ref[...]
          + jnp.dot(ys_ref[...], wos_ref[...], preferred_element_type=F32)
          + jnp.dot(ya_ref[...], woa_ref[...], preferred_element_type=F32))
    gate = jax.nn.sigmoid(jnp.dot(x2.astype(BF16), wg_ref[...], preferred_element_type=F32))
    ple = jnp.dot(p_ref[...].astype(BF16), wp_ref[...], preferred_element_type=F32)
    o_ref[...] = x2 + gate * ple


def _out_proj(x, ys_tm, ya, p, w_out, w_proj, w_gate):
    b, s, d = x.shape
    tm = TOKEN_TILE
    const2 = lambda bi, ti: (0, 0)
    return pl.pallas_call(
        _out_kernel,
        grid=(b, s // tm),
        in_specs=[
            pl.BlockSpec((None, tm, d), lambda bi, ti: (bi, ti, 0)),
            pl.BlockSpec((tm, SSM_WIDTH), lambda bi, ti: (ti, bi)),
            pl.BlockSpec((None, tm, ATTN_WIDTH), lambda bi, ti: (bi, ti, 0)),
            pl.BlockSpec((None, tm, PLE_DIM), lambda bi, ti: (bi, ti, 0)),
            pl.BlockSpec((SSM_WIDTH, d), const2),
            pl.BlockSpec((ATTN_WIDTH, d), const2),
            pl.BlockSpec((d, d), const2),
            pl.BlockSpec((PLE_DIM, d), const2),
        ],
        out_specs=pl.BlockSpec((None, tm, d), lambda bi, ti: (bi, ti, 0)),
        out_shape=jax.ShapeDtypeStruct((b, s, d), F32),
        compiler_params=_compiler_params(2),
        name="out_proj",
    )(x, ys_tm, ya, p, w_out[:SSM_WIDTH].astype(BF16), w_out[SSM_WIDTH:].astype(BF16),
      w_gate.astype(BF16), w_proj.astype(BF16))


def kernel(x, p, positions, norm_w, w_in, ssm_lambda_re, ssm_lambda_im, ssm_log_dt, ssm_b_re, ssm_b_im, ssm_c_re, ssm_c_im, ssm_d, glu_w, glu_b, q_norm_w, k_norm_w, lambda_q1, lambda_k1, lambda_q2, lambda_k2, subln_w, w_out, ple_w_proj, ple_w_gate):
    b, s, _ = x.shape
    depth = norm_w.shape[0]
    for i in range(depth):
        lambda_init = 0.8 - 0.6 * math.exp(-0.3 * i)
        a_re, a_im, bb_re_t, bb_im_t = _ssm_params(
            ssm_lambda_re[i], ssm_lambda_im[i], ssm_log_dt[i], ssm_b_re[i], ssm_b_im[i])
        u_tm, zs_tm, qt, k, vt, zat = _in_proj(x, positions, norm_w[i], w_in[i], q_norm_w[i], k_norm_w[i])
        ys_tm = _ssm_branch(u_tm.reshape(s * b, SSM_WIDTH), zs_tm.reshape(s * b, SSM_WIDTH),
                            a_re, a_im, bb_re_t, bb_im_t, ssm_c_re[i], ssm_c_im[i],
                            ssm_d[i], glu_w[i], glu_b[i])
        ya = _attention(qt, k, vt, zat, lambda_q1[i], lambda_k1[i], lambda_q2[i], lambda_k2[i],
                        subln_w[i], lambda_init)
        x = _out_proj(x, ys_tm.reshape(s, b * SSM_WIDTH), ya, p[i], w_out[i], ple_w_proj[i], ple_w_gate[i])
    return x
```

```python
import functools
import math

import jax
import jax.numpy as jnp
from jax import lax
from jax.experimental import pallas as pl
from jax.experimental.pallas import tpu as pltpu

F32 = jnp.float32
BF16 = jnp.bfloat16

D_MODEL = 1024
PLE_DIM = 256
SSM_WIDTH = 512
SSM_GROUP = 16
SSM_GROUPS = 32
SSM_STATE = 64
ATTN_WIDTH = 512
N_HEADS = 4
HEAD_DIM = 64
V_HEAD_DIM = 128
ROT_DIM = 16
ROT_HALF = 8
ROPE_THETA = 500000.0
EPS = 1e-6
LOG2E = math.log2(math.e)

V7X_LANES = 128
V7X_SUBLANES = 8
V7X_VMEM_BYTES = 64 * 1024 * 1024
VMEM_LIMIT_BYTES = V7X_VMEM_BYTES * 7 // 8

TOKEN_TILE = 512
SSM_STEPS = 64
GROUPS_PER_BLOCK = 8
N_SSM_BLOCKS = SSM_GROUPS // GROUPS_PER_BLOCK
BLOCK_STATE = GROUPS_PER_BLOCK * SSM_STATE
MASK_VALUE = -1e30


def _compiler_params(n_grid_axes):
    return pltpu.CompilerParams(
        dimension_semantics=("arbitrary",) * n_grid_axes,
        vmem_limit_bytes=VMEM_LIMIT_BYTES,
    )


def _ssm_params_kernel(lr_ref, li_ref, logdt_ref, br_ref, bi_ref,
                       ar_ref, ai_ref, bbr_ref, bbi_ref):
    lr = lr_ref[...]
    li = li_ref[...]
    dt = jnp.exp(logdt_ref[...])
    mag = jnp.exp(lr * dt)
    ab_re = mag * jnp.cos(li * dt)
    ab_im = mag * jnp.sin(li * dt)
    nr = ab_re - 1.0
    ni = ab_im
    den = lr * lr + li * li
    coef_re = (nr * lr + ni * li) / den
    coef_im = (ni * lr - nr * li) / den
    br = br_ref[...]
    bi = bi_ref[...]
    ar_ref[...] = ab_re
    ai_ref[...] = ab_im
    bbr_ref[...] = coef_re * br - coef_im * bi
    bbi_ref[...] = coef_re * bi + coef_im * br


def _ssm_params(lam_re, lam_im, log_dt, b_re, b_im):
    g, n, p = SSM_GROUPS, SSM_STATE, SSM_GROUP
    gn = jax.ShapeDtypeStruct((g, 1, n), F32)
    gpn = jax.ShapeDtypeStruct((g, p, n), F32)
    return pl.pallas_call(
        _ssm_params_kernel,
        out_shape=(gn, gn, gpn, gpn),
        name="ssm_params",
    )(lam_re.reshape(g, 1, n), lam_im.reshape(g, 1, n), log_dt.reshape(g, 1, 1),
      jnp.swapaxes(b_re, 1, 2), jnp.swapaxes(b_im, 1, 2))


def _in_proj_kernel(x_ref, nw_ref, wnat_ref, wtr_ref, pos_ref, invf_ref, qnw_ref, knw_ref,
                    u_ref, zs_ref, qt_ref, k_ref, vt_ref, zat_ref):
    x = x_ref[...]
    ms = jnp.mean(x * x, axis=-1, keepdims=True)
    h = (x * lax.rsqrt(ms + EPS) * nw_ref[...]).astype(BF16)
    nat = jnp.dot(h, wnat_ref[...], preferred_element_type=F32)
    u_ref[...] = nat[:, :SSM_WIDTH].astype(BF16)
    zs_ref[...] = nat[:, SSM_WIDTH:].astype(BF16)
    tr = lax.dot_general(wtr_ref[...], h, (((1,), (1,)), ((), ())),
                         preferred_element_type=F32)
    tm = x.shape[0]
    ang = pos_ref[...].astype(F32) * invf_ref[...]
    cos = jnp.cos(ang)
    sin = jnp.sin(ang)

    def norm_rope(t, w_ref):
        t3 = t.reshape(2 * N_HEADS, HEAD_DIM, tm)
        ms3 = jnp.mean(t3 * t3, axis=1, keepdims=True)
        t3 = t3 * lax.rsqrt(ms3 + EPS) * w_ref[...]
        t1 = t3[:, :ROT_HALF, :]
        t2 = t3[:, ROT_HALF:ROT_DIM, :]
        r1 = t1 * cos - t2 * sin
        r2 = t2 * cos + t1 * sin
        t3 = jnp.concatenate([r1, r2, t3[:, ROT_DIM:, :]], axis=1)
        return t3.reshape(ATTN_WIDTH, tm)

    q = norm_rope(tr[0:ATTN_WIDTH], qnw_ref) * (HEAD_DIM ** -0.5 * LOG2E)
    qt_ref[...] = q.reshape(N_HEADS, 2 * HEAD_DIM, tm).astype(BF16)
    k = norm_rope(tr[ATTN_WIDTH:2 * ATTN_WIDTH], knw_ref)
    k_ref[...] = k.T.astype(BF16)
    vt_ref[...] = tr[2 * ATTN_WIDTH:3 * ATTN_WIDTH].reshape(N_HEADS, V_HEAD_DIM, tm).astype(BF16)
    zat_ref[...] = tr[3 * ATTN_WIDTH:].reshape(N_HEADS, V_HEAD_DIM, tm).astype(BF16)


def _in_proj(x, positions, norm_w, w_in, q_norm_w, k_norm_w):
    b, s, d = x.shape
    tm = TOKEN_TILE
    nt = s // tm
    w_nat = w_in[:, :2 * SSM_WIDTH].astype(BF16)
    w_tr = w_in[:, 2 * SSM_WIDTH:].T.astype(BF16)
    inv_freq = ROPE_THETA ** (-jnp.arange(0, ROT_DIM, 2, dtype=F32) / ROT_DIM)
    head_major = jax.ShapeDtypeStruct((b, N_HEADS, nt, V_HEAD_DIM, tm), BF16)
    head_spec = pl.BlockSpec((None, N_HEADS, None, V_HEAD_DIM, tm), lambda bi, ti: (bi, 0, ti, 0, 0))
    ssm_shape = jax.ShapeDtypeStruct((b, s, SSM_WIDTH), BF16)
    ssm_spec = pl.BlockSpec((None, tm, SSM_WIDTH), lambda bi, ti: (bi, ti, 0))
    const2 = lambda bi, ti: (0, 0)
    const3 = lambda bi, ti: (0, 0, 0)
    return pl.pallas_call(
        _in_proj_kernel,
        grid=(b, nt),
        in_specs=[
            pl.BlockSpec((None, tm, d), lambda bi, ti: (bi, ti, 0)),
            pl.BlockSpec((1, d), const2),
            pl.BlockSpec((d, 2 * SSM_WIDTH), const2),
            pl.BlockSpec((4 * ATTN_WIDTH, d), const2),
            pl.BlockSpec((None, 1, tm), lambda bi, ti: (bi, 0, ti)),
            pl.BlockSpec((ROT_HALF, 1), const2),
            pl.BlockSpec((1, HEAD_DIM, 1), const3),
            pl.BlockSpec((1, HEAD_DIM, 1), const3),
        ],
        out_specs=[
            ssm_spec, ssm_spec,
            head_spec,
            pl.BlockSpec((None, tm, ATTN_WIDTH), lambda bi, ti: (bi, ti, 0)),
            head_spec, head_spec,
        ],
        out_shape=(ssm_shape, ssm_shape, head_major,
                   jax.ShapeDtypeStruct((b, s, ATTN_WIDTH), BF16), head_major, head_major),
        compiler_params=_compiler_params(2),
        name="in_proj",
    )(x, norm_w.reshape(1, d), w_nat, w_tr, positions.reshape(b, 1, s),
      inv_freq.reshape(ROT_HALF, 1), q_norm_w.reshape(1, HEAD_DIM, 1), k_norm_w.reshape(1, HEAD_DIM, 1))


def _ssm_kernel(u_ref, zs_ref, perm_ref, permt_ref, bmat_ref, cmat_ref, ar_ref, ai_ref, d_ref,
                gw_ref, gb_ref, y_ref, ut_ref, bu_ref, xs_ref, state_ref):
    n_batch, steps, width = u_ref.shape
    rows = n_batch * steps
    blk = 2 * BLOCK_STATE

    @pl.when(pl.program_id(0) == 0)
    def _():
        state_ref[...] = jnp.zeros_like(state_ref)

    u_tb = jnp.dot(perm_ref[...], u_ref[...].reshape(rows, width), preferred_element_type=F32)
    ut_ref[...] = u_tb.astype(BF16)

    for m in range(N_SSM_BLOCKS):
        bu_ref[:, m * blk:(m + 1) * blk] = jnp.dot(
            ut_ref[:, m * V7X_LANES:(m + 1) * V7X_LANES], bmat_ref[m], preferred_element_type=F32)

    step_rows = 2 * V7X_SUBLANES
    for m in range(N_SSM_BLOCKS):
        re = slice(m * blk, m * blk + BLOCK_STATE)
        im = slice(m * blk + BLOCK_STATE, (m + 1) * blk)
        a_re = jnp.broadcast_to(ar_ref[:, m * BLOCK_STATE:(m + 1) * BLOCK_STATE], (V7X_SUBLANES, BLOCK_STATE))
        a_im = jnp.broadcast_to(ai_ref[:, m * BLOCK_STATE:(m + 1) * BLOCK_STATE], (V7X_SUBLANES, BLOCK_STATE))

        def body(it, carry, re=re, im=im, a_re=a_re, a_im=a_im):
            x_re, x_im = carry
            r0 = pl.multiple_of(it * step_rows, step_rows)
            outs_re, outs_im = [], []
            for half in range(2):
                rr = pl.ds(r0 + half * V7X_SUBLANES, V7X_SUBLANES)
                n_re = a_re * x_re - a_im * x_im + bu_ref[rr, re]
                n_im = a_re * x_im + a_im * x_re + bu_ref[rr, im]
                x_re, x_im = n_re, n_im
                outs_re.append(n_re)
                outs_im.append(n_im)
            xs_ref[pl.ds(r0, step_rows), re] = jnp.concatenate(outs_re, axis=0).astype(BF16)
            xs_ref[pl.ds(r0, step_rows), im] = jnp.concatenate(outs_im, axis=0).astype(BF16)
            return x_re, x_im

        x_re, x_im = lax.fori_loop(0, rows // step_rows, body, (state_ref[:, re], state_ref[:, im]))
        state_ref[:, re] = x_re
        state_ref[:, im] = x_im

    ys = [jnp.dot(xs_ref[:, m * blk:(m + 1) * blk], cmat_ref[m], preferred_element_type=F32)
          for m in range(N_SSM_BLOCKS)]
    y = jnp.concatenate(ys, axis=-1) + d_ref[...] * u_tb
    y = jax.nn.gelu(y)
    gate = jnp.dot(y.astype(BF16), gw_ref[...], preferred_element_type=F32) + gb_ref[...]
    y = (y * jax.nn.sigmoid(gate)).astype(BF16)
    y_bt = jnp.dot(permt_ref[...], y, preferred_element_type=F32)
    z = zs_ref[...].reshape(rows, width).astype(F32)
    y_ref[...] = (y_bt * jax.nn.silu(z)).astype(BF16).reshape(n_batch, steps, width)


def _ssm_branch(u, zs, a_re, a_im, bb_re_t, bb_im_t, c_re, c_im, d_skip, glu_w, glu_b):
    n_batch, s, _ = u.shape
    assert n_batch == V7X_SUBLANES, "one time step of all batches must fill one sublane tile"
    steps = SSM_STEPS
    rows = steps * n_batch
    gpb, nb, p, n = GROUPS_PER_BLOCK, N_SSM_BLOCKS, SSM_GROUP, SSM_STATE
    r = jnp.arange(rows)
    perm = (jnp.arange(rows)[None, :] == ((r % n_batch) * steps + r // n_batch)[:, None]).astype(BF16)
    eye = jnp.eye(gpb, dtype=F32)
    bb = jnp.stack([bb_re_t, bb_im_t]).reshape(2, nb, gpb, p, n)
    bmat = jnp.einsum('rmgpn,gh->mgprhn', bb, eye).reshape(nb, gpb * p, 2 * BLOCK_STATE).astype(BF16)
    cc = jnp.stack([c_re, -c_im]).reshape(2, nb, gpb, p, n)
    cmat = jnp.einsum('rmgpn,gh->mrgnhp', cc, eye).reshape(nb, 2 * BLOCK_STATE, gpb * p).astype(BF16)
    row_spec = pl.BlockSpec((n_batch, steps, SSM_WIDTH), lambda i: (0, i, 0))
    const2 = lambda i: (0, 0)
    const3 = lambda i: (0, 0, 0)
    n_state = SSM_GROUPS * SSM_STATE
    return pl.pallas_call(
        _ssm_kernel,
        grid=(s // steps,),
        in_specs=[
            row_spec, row_spec,
            pl.BlockSpec((rows, rows), const2),
            pl.BlockSpec((rows, rows), const2),
            pl.BlockSpec((nb, gpb * p, 2 * BLOCK_STATE), const3),
            pl.BlockSpec((nb, 2 * BLOCK_STATE, gpb * p), const3),
            pl.BlockSpec((1, n_state), const2),
            pl.BlockSpec((1, n_state), const2),
            pl.BlockSpec((1, SSM_WIDTH), const2),
            pl.BlockSpec((SSM_WIDTH, SSM_WIDTH), const2),
            pl.BlockSpec((1, SSM_WIDTH), const2),
        ],
        out_specs=row_spec,
        out_shape=jax.ShapeDtypeStruct((n_batch, s, SSM_WIDTH), BF16),
        scratch_shapes=[
            pltpu.VMEM((rows, SSM_WIDTH), BF16),
            pltpu.VMEM((rows, 2 * n_state), F32),
            pltpu.VMEM((rows, 2 * n_state), BF16),
            pltpu.VMEM((V7X_SUBLANES, 2 * n_state), F32),
        ],
        compiler_params=_compiler_params(1),
        name="ssm_branch",
    )(u, zs, perm, perm.T, bmat, cmat, a_re.reshape(1, n_state), a_im.reshape(1, n_state),
      d_skip.reshape(1, SSM_WIDTH), glu_w.astype(BF16), glu_b.reshape(1, SSM_WIDTH))


def _attn_kernel(lambda_init, qt_ref, k_ref, vt_ref, zat_ref, lq1_ref, lk1_ref, lq2_ref, lk2_ref,
                 subw_ref, y_ref, wq_ref, s_ref, mx_ref, acc_ref, m_ref, l_ref):
    qi = pl.program_id(2)
    tq = qt_ref.shape[-1]
    tk = tq
    zeros = jnp.zeros((HEAD_DIM, tq), BF16)
    wq_ref[0, :HEAD_DIM] = qt_ref[:HEAD_DIM]
    wq_ref[0, HEAD_DIM:] = zeros
    wq_ref[1, :HEAD_DIM] = zeros
    wq_ref[1, HEAD_DIM:] = qt_ref[HEAD_DIM:]

    m_ref[...] = jnp.full_like(m_ref, MASK_VALUE)
    l_ref[...] = jnp.zeros_like(l_ref)
    acc_ref[...] = jnp.zeros_like(acc_ref)

    def scores(j, slot, masked):
        kb = k_ref[pl.ds(pl.multiple_of(j * tk, tk), tk), :]
        for c in range(2):
            s = jnp.dot(kb, wq_ref[c], preferred_element_type=F32)
            if masked:
                key_idx = lax.broadcasted_iota(jnp.int32, (tk, tq), 0)
                qry_idx = lax.broadcasted_iota(jnp.int32, (tk, tq), 1)
                s = jnp.where(key_idx <= qry_idx, s, MASK_VALUE)
            s_ref[slot, c] = s
            mx_ref[slot, c] = jnp.max(s, axis=0, keepdims=True)

    def accumulate(j, slot):
        vb = vt_ref[j]
        for c in range(2):
            m_old = m_ref[c]
            m_new = jnp.maximum(m_old, mx_ref[slot, c])
            alpha = jnp.exp2(m_old - m_new)
            p = jnp.exp2(s_ref[slot, c] - m_new)
            l_ref[c] = alpha * l_ref[c] + jnp.sum(p, axis=0, keepdims=True)
            acc_ref[c] = alpha * acc_ref[c] + jnp.dot(vb, p.astype(BF16), preferred_element_type=F32)
            m_ref[c] = m_new

    scores(qi, 0, masked=True)

    def pair(i, carry):
        j0 = 2 * i
        scores(j0, 1, masked=False)
        accumulate(jnp.where(i == 0, qi, j0 - 1), 0)
        scores(j0 + 1, 0, masked=False)
        accumulate(j0, 1)
        return carry

    n_pairs = qi // 2
    lax.fori_loop(0, n_pairs, pair, 0)
    in_slot0 = jnp.where(n_pairs == 0, qi, 2 * n_pairs - 1)

    @pl.when(qi % 2 == 1)
    def _():
        scores(qi - 1, 1, masked=False)
        accumulate(in_slot0, 0)
        accumulate(qi - 1, 1)

    @pl.when(qi % 2 == 0)
    def _():
        accumulate(in_slot0, 0)

    lam = (jnp.exp(jnp.sum(lq1_ref[...] * lk1_ref[...], axis=-1, keepdims=True))
           - jnp.exp(jnp.sum(lq2_ref[...] * lk2_ref[...], axis=-1, keepdims=True)) + lambda_init)
    out = acc_ref[0] / l_ref[0] - lam * (acc_ref[1] / l_ref[1])
    ms = jnp.mean(out * out, axis=0, keepdims=True)
    out = out * lax.rsqrt(ms + EPS) * subw_ref[...] * (1.0 - lambda_init)
    out = out * jax.nn.silu(zat_ref[...].astype(F32))
    y_ref[...] = out.T.astype(BF16)


def _attention(qt, k, vt, zat, lq1, lk1, lq2, lk2, subln_w, lambda_init):
    b, _, nt, _, tq = qt.shape
    s = nt * tq
    row64 = pl.BlockSpec((1, HEAD_DIM), lambda bi, hi, qi: (0, 0))
    q_spec = pl.BlockSpec((None, None, None, V_HEAD_DIM, tq), lambda bi, hi, qi: (bi, hi, qi, 0, 0))
    return pl.pallas_call(
        functools.partial(_attn_kernel, lambda_init),
        grid=(b, N_HEADS, nt),
        in_specs=[
            q_spec,
            pl.BlockSpec((None, s, V_HEAD_DIM), lambda bi, hi, qi: (bi, 0, hi)),
            pl.BlockSpec((None, None, nt, V_HEAD_DIM, tq), lambda bi, hi, qi: (bi, hi, 0, 0, 0)),
            q_spec,
            row64, row64, row64, row64,
            pl.BlockSpec((V_HEAD_DIM, 1), lambda bi, hi, qi: (0, 0)),
        ],
        out_specs=pl.BlockSpec((None, tq, V_HEAD_DIM), lambda bi, hi, qi: (bi, qi, hi)),
        out_shape=jax.ShapeDtypeStruct((b, s, ATTN_WIDTH), BF16),
        scratch_shapes=[
            pltpu.VMEM((2, 2 * HEAD_DIM, tq), BF16),
            pltpu.VMEM((2, 2, tq, tq), F32),
            pltpu.VMEM((2, 2, 1, tq), F32),
            pltpu.VMEM((2, V_HEAD_DIM, tq), F32),
            pltpu.VMEM((2, 1, tq), F32),
            pltpu.VMEM((2, 1, tq), F32),
        ],
        compiler_params=_compiler_params(3),
        name="diff_attention",
    )(qt, k, vt, zat, lq1.reshape(1, HEAD_DIM), lk1.reshape(1, HEAD_DIM),
      lq2.reshape(1, HEAD_DIM), lk2.reshape(1, HEAD_DIM), subln_w.reshape(V_HEAD_DIM, 1))


def _out_kernel(x_ref, ys_ref, ya_ref, p_ref, wos_ref, woa_ref, wg_ref, wp_ref, o_ref):
    x2 = (x_ref[...]
          + jnp.dot(ys_ref[...], wos_ref[...], preferred_element_type=F32)
          + jnp.dot(ya_ref[...], woa_ref[...], preferred_element_type=F32))
    gate = jax.nn.sigmoid(jnp.dot(x2.astype(BF16), wg_ref[...], preferred_element_type=F32))
    ple = jnp.dot(p_ref[...].astype(BF16), wp_ref[...], preferred_element_type=F32)
    o_ref[...] = x2 + gate * ple


def _out_proj(x, ys_tm, ya, p, w_out, w_proj, w_gate):
    b, s, d = x.shape
    tm = TOKEN_TILE
    const2 = lambda bi, ti: (0, 0)
    return pl.pallas_call(
        _out_kernel,
        grid=(b, s // tm),
        in_specs=[
            pl.BlockSpec((None, tm, d), lambda bi, ti: (bi, ti, 0)),
            pl.BlockSpec((None, tm, SSM_WIDTH), lambda bi, ti: (bi, ti, 0)),
            pl.BlockSpec((None, tm, ATTN_WIDTH), lambda bi, ti: (bi, ti, 0)),
            pl.BlockSpec((None, tm, PLE_DIM), lambda bi, ti: (bi, ti, 0)),
            pl.BlockSpec((SSM_WIDTH, d), const2),
            pl.BlockSpec((ATTN_WIDTH, d), const2),
            pl.BlockSpec((d, d), const2),
            pl.BlockSpec((PLE_DIM, d), const2),
        ],
        out_specs=pl.BlockSpec((None, tm, d), lambda bi, ti: (bi, ti, 0)),
        out_shape=jax.ShapeDtypeStruct((b, s, d), F32),
        compiler_params=_compiler_params(2),
        name="out_proj",
    )(x, ys_tm, ya, p, w_out[:SSM_WIDTH].astype(BF16), w_out[SSM_WIDTH:].astype(BF16),
      w_gate.astype(BF16), w_proj.astype(BF16))


def kernel(x, p, positions, norm_w, w_in, ssm_lambda_re, ssm_lambda_im, ssm_log_dt, ssm_b_re, ssm_b_im, ssm_c_re, ssm_c_im, ssm_d, glu_w, glu_b, q_norm_w, k_norm_w, lambda_q1, lambda_k1, lambda_q2, lambda_k2, subln_w, w_out, ple_w_proj, ple_w_gate):
    b, s, _ = x.shape
    depth = norm_w.shape[0]
    for i in range(depth):
        lambda_init = 0.8 - 0.6 * math.exp(-0.3 * i)
        a_re, a_im, bb_re_t, bb_im_t = _ssm_params(
            ssm_lambda_re[i], ssm_lambda_im[i], ssm_log_dt[i], ssm_b_re[i], ssm_b_im[i])
        u, zs, qt, k, vt, zat = _in_proj(x, positions, norm_w[i], w_in[i], q_norm_w[i], k_norm_w[i])
        ys = _ssm_branch(u, zs, a_re, a_im, bb_re_t, bb_im_t, ssm_c_re[i], ssm_c_im[i],
                         ssm_d[i], glu_w[i], glu_b[i])
        ya = _attention(qt, k, vt, zat, lambda_q1[i], lambda_k1[i], lambda_q2[i], lambda_k2[i],
                        subln_w[i], lambda_init)
        x = _out_proj(x, ys, ya, p[i], w_out[i], ple_w_proj[i], ple_w_gate[i])
    return x
```

```python
import functools
import math

import jax
import jax.numpy as jnp
from jax import lax
from jax.experimental import pallas as pl
from jax.experimental.pallas import tpu as pltpu

F32 = jnp.float32
BF16 = jnp.bfloat16

D_MODEL = 1024
PLE_DIM = 256
SSM_WIDTH = 512
SSM_GROUP = 16
SSM_GROUPS = 32
SSM_STATE = 64
ATTN_WIDTH = 512
N_HEADS = 4
HEAD_DIM = 64
V_HEAD_DIM = 128
ROT_DIM = 16
ROT_HALF = 8
ROPE_THETA = 500000.0
EPS = 1e-6
LOG2E = math.log2(math.e)

V7X_LANES = 128
V7X_SUBLANES = 8
V7X_VMEM_BYTES = 64 * 1024 * 1024
VMEM_LIMIT_BYTES = V7X_VMEM_BYTES * 7 // 8

TOKEN_TILE = 512
SSM_STEPS = 64
SSM_SUB_STEPS = 32
GROUPS_PER_BLOCK = 8
N_SSM_BLOCKS = SSM_GROUPS // GROUPS_PER_BLOCK
BLOCK_STATE = GROUPS_PER_BLOCK * SSM_STATE
MASK_VALUE = -1e30


def _compiler_params(n_grid_axes):
    return pltpu.CompilerParams(
        dimension_semantics=("arbitrary",) * n_grid_axes,
        vmem_limit_bytes=VMEM_LIMIT_BYTES,
    )


def _ssm_params_kernel(lr_ref, li_ref, logdt_ref, br_ref, bi_ref,
                       ar_ref, ai_ref, bbr_ref, bbi_ref):
    lr = lr_ref[...]
    li = li_ref[...]
    dt = jnp.exp(logdt_ref[...])
    mag = jnp.exp(lr * dt)
    ab_re = mag * jnp.cos(li * dt)
    ab_im = mag * jnp.sin(li * dt)
    nr = ab_re - 1.0
    ni = ab_im
    den = lr * lr + li * li
    coef_re = (nr * lr + ni * li) / den
    coef_im = (ni * lr - nr * li) / den
    br = br_ref[...]
    bi = bi_ref[...]
    ar_ref[...] = ab_re
    ai_ref[...] = ab_im
    bbr_ref[...] = coef_re * br - coef_im * bi
    bbi_ref[...] = coef_re * bi + coef_im * br


def _ssm_params(lam_re, lam_im, log_dt, b_re, b_im):
    g, n, p = SSM_GROUPS, SSM_STATE, SSM_GROUP
    gn = jax.ShapeDtypeStruct((g, 1, n), F32)
    gpn = jax.ShapeDtypeStruct((g, p, n), F32)
    return pl.pallas_call(
        _ssm_params_kernel,
        out_shape=(gn, gn, gpn, gpn),
        name="ssm_params",
    )(lam_re.reshape(g, 1, n), lam_im.reshape(g, 1, n), log_dt.reshape(g, 1, 1),
      jnp.swapaxes(b_re, 1, 2), jnp.swapaxes(b_im, 1, 2))


def _in_proj_kernel(x_ref, nw_ref, wnat_ref, wtr_ref, pos_ref, invf_ref, qnw_ref, knw_ref,
                    u_ref, zs_ref, qt_ref, k_ref, vt_ref, zat_ref):
    x = x_ref[...]
    ms = jnp.mean(x * x, axis=-1, keepdims=True)
    h = (x * lax.rsqrt(ms + EPS) * nw_ref[...]).astype(BF16)
    nat = jnp.dot(h, wnat_ref[...], preferred_element_type=F32)
    u_ref[...] = nat[:, :SSM_WIDTH].astype(BF16)
    zs_ref[...] = nat[:, SSM_WIDTH:].astype(BF16)
    tr = lax.dot_general(wtr_ref[...], h, (((1,), (1,)), ((), ())),
                         preferred_element_type=F32)
    tm = x.shape[0]
    ang = pos_ref[...].astype(F32) * invf_ref[...]
    cos = jnp.cos(ang)
    sin = jnp.sin(ang)

    def norm_rope(t, w_ref):
        t3 = t.reshape(2 * N_HEADS, HEAD_DIM, tm)
        ms3 = jnp.mean(t3 * t3, axis=1, keepdims=True)
        t3 = t3 * lax.rsqrt(ms3 + EPS) * w_ref[...]
        t1 = t3[:, :ROT_HALF, :]
        t2 = t3[:, ROT_HALF:ROT_DIM, :]
        r1 = t1 * cos - t2 * sin
        r2 = t2 * cos + t1 * sin
        t3 = jnp.concatenate([r1, r2, t3[:, ROT_DIM:, :]], axis=1)
        return t3.reshape(ATTN_WIDTH, tm)

    q = norm_rope(tr[0:ATTN_WIDTH], qnw_ref) * (HEAD_DIM ** -0.5 * LOG2E)
    qt_ref[...] = q.reshape(N_HEADS, 2 * HEAD_DIM, tm).astype(BF16)
    k = norm_rope(tr[ATTN_WIDTH:2 * ATTN_WIDTH], knw_ref)
    k_ref[...] = k.T.astype(BF16)
    vt_ref[...] = tr[2 * ATTN_WIDTH:3 * ATTN_WIDTH].reshape(N_HEADS, V_HEAD_DIM, tm).astype(BF16)
    zat_ref[...] = tr[3 * ATTN_WIDTH:].reshape(N_HEADS, V_HEAD_DIM, tm).astype(BF16)


def _in_proj(x, positions, norm_w, w_in, q_norm_w, k_norm_w):
    b, s, d = x.shape
    tm = TOKEN_TILE
    nt = s // tm
    w_nat = w_in[:, :2 * SSM_WIDTH].astype(BF16)
    w_tr = w_in[:, 2 * SSM_WIDTH:].T.astype(BF16)
    inv_freq = ROPE_THETA ** (-jnp.arange(0, ROT_DIM, 2, dtype=F32) / ROT_DIM)
    head_major = jax.ShapeDtypeStruct((b, N_HEADS, nt, V_HEAD_DIM, tm), BF16)
    head_spec = pl.BlockSpec((None, N_HEADS, None, V_HEAD_DIM, tm), lambda bi, ti: (bi, 0, ti, 0, 0))
    ssm_shape = jax.ShapeDtypeStruct((b, s, SSM_WIDTH), BF16)
    ssm_spec = pl.BlockSpec((None, tm, SSM_WIDTH), lambda bi, ti: (bi, ti, 0))
    const2 = lambda bi, ti: (0, 0)
    const3 = lambda bi, ti: (0, 0, 0)
    return pl.pallas_call(
        _in_proj_kernel,
        grid=(b, nt),
        in_specs=[
            pl.BlockSpec((None, tm, d), lambda bi, ti: (bi, ti, 0)),
            pl.BlockSpec((1, d), const2),
            pl.BlockSpec((d, 2 * SSM_WIDTH), const2),
            pl.BlockSpec((4 * ATTN_WIDTH, d), const2),
            pl.BlockSpec((None, 1, tm), lambda bi, ti: (bi, 0, ti)),
            pl.BlockSpec((ROT_HALF, 1), const2),
            pl.BlockSpec((1, HEAD_DIM, 1), const3),
            pl.BlockSpec((1, HEAD_DIM, 1), const3),
        ],
        out_specs=[
            ssm_spec, ssm_spec,
            head_spec,
            pl.BlockSpec((None, tm, ATTN_WIDTH), lambda bi, ti: (bi, ti, 0)),
            head_spec, head_spec,
        ],
        out_shape=(ssm_shape, ssm_shape, head_major,
                   jax.ShapeDtypeStruct((b, s, ATTN_WIDTH), BF16), head_major, head_major),
        compiler_params=_compiler_params(2),
        name="in_proj",
    )(x, norm_w.reshape(1, d), w_nat, w_tr, positions.reshape(b, 1, s),
      inv_freq.reshape(ROT_HALF, 1), q_norm_w.reshape(1, HEAD_DIM, 1), k_norm_w.reshape(1, HEAD_DIM, 1))


def _ssm_kernel(u_ref, zs_ref, perm_ref, permt_ref, bmat_ref, cmat_ref, ar_ref, ai_ref, d_ref,
                gw_ref, gb_ref, y_ref, ut_ref, bu_ref, xs_ref, ys_ref, state_ref):
    n_batch, steps, width = u_ref.shape
    sub_rows = perm_ref.shape[0]
    sub_steps = sub_rows // n_batch
    n_sub = steps // sub_steps
    blk = 2 * BLOCK_STATE
    step_rows = 2 * V7X_SUBLANES

    @pl.when(pl.program_id(0) == 0)
    def _():
        state_ref[...] = jnp.zeros_like(state_ref)

    def rows_of(sub):
        return slice(sub * sub_rows, (sub + 1) * sub_rows)

    def permute_in(sub):
        u_bt = u_ref[:, sub * sub_steps:(sub + 1) * sub_steps, :].reshape(sub_rows, width)
        ut_ref[rows_of(sub), :] = jnp.dot(perm_ref[...], u_bt, preferred_element_type=F32).astype(BF16)

    def project_in(sub, m):
        bu_ref[rows_of(sub), m * blk:(m + 1) * blk] = jnp.dot(
            ut_ref[rows_of(sub), m * V7X_LANES:(m + 1) * V7X_LANES], bmat_ref[m], preferred_element_type=F32)

    def scan(sub, m):
        re = slice(m * blk, m * blk + BLOCK_STATE)
        im = slice(m * blk + BLOCK_STATE, (m + 1) * blk)
        a_re = jnp.broadcast_to(ar_ref[:, m * BLOCK_STATE:(m + 1) * BLOCK_STATE], (V7X_SUBLANES, BLOCK_STATE))
        a_im = jnp.broadcast_to(ai_ref[:, m * BLOCK_STATE:(m + 1) * BLOCK_STATE], (V7X_SUBLANES, BLOCK_STATE))
        x_re = state_ref[:, re]
        x_im = state_ref[:, im]
        for r0 in range(sub * sub_rows, (sub + 1) * sub_rows, step_rows):
            outs_re, outs_im = [], []
            for half in range(2):
                rr = slice(r0 + half * V7X_SUBLANES, r0 + (half + 1) * V7X_SUBLANES)
                n_re = a_re * x_re - a_im * x_im + bu_ref[rr, re]
                n_im = a_re * x_im + a_im * x_re + bu_ref[rr, im]
                x_re, x_im = n_re, n_im
                outs_re.append(n_re)
                outs_im.append(n_im)
            xs_ref[r0:r0 + step_rows, re] = jnp.concatenate(outs_re, axis=0).astype(BF16)
            xs_ref[r0:r0 + step_rows, im] = jnp.concatenate(outs_im, axis=0).astype(BF16)
        state_ref[:, re] = x_re
        state_ref[:, im] = x_im

    def read_out(sub, m):
        ys_ref[rows_of(sub), m * V7X_LANES:(m + 1) * V7X_LANES] = jnp.dot(
            xs_ref[rows_of(sub), m * blk:(m + 1) * blk], cmat_ref[m], preferred_element_type=F32)

    def finish(sub):
        y = ys_ref[rows_of(sub), :] + d_ref[...] * ut_ref[rows_of(sub), :].astype(F32)
        y = jax.nn.gelu(y)
        gate = jnp.dot(y.astype(BF16), gw_ref[...], preferred_element_type=F32) + gb_ref[...]
        y = (y * jax.nn.sigmoid(gate)).astype(BF16)
        y_bt = jnp.dot(permt_ref[...], y, preferred_element_type=F32)
        t = slice(sub * sub_steps, (sub + 1) * sub_steps)
        z = zs_ref[:, t, :].reshape(sub_rows, width).astype(F32)
        y_ref[:, t, :] = (y_bt * jax.nn.silu(z)).astype(BF16).reshape(n_batch, sub_steps, width)

    pairs = [(2 * k, 2 * k + 1) for k in range(N_SSM_BLOCKS // 2)]
    units = [(sub, pair) for sub in range(n_sub) for pair in pairs]
    always = pl.program_id(0) >= 0
    for stage in range(len(units) + 2):
        @pl.when(always)
        def _(stage=stage):
            nxt, cur, prv = stage, stage - 1, stage - 2
            if nxt < len(units):
                sub, pair = units[nxt]
                if pair == pairs[0]:
                    permute_in(sub)
                for m in pair:
                    project_in(sub, m)
            if 0 <= cur < len(units):
                sub, pair = units[cur]
                for m in pair:
                    scan(sub, m)
            if 0 <= prv < len(units):
                sub, pair = units[prv]
                for m in pair:
                    read_out(sub, m)
                if pair == pairs[-1]:
                    finish(sub)


def _ssm_branch(u, zs, a_re, a_im, bb_re_t, bb_im_t, c_re, c_im, d_skip, glu_w, glu_b):
    n_batch, s, _ = u.shape
    assert n_batch == V7X_SUBLANES, "one time step of all batches must fill one sublane tile"
    steps = SSM_STEPS
    rows = steps * n_batch
    gpb, nb, p, n = GROUPS_PER_BLOCK, N_SSM_BLOCKS, SSM_GROUP, SSM_STATE
    sub_rows = SSM_SUB_STEPS * n_batch
    r = jnp.arange(sub_rows)
    perm = (r[None, :] == ((r % n_batch) * SSM_SUB_STEPS + r // n_batch)[:, None]).astype(BF16)
    eye = jnp.eye(gpb, dtype=F32)
    bb = jnp.stack([bb_re_t, bb_im_t]).reshape(2, nb, gpb, p, n)
    bmat = jnp.einsum('rmgpn,gh->mgprhn', bb, eye).reshape(nb, gpb * p, 2 * BLOCK_STATE).astype(BF16)
    cc = jnp.stack([c_re, -c_im]).reshape(2, nb, gpb, p, n)
    cmat = jnp.einsum('rmgpn,gh->mrgnhp', cc, eye).reshape(nb, 2 * BLOCK_STATE, gpb * p).astype(BF16)
    row_spec = pl.BlockSpec((n_batch, steps, SSM_WIDTH), lambda i: (0, i, 0))
    const2 = lambda i: (0, 0)
    const3 = lambda i: (0, 0, 0)
    n_state = SSM_GROUPS * SSM_STATE
    return pl.pallas_call(
        _ssm_kernel,
        grid=(s // steps,),
        in_specs=[
            row_spec, row_spec,
            pl.BlockSpec((sub_rows, sub_rows), const2),
            pl.BlockSpec((sub_rows, sub_rows), const2),
            pl.BlockSpec((nb, gpb * p, 2 * BLOCK_STATE), const3),
            pl.BlockSpec((nb, 2 * BLOCK_STATE, gpb * p), const3),
            pl.BlockSpec((1, n_state), const2),
            pl.BlockSpec((1, n_state), const2),
            pl.BlockSpec((1, SSM_WIDTH), const2),
            pl.BlockSpec((SSM_WIDTH, SSM_WIDTH), const2),
            pl.BlockSpec((1, SSM_WIDTH), const2),
        ],
        out_specs=row_spec,
        out_shape=jax.ShapeDtypeStruct((n_batch, s, SSM_WIDTH), BF16),
        scratch_shapes=[
            pltpu.VMEM((rows, SSM_WIDTH), BF16),
            pltpu.VMEM((rows, 2 * n_state), F32),
            pltpu.VMEM((rows, 2 * n_state), BF16),
            pltpu.VMEM((rows, SSM_WIDTH), F32),
            pltpu.VMEM((V7X_SUBLANES, 2 * n_state), F32),
        ],
        compiler_params=_compiler_params(1),
        name="ssm_branch",
    )(u, zs, perm, perm.T, bmat, cmat, a_re.reshape(1, n_state), a_im.reshape(1, n_state),
      d_skip.reshape(1, SSM_WIDTH), glu_w.astype(BF16), glu_b.reshape(1, SSM_WIDTH))


def _attn_kernel(lambda_init, qt_ref, k_ref, vt_ref, zat_ref, lq1_ref, lk1_ref, lq2_ref, lk2_ref,
                 subw_ref, y_ref, wq_ref, s_ref, mx_ref, acc_ref, m_ref, l_ref):
    qi = pl.program_id(2)
    tq = qt_ref.shape[-1]
    tk = tq
    zeros = jnp.zeros((HEAD_DIM, tq), BF16)
    wq_ref[0, :HEAD_DIM] = qt_ref[:HEAD_DIM]
    wq_ref[0, HEAD_DIM:] = zeros
    wq_ref[1, :HEAD_DIM] = zeros
    wq_ref[1, HEAD_DIM:] = qt_ref[HEAD_DIM:]

    m_ref[...] = jnp.full_like(m_ref, MASK_VALUE)
    l_ref[...] = jnp.zeros_like(l_ref)
    acc_ref[...] = jnp.zeros_like(acc_ref)

    def scores(j, slot, masked):
        kb = k_ref[pl.ds(pl.multiple_of(j * tk, tk), tk), :]
        for c in range(2):
            s = jnp.dot(kb, wq_ref[c], preferred_element_type=F32)
            if masked:
                key_idx = lax.broadcasted_iota(jnp.int32, (tk, tq), 0)
                qry_idx = lax.broadcasted_iota(jnp.int32, (tk, tq), 1)
                s = jnp.where(key_idx <= qry_idx, s, MASK_VALUE)
            s_ref[slot, c] = s
            mx_ref[slot, c] = jnp.max(s, axis=0, keepdims=True)

    def accumulate(j, slot):
        vb = vt_ref[j]
        for c in range(2):
            m_old = m_ref[c]
            m_new = jnp.maximum(m_old, mx_ref[slot, c])
            alpha = jnp.exp2(m_old - m_new)
            p = jnp.exp2(s_ref[slot, c] - m_new)
            l_ref[c] = alpha * l_ref[c] + jnp.sum(p, axis=0, keepdims=True)
            acc_ref[c] = alpha * acc_ref[c] + jnp.dot(vb, p.astype(BF16), preferred_element_type=F32)
            m_ref[c] = m_new

    scores(qi, 0, masked=True)

    def pair(i, carry):
        j0 = 2 * i
        scores(j0, 1, masked=False)
        accumulate(jnp.where(i == 0, qi, j0 - 1), 0)
        scores(j0 + 1, 0, masked=False)
        accumulate(j0, 1)
        return carry

    n_pairs = qi // 2
    lax.fori_loop(0, n_pairs, pair, 0)
    in_slot0 = jnp.where(n_pairs == 0, qi, 2 * n_pairs - 1)

    @pl.when(qi % 2 == 1)
    def _():
        scores(qi - 1, 1, masked=False)
        accumulate(in_slot0, 0)
        accumulate(qi - 1, 1)

    @pl.when(qi % 2 == 0)
    def _():
        accumulate(in_slot0, 0)

    lam = (jnp.exp(jnp.sum(lq1_ref[...] * lk1_ref[...], axis=-1, keepdims=True))
           - jnp.exp(jnp.sum(lq2_ref[...] * lk2_ref[...], axis=-1, keepdims=True)) + lambda_init)
    out = acc_ref[0] / l_ref[0] - lam * (acc_ref[1] / l_ref[1])
    ms = jnp.mean(out * out, axis=0, keepdims=True)
    out = out * lax.rsqrt(ms + EPS) * subw_ref[...] * (1.0 - lambda_init)
    out = out * jax.nn.silu(zat_ref[...].astype(F32))
    y_ref[...] = out.T.astype(BF16)


def _attention(qt, k, vt, zat, lq1, lk1, lq2, lk2, subln_w, lambda_init):
    b, _, nt, _, tq = qt.shape
    s = nt * tq
    row64 = pl.BlockSpec((1, HEAD_DIM), lambda bi, hi, qi: (0, 0))
    q_spec = pl.BlockSpec((None, None, None, V_HEAD_DIM, tq), lambda bi, hi, qi: (bi, hi, qi, 0, 0))
    return pl.pallas_call(
        functools.partial(_attn_kernel, lambda_init),
        grid=(b, N_HEADS, nt),
        in_specs=[
            q_spec,
            pl.BlockSpec((None, s, V_HEAD_DIM), lambda bi, hi, qi: (bi, 0, hi)),
            pl.BlockSpec((None, None, nt, V_HEAD_DIM, tq), lambda bi, hi, qi: (bi, hi, 0, 0, 0)),
            q_spec,
            row64, row64, row64, row64,
            pl.BlockSpec((V_HEAD_DIM, 1), lambda bi, hi, qi: (0, 0)),
        ],
        out_specs=pl.BlockSpec((None, tq, V_HEAD_DIM), lambda bi, hi, qi: (bi, qi, hi)),
        out_shape=jax.ShapeDtypeStruct((b, s, ATTN_WIDTH), BF16),
        scratch_shapes=[
            pltpu.VMEM((2, 2 * HEAD_DIM, tq), BF16),
            pltpu.VMEM((2, 2, tq, tq), F32),
            pltpu.VMEM((2, 2, 1, tq), F32),
            pltpu.VMEM((2, V_HEAD_DIM, tq), F32),
            pltpu.VMEM((2, 1, tq), F32),
            pltpu.VMEM((2, 1, tq), F32),
        ],
        compiler_params=_compiler_params(3),
        name="diff_attention",
    )(qt, k, vt, zat, lq1.reshape(1, HEAD_DIM), lk1.reshape(1, HEAD_DIM),
      lq2.reshape(1, HEAD_DIM), lk2.reshape(1, HEAD_DIM), subln_w.reshape(V_HEAD_DIM, 1))


def _out_kernel(x_ref, ys_ref, ya_ref, p_ref, wos_ref, woa_ref, wg_ref, wp_ref, o_ref):
    x2 = (x_ref[...]
          + jnp.dot(ys_ref[...], wos_ref[...], preferred_element_type=F32)
          + jnp.dot(ya_ref[...], woa_ref[...], preferred_element_type=F32))
    gate = jax.nn.sigmoid(jnp.dot(x2.astype(BF16), wg_ref[...], preferred_element_type=F32))
    ple = jnp.dot(p_ref[...].astype(BF16), wp_ref[...], preferred_element_type=F32)
    o_ref[...] = x2 + gate * ple


def _out_proj(x, ys_tm, ya, p, w_out, w_proj, w_gate):
    b, s, d = x.shape
    tm = TOKEN_TILE
    const2 = lambda bi, ti: (0, 0)
    return pl.pallas_call(
        _out_kernel,
        grid=(b, s // tm),
        in_specs=[
            pl.BlockSpec((None, tm, d), lambda bi, ti: (bi, ti, 0)),
            pl.BlockSpec((None, tm, SSM_WIDTH), lambda bi, ti: (bi, ti, 0)),
            pl.BlockSpec((None, tm, ATTN_WIDTH), lambda bi, ti: (bi, ti, 0)),
            pl.BlockSpec((None, tm, PLE_DIM), lambda bi, ti: (bi, ti, 0)),
            pl.BlockSpec((SSM_WIDTH, d), const2),
            pl.BlockSpec((ATTN_WIDTH, d), const2),
            pl.BlockSpec((d, d), const2),
            pl.BlockSpec((PLE_DIM, d), const2),
        ],
        out_specs=pl.BlockSpec((None, tm, d), lambda bi, ti: (bi, ti, 0)),
        out_shape=jax.ShapeDtypeStruct((b, s, d), F32),
        compiler_params=_compiler_params(2),
        name="out_proj",
    )(x, ys_tm, ya, p, w_out[:SSM_WIDTH].astype(BF16), w_out[SSM_WIDTH:].astype(BF16),
      w_gate.astype(BF16), w_proj.astype(BF16))


def kernel(x, p, positions, norm_w, w_in, ssm_lambda_re, ssm_lambda_im, ssm_log_dt, ssm_b_re, ssm_b_im, ssm_c_re, ssm_c_im, ssm_d, glu_w, glu_b, q_norm_w, k_norm_w, lambda_q1, lambda_k1, lambda_q2, lambda_k2, subln_w, w_out, ple_w_proj, ple_w_gate):
    b, s, _ = x.shape
    depth = norm_w.shape[0]
    for i in range(depth):
        lambda_init = 0.8 - 0.6 * math.exp(-0.3 * i)
        a_re, a_im, bb_re_t, bb_im_t = _ssm_params(
            ssm_lambda_re[i], ssm_lambda_im[i], ssm_log_dt[i], ssm_b_re[i], ssm_b_im[i])
        u, zs, qt, k, vt, zat = _in_proj(x, positions, norm_w[i], w_in[i], q_norm_w[i], k_norm_w[i])
        ys = _ssm_branch(u, zs, a_re, a_im, bb_re_t, bb_im_t, ssm_c_re[i], ssm_c_im[i],
                         ssm_d[i], glu_w[i], glu_b[i])
        ya = _attention(qt, k, vt, zat, lambda_q1[i], lambda_k1[i], lambda_q2[i], lambda_k2[i],
                        subln_w[i], lambda_init)
        x = _out_proj(x, ys, ya, p[i], w_out[i], ple_w_proj[i], ple_w_gate[i])
    return x
```

```python
import functools
import math

import jax
import jax.numpy as jnp
from jax import lax
from jax.experimental import pallas as pl
from jax.experimental.pallas import tpu as pltpu

F32 = jnp.float32
BF16 = jnp.bfloat16

D_MODEL = 1024
PLE_DIM = 256
SSM_WIDTH = 512
SSM_GROUP = 16
SSM_GROUPS = 32
SSM_STATE = 64
ATTN_WIDTH = 512
N_HEADS = 4
HEAD_DIM = 64
V_HEAD_DIM = 128
ROT_DIM = 16
ROT_HALF = 8
ROPE_THETA = 500000.0
EPS = 1e-6
LOG2E = math.log2(math.e)

V7X_LANES = 128
V7X_SUBLANES = 8
V7X_VMEM_BYTES = 64 * 1024 * 1024
VMEM_LIMIT_BYTES = V7X_VMEM_BYTES * 7 // 8

TOKEN_TILE = 512
SSM_STEPS = 64
SSM_SUB_STEPS = 32
GROUPS_PER_BLOCK = 8
N_SSM_BLOCKS = SSM_GROUPS // GROUPS_PER_BLOCK
BLOCK_STATE = GROUPS_PER_BLOCK * SSM_STATE
MASK_VALUE = -1e30


def _compiler_params(n_grid_axes):
    return pltpu.CompilerParams(
        dimension_semantics=("arbitrary",) * n_grid_axes,
        vmem_limit_bytes=VMEM_LIMIT_BYTES,
    )


def _ssm_params_kernel(lr_ref, li_ref, logdt_ref, br_ref, bi_ref,
                       ar_ref, ai_ref, bbr_ref, bbi_ref):
    lr = lr_ref[...]
    li = li_ref[...]
    dt = jnp.exp(logdt_ref[...])
    mag = jnp.exp(lr * dt)
    ab_re = mag * jnp.cos(li * dt)
    ab_im = mag * jnp.sin(li * dt)
    nr = ab_re - 1.0
    ni = ab_im
    den = lr * lr + li * li
    coef_re = (nr * lr + ni * li) / den
    coef_im = (ni * lr - nr * li) / den
    br = br_ref[...]
    bi = bi_ref[...]
    ar_ref[...] = ab_re
    ai_ref[...] = ab_im
    bbr_ref[...] = coef_re * br - coef_im * bi
    bbi_ref[...] = coef_re * bi + coef_im * br


def _ssm_params(lam_re, lam_im, log_dt, b_re, b_im):
    g, n, p = SSM_GROUPS, SSM_STATE, SSM_GROUP
    gn = jax.ShapeDtypeStruct((g, 1, n), F32)
    gpn = jax.ShapeDtypeStruct((g, p, n), F32)
    return pl.pallas_call(
        _ssm_params_kernel,
        out_shape=(gn, gn, gpn, gpn),
        name="ssm_params",
    )(lam_re.reshape(g, 1, n), lam_im.reshape(g, 1, n), log_dt.reshape(g, 1, 1),
      jnp.swapaxes(b_re, 1, 2), jnp.swapaxes(b_im, 1, 2))


def _in_proj_kernel(x_ref, nw_ref, wnat_ref, wtr_ref, pos_ref, invf_ref, qnw_ref, knw_ref,
                    u_ref, zs_ref, qt_ref, k_ref, vt_ref, zat_ref):
    x = x_ref[...]
    ms = jnp.mean(x * x, axis=-1, keepdims=True)
    h = (x * lax.rsqrt(ms + EPS) * nw_ref[...]).astype(BF16)
    nat = jnp.dot(h, wnat_ref[...], preferred_element_type=F32)
    u_ref[...] = nat[:, :SSM_WIDTH].astype(BF16)
    zs_ref[...] = nat[:, SSM_WIDTH:].astype(BF16)
    tr = lax.dot_general(wtr_ref[...], h, (((1,), (1,)), ((), ())),
                         preferred_element_type=F32)
    tm = x.shape[0]
    ang = pos_ref[...].astype(F32) * invf_ref[...]
    cos = jnp.cos(ang)
    sin = jnp.sin(ang)

    def norm_rope(t, w_ref):
        t3 = t.reshape(2 * N_HEADS, HEAD_DIM, tm)
        ms3 = jnp.mean(t3 * t3, axis=1, keepdims=True)
        t3 = t3 * lax.rsqrt(ms3 + EPS) * w_ref[...]
        t1 = t3[:, :ROT_HALF, :]
        t2 = t3[:, ROT_HALF:ROT_DIM, :]
        r1 = t1 * cos - t2 * sin
        r2 = t2 * cos + t1 * sin
        t3 = jnp.concatenate([r1, r2, t3[:, ROT_DIM:, :]], axis=1)
        return t3.reshape(ATTN_WIDTH, tm)

    q = norm_rope(tr[0:ATTN_WIDTH], qnw_ref) * (HEAD_DIM ** -0.5 * LOG2E)
    qt_ref[...] = q.reshape(N_HEADS, 2 * HEAD_DIM, tm).astype(BF16)
    k = norm_rope(tr[ATTN_WIDTH:2 * ATTN_WIDTH], knw_ref)
    k_ref[...] = k.T.astype(BF16)
    vt_ref[...] = tr[2 * ATTN_WIDTH:3 * ATTN_WIDTH].reshape(N_HEADS, V_HEAD_DIM, tm).astype(BF16)
    zat_ref[...] = tr[3 * ATTN_WIDTH:].reshape(N_HEADS, V_HEAD_DIM, tm).astype(BF16)


def _in_proj(x, positions, norm_w, w_in, q_norm_w, k_norm_w):
    b, s, d = x.shape
    tm = TOKEN_TILE
    nt = s // tm
    w_nat = w_in[:, :2 * SSM_WIDTH].astype(BF16)
    w_tr = w_in[:, 2 * SSM_WIDTH:].T.astype(BF16)
    inv_freq = ROPE_THETA ** (-jnp.arange(0, ROT_DIM, 2, dtype=F32) / ROT_DIM)
    head_major = jax.ShapeDtypeStruct((b, N_HEADS, nt, V_HEAD_DIM, tm), BF16)
    head_spec = pl.BlockSpec((None, N_HEADS, None, V_HEAD_DIM, tm), lambda bi, ti: (bi, 0, ti, 0, 0))
    ssm_shape = jax.ShapeDtypeStruct((b, s, SSM_WIDTH), BF16)
    ssm_spec = pl.BlockSpec((None, tm, SSM_WIDTH), lambda bi, ti: (bi, ti, 0))
    const2 = lambda bi, ti: (0, 0)
    const3 = lambda bi, ti: (0, 0, 0)
    return pl.pallas_call(
        _in_proj_kernel,
        grid=(b, nt),
        in_specs=[
            pl.BlockSpec((None, tm, d), lambda bi, ti: (bi, ti, 0)),
            pl.BlockSpec((1, d), const2),
            pl.BlockSpec((d, 2 * SSM_WIDTH), const2),
            pl.BlockSpec((4 * ATTN_WIDTH, d), const2),
            pl.BlockSpec((None, 1, tm), lambda bi, ti: (bi, 0, ti)),
            pl.BlockSpec((ROT_HALF, 1), const2),
            pl.BlockSpec((1, HEAD_DIM, 1), const3),
            pl.BlockSpec((1, HEAD_DIM, 1), const3),
        ],
        out_specs=[
            ssm_spec, ssm_spec,
            head_spec,
            pl.BlockSpec((None, tm, ATTN_WIDTH), lambda bi, ti: (bi, ti, 0)),
            head_spec, head_spec,
        ],
        out_shape=(ssm_shape, ssm_shape, head_major,
                   jax.ShapeDtypeStruct((b, s, ATTN_WIDTH), BF16), head_major, head_major),
        compiler_params=_compiler_params(2),
        name="in_proj",
    )(x, norm_w.reshape(1, d), w_nat, w_tr, positions.reshape(b, 1, s),
      inv_freq.reshape(ROT_HALF, 1), q_norm_w.reshape(1, HEAD_DIM, 1), k_norm_w.reshape(1, HEAD_DIM, 1))


def _ssm_kernel(u_ref, zs_ref, perm_ref, permt_ref, bmat_ref, cmat_ref, ar_ref, ai_ref, d_ref,
                gw_ref, gb_ref, y_ref, ut_ref, bu_ref, xs_ref, ys_ref, state_ref):
    n_batch, steps, width = u_ref.shape
    sub_rows = perm_ref.shape[0]
    sub_steps = sub_rows // n_batch
    n_sub = steps // sub_steps
    blk = 2 * BLOCK_STATE
    step_rows = 2 * V7X_SUBLANES

    @pl.when(pl.program_id(0) == 0)
    def _():
        state_ref[...] = jnp.zeros_like(state_ref)

    def rows_of(sub):
        return slice(sub * sub_rows, (sub + 1) * sub_rows)

    def permute_in(sub):
        u_bt = u_ref[:, sub * sub_steps:(sub + 1) * sub_steps, :].reshape(sub_rows, width)
        ut_ref[rows_of(sub), :] = jnp.dot(perm_ref[...], u_bt, preferred_element_type=F32).astype(BF16)

    def project_in(sub, m):
        bu_ref[rows_of(sub), m * blk:(m + 1) * blk] = jnp.dot(
            ut_ref[rows_of(sub), m * V7X_LANES:(m + 1) * V7X_LANES], bmat_ref[m], preferred_element_type=F32)

    def scan(sub, m):
        re = slice(m * blk, m * blk + BLOCK_STATE)
        im = slice(m * blk + BLOCK_STATE, (m + 1) * blk)
        a_re = jnp.broadcast_to(ar_ref[:, m * BLOCK_STATE:(m + 1) * BLOCK_STATE], (V7X_SUBLANES, BLOCK_STATE))
        a_im = jnp.broadcast_to(ai_ref[:, m * BLOCK_STATE:(m + 1) * BLOCK_STATE], (V7X_SUBLANES, BLOCK_STATE))
        x_re = state_ref[:, re]
        x_im = state_ref[:, im]
        for r0 in range(sub * sub_rows, (sub + 1) * sub_rows, step_rows):
            outs_re, outs_im = [], []
            for half in range(2):
                rr = slice(r0 + half * V7X_SUBLANES, r0 + (half + 1) * V7X_SUBLANES)
                n_re = a_re * x_re - a_im * x_im + bu_ref[rr, re]
                n_im = a_re * x_im + a_im * x_re + bu_ref[rr, im]
                x_re, x_im = n_re, n_im
                outs_re.append(n_re)
                outs_im.append(n_im)
            xs_ref[r0:r0 + step_rows, re] = jnp.concatenate(outs_re, axis=0).astype(BF16)
            xs_ref[r0:r0 + step_rows, im] = jnp.concatenate(outs_im, axis=0).astype(BF16)
        state_ref[:, re] = x_re
        state_ref[:, im] = x_im

    def read_out(sub, m):
        ys_ref[rows_of(sub), m * V7X_LANES:(m + 1) * V7X_LANES] = jnp.dot(
            xs_ref[rows_of(sub), m * blk:(m + 1) * blk], cmat_ref[m], preferred_element_type=F32)

    def finish(sub):
        y = ys_ref[rows_of(sub), :] + d_ref[...] * ut_ref[rows_of(sub), :].astype(F32)
        y = jax.nn.gelu(y)
        gate = jnp.dot(y.astype(BF16), gw_ref[...], preferred_element_type=F32) + gb_ref[...]
        y = (y * jax.nn.sigmoid(gate)).astype(BF16)
        y_bt = jnp.dot(permt_ref[...], y, preferred_element_type=F32)
        t = slice(sub * sub_steps, (sub + 1) * sub_steps)
        z = zs_ref[:, t, :].reshape(sub_rows, width).astype(F32)
        y_ref[:, t, :] = (y_bt * jax.nn.silu(z)).astype(BF16).reshape(n_batch, sub_steps, width)

    pairs = [(2 * k, 2 * k + 1) for k in range(N_SSM_BLOCKS // 2)]
    units = [(sub, pair) for sub in range(n_sub) for pair in pairs]
    always = pl.program_id(0) >= 0
    for stage in range(len(units) + 2):
        @pl.when(always)
        def _(stage=stage):
            nxt, cur, prv = stage, stage - 1, stage - 2
            if nxt < len(units):
                sub, pair = units[nxt]
                if pair == pairs[0]:
                    permute_in(sub)
                for m in pair:
                    project_in(sub, m)
            if 0 <= cur < len(units):
                sub, pair = units[cur]
                for m in pair:
                    scan(sub, m)
            if 0 <= prv < len(units):
                sub, pair = units[prv]
                for m in pair:
                    read_out(sub, m)
                if pair == pairs[-1]:
                    finish(sub)


def _ssm_branch(u, zs, a_re, a_im, bb_re_t, bb_im_t, c_re, c_im, d_skip, glu_w, glu_b):
    n_batch, s, _ = u.shape
    assert n_batch == V7X_SUBLANES, "one time step of all batches must fill one sublane tile"
    steps = SSM_STEPS
    rows = steps * n_batch
    gpb, nb, p, n = GROUPS_PER_BLOCK, N_SSM_BLOCKS, SSM_GROUP, SSM_STATE
    sub_rows = SSM_SUB_STEPS * n_batch
    r = jnp.arange(sub_rows)
    perm = (r[None, :] == ((r % n_batch) * SSM_SUB_STEPS + r // n_batch)[:, None]).astype(BF16)
    eye = jnp.eye(gpb, dtype=F32)
    bb = jnp.stack([bb_re_t, bb_im_t]).reshape(2, nb, gpb, p, n)
    bmat = jnp.einsum('rmgpn,gh->mgprhn', bb, eye).reshape(nb, gpb * p, 2 * BLOCK_STATE).astype(BF16)
    cc = jnp.stack([c_re, -c_im]).reshape(2, nb, gpb, p, n)
    cmat = jnp.einsum('rmgpn,gh->mrgnhp', cc, eye).reshape(nb, 2 * BLOCK_STATE, gpb * p).astype(BF16)
    row_spec = pl.BlockSpec((n_batch, steps, SSM_WIDTH), lambda i: (0, i, 0))
    const2 = lambda i: (0, 0)
    const3 = lambda i: (0, 0, 0)
    n_state = SSM_GROUPS * SSM_STATE
    return pl.pallas_call(
        _ssm_kernel,
        grid=(s // steps,),
        in_specs=[
            row_spec, row_spec,
            pl.BlockSpec((sub_rows, sub_rows), const2),
            pl.BlockSpec((sub_rows, sub_rows), const2),
            pl.BlockSpec((nb, gpb * p, 2 * BLOCK_STATE), const3),
            pl.BlockSpec((nb, 2 * BLOCK_STATE, gpb * p), const3),
            pl.BlockSpec((1, n_state), const2),
            pl.BlockSpec((1, n_state), const2),
            pl.BlockSpec((1, SSM_WIDTH), const2),
            pl.BlockSpec((SSM_WIDTH, SSM_WIDTH), const2),
            pl.BlockSpec((1, SSM_WIDTH), const2),
        ],
        out_specs=row_spec,
        out_shape=jax.ShapeDtypeStruct((n_batch, s, SSM_WIDTH), BF16),
        scratch_shapes=[
            pltpu.VMEM((rows, SSM_WIDTH), BF16),
            pltpu.VMEM((rows, 2 * n_state), F32),
            pltpu.VMEM((rows, 2 * n_state), BF16),
            pltpu.VMEM((rows, SSM_WIDTH), F32),
            pltpu.VMEM((V7X_SUBLANES, 2 * n_state), F32),
        ],
        compiler_params=_compiler_params(1),
        name="ssm_branch",
    )(u, zs, perm, perm.T, bmat, cmat, a_re.reshape(1, n_state), a_im.reshape(1, n_state),
      d_skip.reshape(1, SSM_WIDTH), glu_w.astype(BF16), glu_b.reshape(1, SSM_WIDTH))


def _attn_kernel(lambda_init, qt_ref, k_ref, vt_ref, zat_ref, lq1_ref, lk1_ref, lq2_ref, lk2_ref,
                 subw_ref, y_ref, wq_ref, s_ref, mx_ref, acc_ref, m_ref, l_ref):
    n_tiles, _, tq = qt_ref.shape
    tk = tq
    diag_slot = 2

    def load_queries(q):
        wq_ref[0, :HEAD_DIM] = qt_ref[q, :HEAD_DIM]
        wq_ref[1, HEAD_DIM:] = qt_ref[q, HEAD_DIM:]

    def reset_stats():
        m_ref[...] = jnp.full_like(m_ref, MASK_VALUE)
        l_ref[...] = jnp.zeros_like(l_ref)
        acc_ref[...] = jnp.zeros_like(acc_ref)

    def scores(j, slot, masked):
        kb = k_ref[pl.ds(pl.multiple_of(j * tk, tk), tk), :]
        for c in range(2):
            s = jnp.dot(kb, wq_ref[c], preferred_element_type=F32)
            if masked:
                key_idx = lax.broadcasted_iota(jnp.int32, (tk, tq), 0)
                qry_idx = lax.broadcasted_iota(jnp.int32, (tk, tq), 1)
                s = jnp.where(key_idx <= qry_idx, s, MASK_VALUE)
            s_ref[slot, c] = s
            mx_ref[slot, c] = jnp.max(s, axis=0, keepdims=True)

    def accumulate(j, slot):
        vb = vt_ref[j]
        for c in range(2):
            m_old = m_ref[c]
            m_new = jnp.maximum(m_old, mx_ref[slot, c])
            alpha = jnp.exp2(m_old - m_new)
            p = jnp.exp2(s_ref[slot, c] - m_new)
            l_ref[c] = alpha * l_ref[c] + jnp.sum(p, axis=0, keepdims=True)
            acc_ref[c] = alpha * acc_ref[c] + jnp.dot(vb, p.astype(BF16), preferred_element_type=F32)
            m_ref[c] = m_new

    def finalize(q):
        lam = (jnp.exp(jnp.sum(lq1_ref[...] * lk1_ref[...], axis=-1, keepdims=True))
               - jnp.exp(jnp.sum(lq2_ref[...] * lk2_ref[...], axis=-1, keepdims=True)) + lambda_init)
        out = acc_ref[0] / l_ref[0] - lam * (acc_ref[1] / l_ref[1])
        ms = jnp.mean(out * out, axis=0, keepdims=True)
        out = out * lax.rsqrt(ms + EPS) * subw_ref[...] * (1.0 - lambda_init)
        out = out * jax.nn.silu(zat_ref[q].astype(F32))
        y_ref[pl.ds(pl.multiple_of(q * tq, tq), tq), :] = out.T.astype(BF16)

    zeros = jnp.zeros((HEAD_DIM, tq), BF16)
    wq_ref[0, HEAD_DIM:] = zeros
    wq_ref[1, :HEAD_DIM] = zeros
    load_queries(0)
    reset_stats()
    scores(0, 1, masked=True)

    def tile(q, carry):
        load_queries(q)
        last_block = jnp.maximum(q - 2, 0)

        @pl.when(q % 2 == 0)
        def _():
            scores(q, diag_slot, masked=True)
            accumulate(last_block, 0)
            finalize(q - 1)

        @pl.when(q % 2 == 1)
        def _():
            scores(q, diag_slot, masked=True)
            accumulate(last_block, 1)
            finalize(q - 1)

        reset_stats()
        scores(0, 0, masked=False)
        accumulate(q, diag_slot)

        def pair(i, c):
            j = 2 * i
            scores(j + 1, 1, masked=False)
            accumulate(j, 0)
            scores(j + 2, 0, masked=False)
            accumulate(j + 1, 1)
            return c

        lax.fori_loop(0, (q - 1) // 2, pair, 0)

        @pl.when(q % 2 == 0)
        def _():
            scores(q - 1, 1, masked=False)
            accumulate(q - 2, 0)

        return carry

    lax.fori_loop(1, n_tiles, tile, 0)
    accumulate(n_tiles - 2, (n_tiles - 2) % 2)
    finalize(n_tiles - 1)


def _attention(qt, k, vt, zat, lq1, lk1, lq2, lk2, subln_w, lambda_init):
    b, _, nt, _, tq = qt.shape
    s = nt * tq
    assert nt >= 2
    row64 = pl.BlockSpec((1, HEAD_DIM), lambda bi, hi: (0, 0))
    head_spec = pl.BlockSpec((None, None, nt, V_HEAD_DIM, tq), lambda bi, hi: (bi, hi, 0, 0, 0))
    token_spec = pl.BlockSpec((None, s, V_HEAD_DIM), lambda bi, hi: (bi, 0, hi))
    return pl.pallas_call(
        functools.partial(_attn_kernel, lambda_init),
        grid=(b, N_HEADS),
        in_specs=[
            head_spec, token_spec, head_spec, head_spec,
            row64, row64, row64, row64,
            pl.BlockSpec((V_HEAD_DIM, 1), lambda bi, hi: (0, 0)),
        ],
        out_specs=token_spec,
        out_shape=jax.ShapeDtypeStruct((b, s, ATTN_WIDTH), BF16),
        scratch_shapes=[
            pltpu.VMEM((2, 2 * HEAD_DIM, tq), BF16),
            pltpu.VMEM((3, 2, tq, tq), F32),
            pltpu.VMEM((3, 2, 1, tq), F32),
            pltpu.VMEM((2, V_HEAD_DIM, tq), F32),
            pltpu.VMEM((2, 1, tq), F32),
            pltpu.VMEM((2, 1, tq), F32),
        ],
        compiler_params=_compiler_params(2),
        name="diff_attention",
    )(qt, k, vt, zat, lq1.reshape(1, HEAD_DIM), lk1.reshape(1, HEAD_DIM),
      lq2.reshape(1, HEAD_DIM), lk2.reshape(1, HEAD_DIM), subln_w.reshape(V_HEAD_DIM, 1))


def _out_kernel(x_ref, ys_ref, ya_ref, p_ref, wos_ref, woa_ref, wg_ref, wp_ref, o_ref):
    x2 = (x_ref[...]
          + jnp.dot(ys_ref[...], wos_ref[...], preferred_element_type=F32)
          + jnp.dot(ya_ref[...], woa_ref[...], preferred_element_type=F32))
    gate = jax.nn.sigmoid(jnp.dot(x2.astype(BF16), wg_ref[...], preferred_element_type=F32))
    ple = jnp.dot(p_ref[...].astype(BF16), wp_ref[...], preferred_element_type=F32)
    o_ref[...] = x2 + gate * ple


def _out_proj(x, ys_tm, ya, p, w_out, w_proj, w_gate):
    b, s, d = x.shape
    tm = TOKEN_TILE
    const2 = lambda bi, ti: (0, 0)
    return pl.pallas_call(
        _out_kernel,
        grid=(b, s // tm),
        in_specs=[
            pl.BlockSpec((None, tm, d), lambda bi, ti: (bi, ti, 0)),
            pl.BlockSpec((None, tm, SSM_WIDTH), lambda bi, ti: (bi, ti, 0)),
            pl.BlockSpec((None, tm, ATTN_WIDTH), lambda bi, ti: (bi, ti, 0)),
            pl.BlockSpec((None, tm, PLE_DIM), lambda bi, ti: (bi, ti, 0)),
            pl.BlockSpec((SSM_WIDTH, d), const2),
            pl.BlockSpec((ATTN_WIDTH, d), const2),
            pl.BlockSpec((d, d), const2),
            pl.BlockSpec((PLE_DIM, d), const2),
        ],
        out_specs=pl.BlockSpec((None, tm, d), lambda bi, ti: (bi, ti, 0)),
        out_shape=jax.ShapeDtypeStruct((b, s, d), F32),
        compiler_params=_compiler_params(2),
        name="out_proj",
    )(x, ys_tm, ya, p, w_out[:SSM_WIDTH].astype(BF16), w_out[SSM_WIDTH:].astype(BF16),
      w_gate.astype(BF16), w_proj.astype(BF16))


def kernel(x, p, positions, norm_w, w_in, ssm_lambda_re, ssm_lambda_im, ssm_log_dt, ssm_b_re, ssm_b_im, ssm_c_re, ssm_c_im, ssm_d, glu_w, glu_b, q_norm_w, k_norm_w, lambda_q1, lambda_k1, lambda_q2, lambda_k2, subln_w, w_out, ple_w_proj, ple_w_gate):
    b, s, _ = x.shape
    depth = norm_w.shape[0]
    for i in range(depth):
        lambda_init = 0.8 - 0.6 * math.exp(-0.3 * i)
        a_re, a_im, bb_re_t, bb_im_t = _ssm_params(
            ssm_lambda_re[i], ssm_lambda_im[i], ssm_log_dt[i], ssm_b_re[i], ssm_b_im[i])
        u, zs, qt, k, vt, zat = _in_proj(x, positions, norm_w[i], w_in[i], q_norm_w[i], k_norm_w[i])
        ys = _ssm_branch(u, zs, a_re, a_im, bb_re_t, bb_im_t, ssm_c_re[i], ssm_c_im[i],
                         ssm_d[i], glu_w[i], glu_b[i])
        ya = _attention(qt, k, vt, zat, lambda_q1[i], lambda_k1[i], lambda_q2[i], lambda_k2[i],
                        subln_w[i], lambda_init)
        x = _out_proj(x, ys, ya, p[i], w_out[i], ple_w_proj[i], ple_w_gate[i])
    return x
```

```python
import functools
import math

import jax
import jax.numpy as jnp
from jax import lax
from jax.experimental import pallas as pl
from jax.experimental.pallas import tpu as pltpu

F32 = jnp.float32
BF16 = jnp.bfloat16

D_MODEL = 1024
PLE_DIM = 256
SSM_WIDTH = 512
SSM_GROUP = 16
SSM_GROUPS = 32
SSM_STATE = 64
ATTN_WIDTH = 512
N_HEADS = 4
HEAD_DIM = 64
V_HEAD_DIM = 128
ROT_DIM = 16
ROT_HALF = 8
ROPE_THETA = 500000.0
EPS = 1e-6
LOG2E = math.log2(math.e)

V7X_LANES = 128
V7X_SUBLANES = 8
V7X_VMEM_BYTES = 64 * 1024 * 1024
VMEM_LIMIT_BYTES = V7X_VMEM_BYTES * 7 // 8

TOKEN_TILE = 512
SSM_FOLD = 4
SSM_STEPS = 128
SSM_PART_STEPS = 64
SSM_PERM_STEPS = 32
GROUPS_PER_BLOCK = 8
N_SSM_BLOCKS = SSM_GROUPS // GROUPS_PER_BLOCK
BLOCK_STATE = GROUPS_PER_BLOCK * SSM_STATE
MASK_VALUE = -1e30


def _compiler_params(n_grid_axes):
    return pltpu.CompilerParams(
        dimension_semantics=("arbitrary",) * n_grid_axes,
        vmem_limit_bytes=VMEM_LIMIT_BYTES,
    )


def _ssm_params_kernel(lr_ref, li_ref, logdt_ref, br_ref, bi_ref,
                       ar_ref, ai_ref, bbr_ref, bbi_ref):
    lr = lr_ref[...]
    li = li_ref[...]
    dt = jnp.exp(logdt_ref[...])
    mag = jnp.exp(lr * dt)
    ab_re = mag * jnp.cos(li * dt)
    ab_im = mag * jnp.sin(li * dt)
    nr = ab_re - 1.0
    ni = ab_im
    den = lr * lr + li * li
    coef_re = (nr * lr + ni * li) / den
    coef_im = (ni * lr - nr * li) / den
    br = br_ref[...]
    bi = bi_ref[...]
    ar_ref[...] = ab_re
    ai_ref[...] = ab_im
    bbr_ref[...] = coef_re * br - coef_im * bi
    bbi_ref[...] = coef_re * bi + coef_im * br


def _ssm_params(lam_re, lam_im, log_dt, b_re, b_im):
    g, n, p = SSM_GROUPS, SSM_STATE, SSM_GROUP
    gn = jax.ShapeDtypeStruct((g, 1, n), F32)
    gpn = jax.ShapeDtypeStruct((g, p, n), F32)
    return pl.pallas_call(
        _ssm_params_kernel,
        out_shape=(gn, gn, gpn, gpn),
        name="ssm_params",
    )(lam_re.reshape(g, 1, n), lam_im.reshape(g, 1, n), log_dt.reshape(g, 1, 1),
      jnp.swapaxes(b_re, 1, 2), jnp.swapaxes(b_im, 1, 2))


def _in_proj_kernel(x_ref, nw_ref, wnat_ref, wtr_ref, pos_ref, invf_ref, qnw_ref, knw_ref,
                    u_ref, zs_ref, qt_ref, k_ref, vt_ref, zat_ref):
    x = x_ref[...]
    ms = jnp.mean(x * x, axis=-1, keepdims=True)
    h = (x * lax.rsqrt(ms + EPS) * nw_ref[...]).astype(BF16)
    nat = jnp.dot(h, wnat_ref[...], preferred_element_type=F32)
    u_ref[...] = nat[:, :SSM_WIDTH].astype(BF16)
    zs_ref[...] = nat[:, SSM_WIDTH:].astype(BF16)
    tr = lax.dot_general(wtr_ref[...], h, (((1,), (1,)), ((), ())),
                         preferred_element_type=F32)
    tm = x.shape[0]
    ang = pos_ref[...].astype(F32) * invf_ref[...]
    cos = jnp.cos(ang)
    sin = jnp.sin(ang)

    def norm_rope(t, w_ref):
        t3 = t.reshape(2 * N_HEADS, HEAD_DIM, tm)
        ms3 = jnp.mean(t3 * t3, axis=1, keepdims=True)
        t3 = t3 * lax.rsqrt(ms3 + EPS) * w_ref[...]
        t1 = t3[:, :ROT_HALF, :]
        t2 = t3[:, ROT_HALF:ROT_DIM, :]
        r1 = t1 * cos - t2 * sin
        r2 = t2 * cos + t1 * sin
        t3 = jnp.concatenate([r1, r2, t3[:, ROT_DIM:, :]], axis=1)
        return t3.reshape(ATTN_WIDTH, tm)

    q = norm_rope(tr[0:ATTN_WIDTH], qnw_ref) * (HEAD_DIM ** -0.5 * LOG2E)
    qt_ref[...] = q.reshape(N_HEADS, 2 * HEAD_DIM, tm).astype(BF16)
    k = norm_rope(tr[ATTN_WIDTH:2 * ATTN_WIDTH], knw_ref)
    k_ref[...] = k.T.astype(BF16)
    vt_ref[...] = tr[2 * ATTN_WIDTH:3 * ATTN_WIDTH].reshape(N_HEADS, V_HEAD_DIM, tm).astype(BF16)
    zat_ref[...] = tr[3 * ATTN_WIDTH:].reshape(N_HEADS, V_HEAD_DIM, tm).astype(BF16)


def _in_proj(x, positions, norm_w, w_in, q_norm_w, k_norm_w):
    b, s, d = x.shape
    tm = TOKEN_TILE
    nt = s // tm
    w_nat = w_in[:, :2 * SSM_WIDTH].astype(BF16)
    w_tr = w_in[:, 2 * SSM_WIDTH:].T.astype(BF16)
    inv_freq = ROPE_THETA ** (-jnp.arange(0, ROT_DIM, 2, dtype=F32) / ROT_DIM)
    head_major = jax.ShapeDtypeStruct((b, N_HEADS, nt, V_HEAD_DIM, tm), BF16)
    head_spec = pl.BlockSpec((None, N_HEADS, None, V_HEAD_DIM, tm), lambda bi, ti: (bi, 0, ti, 0, 0))
    ssm_shape = jax.ShapeDtypeStruct((b, s, SSM_WIDTH), BF16)
    ssm_spec = pl.BlockSpec((None, tm, SSM_WIDTH), lambda bi, ti: (bi, ti, 0))
    const2 = lambda bi, ti: (0, 0)
    const3 = lambda bi, ti: (0, 0, 0)
    return pl.pallas_call(
        _in_proj_kernel,
        grid=(b, nt),
        in_specs=[
            pl.BlockSpec((None, tm, d), lambda bi, ti: (bi, ti, 0)),
            pl.BlockSpec((1, d), const2),
            pl.BlockSpec((d, 2 * SSM_WIDTH), const2),
            pl.BlockSpec((4 * ATTN_WIDTH, d), const2),
            pl.BlockSpec((None, 1, tm), lambda bi, ti: (bi, 0, ti)),
            pl.BlockSpec((ROT_HALF, 1), const2),
            pl.BlockSpec((1, HEAD_DIM, 1), const3),
            pl.BlockSpec((1, HEAD_DIM, 1), const3),
        ],
        out_specs=[
            ssm_spec, ssm_spec,
            head_spec,
            pl.BlockSpec((None, tm, ATTN_WIDTH), lambda bi, ti: (bi, ti, 0)),
            head_spec, head_spec,
        ],
        out_shape=(ssm_shape, ssm_shape, head_major,
                   jax.ShapeDtypeStruct((b, s, ATTN_WIDTH), BF16), head_major, head_major),
        compiler_params=_compiler_params(2),
        name="in_proj",
    )(x, norm_w.reshape(1, d), w_nat, w_tr, positions.reshape(b, 1, s),
      inv_freq.reshape(ROT_HALF, 1), q_norm_w.reshape(1, HEAD_DIM, 1), k_norm_w.reshape(1, HEAD_DIM, 1))


def _ssm_kernel(u_ref, zs_ref, pin_ref, pout_ref, bw_ref, cs_ref, cd_ref, ar_ref, ai_ref, d_ref,
                gw_ref, gb_ref, y_ref, uf_ref, bu_ref, xs_ref, y2_ref, yg_ref, state_ref):
    n_batch, steps, width = u_ref.shape
    fold, perm_out_rows, perm_in_rows = pin_ref.shape
    perm_steps = perm_in_rows // n_batch
    part_steps = SSM_PART_STEPS
    perms_per_part = part_steps // perm_steps
    part_rows = perms_per_part * perm_out_rows
    n_parts = steps // part_steps
    blk = 2 * BLOCK_STATE
    fw = fold * V7X_LANES
    pack_rows = 2 * V7X_SUBLANES

    @pl.when(pl.program_id(0) == 0)
    def _():
        state_ref[...] = jnp.zeros_like(state_ref)

    def rows_of(part):
        return slice(part * part_rows, (part + 1) * part_rows)

    def fold_in(part):
        for h in range(perms_per_part):
            t0 = part * part_steps + h * perm_steps
            u_bt = u_ref[:, t0:t0 + perm_steps, :].reshape(perm_in_rows, width)
            r0 = part * part_rows + h * perm_out_rows
            for i in range(fold):
                sel = jnp.dot(pin_ref[i], u_bt, preferred_element_type=F32).astype(BF16)
                for m in range(N_SSM_BLOCKS):
                    uf_ref[r0:r0 + perm_out_rows, m * fw + i * V7X_LANES:m * fw + (i + 1) * V7X_LANES] = (
                        sel[:, m * V7X_LANES:(m + 1) * V7X_LANES])

    def state_block(part, m):
        rows = rows_of(part)
        re = slice(m * blk, m * blk + BLOCK_STATE)
        im = slice(m * blk + BLOCK_STATE, (m + 1) * blk)
        bu_ref[rows, m * blk:(m + 1) * blk] = jnp.dot(
            uf_ref[rows, m * fw:(m + 1) * fw], bw_ref[m], preferred_element_type=F32)
        a_re = jnp.broadcast_to(ar_ref[:, m * BLOCK_STATE:(m + 1) * BLOCK_STATE], (V7X_SUBLANES, BLOCK_STATE))
        a_im = jnp.broadcast_to(ai_ref[:, m * BLOCK_STATE:(m + 1) * BLOCK_STATE], (V7X_SUBLANES, BLOCK_STATE))
        x_re = state_ref[:, re]
        x_im = state_ref[:, im]
        for r0 in range(part * part_rows, (part + 1) * part_rows, pack_rows):
            ins_re, ins_im = [], []
            for half in range(2):
                rr = slice(r0 + half * V7X_SUBLANES, r0 + (half + 1) * V7X_SUBLANES)
                ins_re.append(x_re)
                ins_im.append(x_im)
                n_re = a_re * x_re - a_im * x_im + bu_ref[rr, re]
                n_im = a_re * x_im + a_im * x_re + bu_ref[rr, im]
                x_re, x_im = n_re, n_im
            xs_ref[r0:r0 + pack_rows, re] = jnp.concatenate(ins_re, axis=0).astype(BF16)
            xs_ref[r0:r0 + pack_rows, im] = jnp.concatenate(ins_im, axis=0).astype(BF16)
        state_ref[:, re] = x_re
        state_ref[:, im] = x_im
        y2_ref[rows, m * fw:(m + 1) * fw] = (
            jnp.dot(xs_ref[rows, m * blk:(m + 1) * blk], cs_ref[m], preferred_element_type=F32)
            + jnp.dot(uf_ref[rows, m * fw:(m + 1) * fw], cd_ref[m], preferred_element_type=F32))

    def gate_offset(part, i):
        rows = rows_of(part)
        cols = [slice(m * fw + i * V7X_LANES, m * fw + (i + 1) * V7X_LANES) for m in range(N_SSM_BLOCKS)]
        y = jnp.concatenate([y2_ref[rows, c] for c in cols], axis=-1)
        u_i = jnp.concatenate([uf_ref[rows, c] for c in cols], axis=-1).astype(F32)
        y = jax.nn.gelu(y + d_ref[...] * u_i)
        gate = jnp.dot(y.astype(BF16), gw_ref[...], preferred_element_type=F32) + gb_ref[...]
        y = (y * jax.nn.sigmoid(gate)).astype(BF16)
        for h in range(perms_per_part):
            yg_ref[part * perms_per_part + h, i * perm_out_rows:(i + 1) * perm_out_rows, :] = (
                y[h * perm_out_rows:(h + 1) * perm_out_rows])

    def unfold_out(part):
        for h in range(perms_per_part):
            t0 = part * part_steps + h * perm_steps
            y_bt = jnp.dot(pout_ref[...], yg_ref[part * perms_per_part + h], preferred_element_type=F32)
            z = zs_ref[:, t0:t0 + perm_steps, :].reshape(perm_in_rows, width).astype(F32)
            y_ref[:, t0:t0 + perm_steps, :] = (
                (y_bt * jax.nn.silu(z)).astype(BF16).reshape(n_batch, perm_steps, width))

    assert fold == N_SSM_BLOCKS
    for part in range(n_parts + 1):
        if part < n_parts:
            fold_in(part)
        for j in range(N_SSM_BLOCKS):
            if part < n_parts:
                state_block(part, j)
            if part >= 1:
                gate_offset(part - 1, j)
        if part >= 1:
            unfold_out(part - 1)


def _ssm_weights(a_re, a_im, bb_re_t, bb_im_t, c_re, c_im):
    fold, gpb, nb, p, n = SSM_FOLD, GROUPS_PER_BLOCK, N_SSM_BLOCKS, SSM_GROUP, SSM_STATE
    hi = lax.Precision.HIGHEST
    eye = jnp.eye(gpb, dtype=F32)
    a_re = a_re.reshape(SSM_GROUPS, n)
    a_im = a_im.reshape(SSM_GROUPS, n)

    def cmul(xr, xi, yr, yi):
        return xr * yr - xi * yi, xr * yi + xi * yr

    powers = [(jnp.ones_like(a_re), jnp.zeros_like(a_im))]
    for _ in range(fold):
        powers.append(cmul(powers[-1][0], powers[-1][1], a_re, a_im))
    pw = [(r[:, None, :], i[:, None, :]) for r, i in powers]

    ab = [cmul(pw[fold - 1 - i][0], pw[fold - 1 - i][1], bb_re_t, bb_im_t) for i in range(fold)]
    ab = jnp.stack([jnp.stack(t) for t in ab]).reshape(fold, 2, nb, gpb, p, n)
    bw = jnp.einsum('irmgpn,gh->migprhn', ab, eye, precision=hi).reshape(nb, fold * gpb * p, 2 * BLOCK_STATE)
    ca = [cmul(c_re, c_im, pw[i + 1][0], pw[i + 1][1]) for i in range(fold)]
    ca = jnp.stack([jnp.stack([r, -im]) for r, im in ca]).reshape(fold, 2, nb, gpb, p, n)
    cs = jnp.einsum('irmgpn,gh->mrgnihp', ca, eye, precision=hi).reshape(nb, 2 * BLOCK_STATE, fold * gpb * p)
    lags = []
    for lag in range(fold):
        cr, ci = cmul(c_re, c_im, pw[lag][0], pw[lag][1])
        lags.append(jnp.einsum('gpn,gqn->gqp', cr, bb_re_t, precision=hi)
                    - jnp.einsum('gpn,gqn->gqp', ci, bb_im_t, precision=hi))
    zero = jnp.zeros_like(lags[0])
    toep = jnp.stack([jnp.stack([lags[i - k] if i >= k else zero for i in range(fold)]) for k in range(fold)])
    toep = toep.reshape(fold, fold, nb, gpb, p, p)
    cd = jnp.einsum('kimgqp,gh->mkgqihp', toep, eye, precision=hi).reshape(nb, fold * gpb * p, fold * gpb * p)
    return bw.astype(BF16), cs.astype(BF16), cd.astype(BF16), powers[fold]


def _ssm_branch(u, zs, a_re, a_im, bb_re_t, bb_im_t, c_re, c_im, d_skip, glu_w, glu_b):
    n_batch, s, _ = u.shape
    assert n_batch == V7X_SUBLANES, "one time group of all batches must fill one sublane tile"
    fold, steps, perm_steps = SSM_FOLD, SSM_STEPS, SSM_PERM_STEPS
    g_rows = (perm_steps // fold) * n_batch
    n_rows = perm_steps * n_batch
    r = jnp.arange(g_rows)
    src = (r % n_batch) * perm_steps + fold * (r // n_batch)
    pin = (jnp.arange(n_rows)[None, None, :] == (src[None, :, None] + jnp.arange(fold)[:, None, None])).astype(BF16)
    pout = pin.reshape(fold * g_rows, n_rows).T
    bw, cs, cd, (al_re, al_im) = _ssm_weights(a_re, a_im, bb_re_t, bb_im_t, c_re, c_im)
    row_spec = pl.BlockSpec((n_batch, steps, SSM_WIDTH), lambda i: (0, i, 0))
    const2 = lambda i: (0, 0)
    const3 = lambda i: (0, 0, 0)
    n_state = SSM_GROUPS * SSM_STATE
    f_rows = (steps // fold) * n_batch
    fw_all = fold * SSM_WIDTH
    return pl.pallas_call(
        _ssm_kernel,
        grid=(s // steps,),
        in_specs=[
            row_spec, row_spec,
            pl.BlockSpec(pin.shape, const3),
            pl.BlockSpec(pout.shape, const2),
            pl.BlockSpec(bw.shape, const3),
            pl.BlockSpec(cs.shape, const3),
            pl.BlockSpec(cd.shape, const3),
            pl.BlockSpec((1, n_state), const2),
            pl.BlockSpec((1, n_state), const2),
            pl.BlockSpec((1, SSM_WIDTH), const2),
            pl.BlockSpec((SSM_WIDTH, SSM_WIDTH), const2),
            pl.BlockSpec((1, SSM_WIDTH), const2),
        ],
        out_specs=row_spec,
        out_shape=jax.ShapeDtypeStruct((n_batch, s, SSM_WIDTH), BF16),
        scratch_shapes=[
            pltpu.VMEM((f_rows, fw_all), BF16),
            pltpu.VMEM((f_rows, 2 * n_state), F32),
            pltpu.VMEM((f_rows, 2 * n_state), BF16),
            pltpu.VMEM((f_rows, fw_all), F32),
            pltpu.VMEM((steps // perm_steps, fold * g_rows, SSM_WIDTH), BF16),
            pltpu.VMEM((V7X_SUBLANES, 2 * n_state), F32),
        ],
        compiler_params=_compiler_params(1),
        name="ssm_branch",
    )(u, zs, pin, pout, bw, cs, cd, al_re.reshape(1, n_state), al_im.reshape(1, n_state),
      d_skip.reshape(1, SSM_WIDTH), glu_w.astype(BF16), glu_b.reshape(1, SSM_WIDTH))


def _attn_kernel(lambda_init, qt_ref, k_ref, vt_ref, zat_ref, lq1_ref, lk1_ref, lq2_ref, lk2_ref,
                 subw_ref, y_ref, wq_ref, s_ref, mx_ref, acc_ref, m_ref, l_ref):
    n_tiles, _, tq = qt_ref.shape
    tk = tq
    diag_slot = 2

    def load_queries(q):
        wq_ref[0, :HEAD_DIM] = qt_ref[q, :HEAD_DIM]
        wq_ref[1, HEAD_DIM:] = qt_ref[q, HEAD_DIM:]

    def reset_stats():
        m_ref[...] = jnp.full_like(m_ref, MASK_VALUE)
        l_ref[...] = jnp.zeros_like(l_ref)
        acc_ref[...] = jnp.zeros_like(acc_ref)

    def scores(j, slot, masked):
        kb = k_ref[pl.ds(pl.multiple_of(j * tk, tk), tk), :]
        for c in range(2):
            s = jnp.dot(kb, wq_ref[c], preferred_element_type=F32)
            if masked:
                key_idx = lax.broadcasted_iota(jnp.int32, (tk, tq), 0)
                qry_idx = lax.broadcasted_iota(jnp.int32, (tk, tq), 1)
                s = jnp.where(key_idx <= qry_idx, s, MASK_VALUE)
            s_ref[slot, c] = s
            mx_ref[slot, c] = jnp.max(s, axis=0, keepdims=True)

    def accumulate(j, slot):
        vb = vt_ref[j]
        for c in range(2):
            m_old = m_ref[c]
            m_new = jnp.maximum(m_old, mx_ref[slot, c])
            alpha = jnp.exp2(m_old - m_new)
            p = jnp.exp2(s_ref[slot, c] - m_new)
            l_ref[c] = alpha * l_ref[c] + jnp.sum(p, axis=0, keepdims=True)
            acc_ref[c] = alpha * acc_ref[c] + jnp.dot(vb, p.astype(BF16), preferred_element_type=F32)
            m_ref[c] = m_new

    def finalize(q):
        lam = (jnp.exp(jnp.sum(lq1_ref[...] * lk1_ref[...], axis=-1, keepdims=True))
               - jnp.exp(jnp.sum(lq2_ref[...] * lk2_ref[...], axis=-1, keepdims=True)) + lambda_init)
        out = acc_ref[0] / l_ref[0] - lam * (acc_ref[1] / l_ref[1])
        ms = jnp.mean(out * out, axis=0, keepdims=True)
        out = out * lax.rsqrt(ms + EPS) * subw_ref[...] * (1.0 - lambda_init)
        out = out * jax.nn.silu(zat_ref[q].astype(F32))
        y_ref[pl.ds(pl.multiple_of(q * tq, tq), tq), :] = out.T.astype(BF16)

    zeros = jnp.zeros((HEAD_DIM, tq), BF16)
    wq_ref[0, HEAD_DIM:] = zeros
    wq_ref[1, :HEAD_DIM] = zeros
    load_queries(0)
    reset_stats()
    scores(0, 1, masked=True)

    def tile(q, carry):
        load_queries(q)
        last_block = jnp.maximum(q - 2, 0)

        @pl.when(q % 2 == 0)
        def _():
            scores(q, diag_slot, masked=True)
            accumulate(last_block, 0)
            finalize(q - 1)

        @pl.when(q % 2 == 1)
        def _():
            scores(q, diag_slot, masked=True)
            accumulate(last_block, 1)
            finalize(q - 1)

        reset_stats()
        scores(0, 0, masked=False)
        accumulate(q, diag_slot)

        def pair(i, c):
            j = 2 * i
            scores(j + 1, 1, masked=False)
            accumulate(j, 0)
            scores(j + 2, 0, masked=False)
            accumulate(j + 1, 1)
            return c

        lax.fori_loop(0, (q - 1) // 2, pair, 0)

        @pl.when(q % 2 == 0)
        def _():
            scores(q - 1, 1, masked=False)
            accumulate(q - 2, 0)

        return carry

    lax.fori_loop(1, n_tiles, tile, 0)
    accumulate(n_tiles - 2, (n_tiles - 2) % 2)
    finalize(n_tiles - 1)


def _attention(qt, k, vt, zat, lq1, lk1, lq2, lk2, subln_w, lambda_init):
    b, _, nt, _, tq = qt.shape
    s = nt * tq
    assert nt >= 2
    row64 = pl.BlockSpec((1, HEAD_DIM), lambda bi, hi: (0, 0))
    head_spec = pl.BlockSpec((None, None, nt, V_HEAD_DIM, tq), lambda bi, hi: (bi, hi, 0, 0, 0))
    token_spec = pl.BlockSpec((None, s, V_HEAD_DIM), lambda bi, hi: (bi, 0, hi))
    return pl.pallas_call(
        functools.partial(_attn_kernel, lambda_init),
        grid=(b, N_HEADS),
        in_specs=[
            head_spec, token_spec, head_spec, head_spec,
            row64, row64, row64, row64,
            pl.BlockSpec((V_HEAD_DIM, 1), lambda bi, hi: (0, 0)),
        ],
        out_specs=token_spec,
        out_shape=jax.ShapeDtypeStruct((b, s, ATTN_WIDTH), BF16),
        scratch_shapes=[
            pltpu.VMEM((2, 2 * HEAD_DIM, tq), BF16),
            pltpu.VMEM((3, 2, tq, tq), F32),
            pltpu.VMEM((3, 2, 1, tq), F32),
            pltpu.VMEM((2, V_HEAD_DIM, tq), F32),
            pltpu.VMEM((2, 1, tq), F32),
            pltpu.VMEM((2, 1, tq), F32),
        ],
        compiler_params=_compiler_params(2),
        name="diff_attention",
    )(qt, k, vt, zat, lq1.reshape(1, HEAD_DIM), lk1.reshape(1, HEAD_DIM),
      lq2.reshape(1, HEAD_DIM), lk2.reshape(1, HEAD_DIM), subln_w.reshape(V_HEAD_DIM, 1))


def _out_kernel(x_ref, ys_ref, ya_ref, p_ref, wos_ref, woa_ref, wg_ref, wp_ref, o_ref):
    x2 = (x_ref[...]
          + jnp.dot(ys_ref[...], wos_ref[...], preferred_element_type=F32)
          + jnp.dot(ya_ref[...], woa_ref[...], preferred_element_type=F32))
    gate = jax.nn.sigmoid(jnp.dot(x2.astype(BF16), wg_ref[...], preferred_element_type=F32))
    ple = jnp.dot(p_ref[...].astype(BF16), wp_ref[...], preferred_element_type=F32)
    o_ref[...] = x2 + gate * ple


def _out_proj(x, ys, ya, p, w_out, w_proj, w_gate):
    b, s, d = x.shape
    tm = TOKEN_TILE
    const2 = lambda bi, ti: (0, 0)
    return pl.pallas_call(
        _out_kernel,
        grid=(b, s // tm),
        in_specs=[
            pl.BlockSpec((None, tm, d), lambda bi, ti: (bi, ti, 0)),
            pl.BlockSpec((None, tm, SSM_WIDTH), lambda bi, ti: (bi, ti, 0)),
            pl.BlockSpec((None, tm, ATTN_WIDTH), lambda bi, ti: (bi, ti, 0)),
            pl.BlockSpec((None, tm, PLE_DIM), lambda bi, ti: (bi, ti, 0)),
            pl.BlockSpec((SSM_WIDTH, d), const2),
            pl.BlockSpec((ATTN_WIDTH, d), const2),
            pl.BlockSpec((d, d), const2),
            pl.BlockSpec((PLE_DIM, d), const2),
        ],
        out_specs=pl.BlockSpec((None, tm, d), lambda bi, ti: (bi, ti, 0)),
        out_shape=jax.ShapeDtypeStruct((b, s, d), F32),
        compiler_params=_compiler_params(2),
        name="out_proj",
    )(x, ys, ya, p, w_out[:SSM_WIDTH].astype(BF16), w_out[SSM_WIDTH:].astype(BF16),
      w_gate.astype(BF16), w_proj.astype(BF16))


def kernel(x, p, positions, norm_w, w_in, ssm_lambda_re, ssm_lambda_im, ssm_log_dt, ssm_b_re, ssm_b_im, ssm_c_re, ssm_c_im, ssm_d, glu_w, glu_b, q_norm_w, k_norm_w, lambda_q1, lambda_k1, lambda_q2, lambda_k2, subln_w, w_out, ple_w_proj, ple_w_gate):
    depth = norm_w.shape[0]
    for i in range(depth):
        lambda_init = 0.8 - 0.6 * math.exp(-0.3 * i)
        a_re, a_im, bb_re_t, bb_im_t = _ssm_params(
            ssm_lambda_re[i], ssm_lambda_im[i], ssm_log_dt[i], ssm_b_re[i], ssm_b_im[i])
        u, zs, qt, k, vt, zat = _in_proj(x, positions, norm_w[i], w_in[i], q_norm_w[i], k_norm_w[i])
        ys = _ssm_branch(u, zs, a_re, a_im, bb_re_t, bb_im_t, ssm_c_re[i], ssm_c_im[i],
                         ssm_d[i], glu_w[i], glu_b[i])
        ya = _attention(qt, k, vt, zat, lambda_q1[i], lambda_k1[i], lambda_q2[i], lambda_k2[i],
                        subln_w[i], lambda_init)
        x = _out_proj(x, ys, ya, p[i], w_out[i], ple_w_proj[i], ple_w_gate[i])
    return x
```

```python
import functools
import math

import jax
import jax.numpy as jnp
from jax import lax
from jax.experimental import pallas as pl
from jax.experimental.pallas import tpu as pltpu

F32 = jnp.float32
BF16 = jnp.bfloat16

D_MODEL = 1024
PLE_DIM = 256
SSM_WIDTH = 512
SSM_GROUP = 16
SSM_GROUPS = 32
SSM_STATE = 64
ATTN_WIDTH = 512
N_HEADS = 4
HEAD_DIM = 64
V_HEAD_DIM = 128
ROT_DIM = 16
ROT_HALF = 8
ROPE_THETA = 500000.0
EPS = 1e-6
LOG2E = math.log2(math.e)

V7X_LANES = 128
V7X_SUBLANES = 8
V7X_VMEM_BYTES = 64 * 1024 * 1024
VMEM_LIMIT_BYTES = V7X_VMEM_BYTES * 7 // 8

TOKEN_TILE = 512
SSM_FOLD = 4
SSM_STEPS = 128
SSM_PART_STEPS = 64
SSM_PERM_STEPS = 32
GROUPS_PER_BLOCK = 8
N_SSM_BLOCKS = SSM_GROUPS // GROUPS_PER_BLOCK
BLOCK_STATE = GROUPS_PER_BLOCK * SSM_STATE
MASK_VALUE = -1e30


def _compiler_params(n_grid_axes):
    return pltpu.CompilerParams(
        dimension_semantics=("arbitrary",) * n_grid_axes,
        vmem_limit_bytes=VMEM_LIMIT_BYTES,
    )


def _ssm_params_kernel(lr_ref, li_ref, logdt_ref, br_ref, bi_ref,
                       ar_ref, ai_ref, bbr_ref, bbi_ref):
    lr = lr_ref[...]
    li = li_ref[...]
    dt = jnp.exp(logdt_ref[...])
    mag = jnp.exp(lr * dt)
    ab_re = mag * jnp.cos(li * dt)
    ab_im = mag * jnp.sin(li * dt)
    nr = ab_re - 1.0
    ni = ab_im
    den = lr * lr + li * li
    coef_re = (nr * lr + ni * li) / den
    coef_im = (ni * lr - nr * li) / den
    br = br_ref[...]
    bi = bi_ref[...]
    ar_ref[...] = ab_re
    ai_ref[...] = ab_im
    bbr_ref[...] = coef_re * br - coef_im * bi
    bbi_ref[...] = coef_re * bi + coef_im * br


def _ssm_params(lam_re, lam_im, log_dt, b_re, b_im):
    g, n, p = SSM_GROUPS, SSM_STATE, SSM_GROUP
    gn = jax.ShapeDtypeStruct((g, 1, n), F32)
    gpn = jax.ShapeDtypeStruct((g, p, n), F32)
    return pl.pallas_call(
        _ssm_params_kernel,
        out_shape=(gn, gn, gpn, gpn),
        name="ssm_params",
    )(lam_re.reshape(g, 1, n), lam_im.reshape(g, 1, n), log_dt.reshape(g, 1, 1),
      jnp.swapaxes(b_re, 1, 2), jnp.swapaxes(b_im, 1, 2))


def _in_proj_kernel(x_ref, nw_ref, wnat_ref, wtr_ref, pos_ref, invf_ref, qnw_ref, knw_ref,
                    u_ref, zs_ref, qt_ref, k_ref, vt_ref, zat_ref):
    x = x_ref[...]
    ms = jnp.mean(x * x, axis=-1, keepdims=True)
    h = (x * lax.rsqrt(ms + EPS) * nw_ref[...]).astype(BF16)
    nat = jnp.dot(h, wnat_ref[...], preferred_element_type=F32)
    u_ref[...] = nat[:, :SSM_WIDTH].astype(BF16)
    zs_ref[...] = nat[:, SSM_WIDTH:].astype(BF16)
    tr = lax.dot_general(wtr_ref[...], h, (((1,), (1,)), ((), ())),
                         preferred_element_type=F32)
    tm = x.shape[0]
    ang = pos_ref[...].astype(F32) * invf_ref[...]
    cos = jnp.cos(ang)
    sin = jnp.sin(ang)

    def norm_rope(t, w_ref):
        t3 = t.reshape(2 * N_HEADS, HEAD_DIM, tm)
        ms3 = jnp.mean(t3 * t3, axis=1, keepdims=True)
        t3 = t3 * lax.rsqrt(ms3 + EPS) * w_ref[...]
        t1 = t3[:, :ROT_HALF, :]
        t2 = t3[:, ROT_HALF:ROT_DIM, :]
        r1 = t1 * cos - t2 * sin
        r2 = t2 * cos + t1 * sin
        t3 = jnp.concatenate([r1, r2, t3[:, ROT_DIM:, :]], axis=1)
        return t3.reshape(ATTN_WIDTH, tm)

    q = norm_rope(tr[0:ATTN_WIDTH], qnw_ref) * (HEAD_DIM ** -0.5 * LOG2E)
    qt_ref[...] = q.reshape(N_HEADS, 2 * HEAD_DIM, tm).astype(BF16)
    k = norm_rope(tr[ATTN_WIDTH:2 * ATTN_WIDTH], knw_ref)
    k_ref[...] = k.T.astype(BF16)
    vt_ref[...] = tr[2 * ATTN_WIDTH:3 * ATTN_WIDTH].reshape(N_HEADS, V_HEAD_DIM, tm).astype(BF16)
    zat_ref[...] = tr[3 * ATTN_WIDTH:].reshape(N_HEADS, V_HEAD_DIM, tm).astype(BF16)


def _in_proj(x, positions, norm_w, w_in, q_norm_w, k_norm_w):
    b, s, d = x.shape
    tm = TOKEN_TILE
    nt = s // tm
    w_nat = w_in[:, :2 * SSM_WIDTH].astype(BF16)
    w_tr = w_in[:, 2 * SSM_WIDTH:].T.astype(BF16)
    inv_freq = ROPE_THETA ** (-jnp.arange(0, ROT_DIM, 2, dtype=F32) / ROT_DIM)
    head_major = jax.ShapeDtypeStruct((b, N_HEADS, nt, V_HEAD_DIM, tm), BF16)
    head_spec = pl.BlockSpec((None, N_HEADS, None, V_HEAD_DIM, tm), lambda bi, ti: (bi, 0, ti, 0, 0))
    ssm_shape = jax.ShapeDtypeStruct((b, s, SSM_WIDTH), BF16)
    ssm_spec = pl.BlockSpec((None, tm, SSM_WIDTH), lambda bi, ti: (bi, ti, 0))
    const2 = lambda bi, ti: (0, 0)
    const3 = lambda bi, ti: (0, 0, 0)
    return pl.pallas_call(
        _in_proj_kernel,
        grid=(b, nt),
        in_specs=[
            pl.BlockSpec((None, tm, d), lambda bi, ti: (bi, ti, 0)),
            pl.BlockSpec((1, d), const2),
            pl.BlockSpec((d, 2 * SSM_WIDTH), const2),
            pl.BlockSpec((4 * ATTN_WIDTH, d), const2),
            pl.BlockSpec((None, 1, tm), lambda bi, ti: (bi, 0, ti)),
            pl.BlockSpec((ROT_HALF, 1), const2),
            pl.BlockSpec((1, HEAD_DIM, 1), const3),
            pl.BlockSpec((1, HEAD_DIM, 1), const3),
        ],
        out_specs=[
            ssm_spec, ssm_spec,
            head_spec,
            pl.BlockSpec((None, tm, ATTN_WIDTH), lambda bi, ti: (bi, ti, 0)),
            head_spec, head_spec,
        ],
        out_shape=(ssm_shape, ssm_shape, head_major,
                   jax.ShapeDtypeStruct((b, s, ATTN_WIDTH), BF16), head_major, head_major),
        compiler_params=_compiler_params(2),
        name="in_proj",
    )(x, norm_w.reshape(1, d), w_nat, w_tr, positions.reshape(b, 1, s),
      inv_freq.reshape(ROT_HALF, 1), q_norm_w.reshape(1, HEAD_DIM, 1), k_norm_w.reshape(1, HEAD_DIM, 1))


def _ssm_kernel(u_ref, zs_ref, pin_ref, pout_ref, bw_ref, cs_ref, cd_ref, ar_ref, ai_ref, d_ref,
                gw_ref, gb_ref, y_ref, uf_ref, bu_ref, xs_ref, y2_ref, yg_ref, state_ref):
    n_batch, steps, width = u_ref.shape
    fold, perm_out_rows, perm_in_rows = pin_ref.shape
    perm_steps = perm_in_rows // n_batch
    part_steps = SSM_PART_STEPS
    perms_per_part = part_steps // perm_steps
    part_rows = perms_per_part * perm_out_rows
    n_parts = steps // part_steps
    blk = 2 * BLOCK_STATE
    fw = fold * V7X_LANES
    pack_rows = 2 * V7X_SUBLANES

    @pl.when(pl.program_id(0) == 0)
    def _():
        state_ref[...] = jnp.zeros_like(state_ref)

    def rows_of(part):
        return slice(part * part_rows, (part + 1) * part_rows)

    def fold_in(part):
        for h in range(perms_per_part):
            t0 = part * part_steps + h * perm_steps
            u_bt = u_ref[:, t0:t0 + perm_steps, :].reshape(perm_in_rows, width)
            r0 = part * part_rows + h * perm_out_rows
            for i in range(fold):
                sel = jnp.dot(pin_ref[i], u_bt, preferred_element_type=F32).astype(BF16)
                for m in range(N_SSM_BLOCKS):
                    uf_ref[r0:r0 + perm_out_rows, m * fw + i * V7X_LANES:m * fw + (i + 1) * V7X_LANES] = (
                        sel[:, m * V7X_LANES:(m + 1) * V7X_LANES])

    def state_block(part, m):
        rows = rows_of(part)
        re = slice(m * blk, m * blk + BLOCK_STATE)
        im = slice(m * blk + BLOCK_STATE, (m + 1) * blk)
        bu_ref[rows, m * blk:(m + 1) * blk] = jnp.dot(
            uf_ref[rows, m * fw:(m + 1) * fw], bw_ref[m], preferred_element_type=F32)
        a_re = jnp.broadcast_to(ar_ref[:, m * BLOCK_STATE:(m + 1) * BLOCK_STATE], (V7X_SUBLANES, BLOCK_STATE))
        a_im = jnp.broadcast_to(ai_ref[:, m * BLOCK_STATE:(m + 1) * BLOCK_STATE], (V7X_SUBLANES, BLOCK_STATE))
        x_re = state_ref[:, re]
        x_im = state_ref[:, im]
        for r0 in range(part * part_rows, (part + 1) * part_rows, pack_rows):
            ins_re, ins_im = [], []
            for half in range(2):
                rr = slice(r0 + half * V7X_SUBLANES, r0 + (half + 1) * V7X_SUBLANES)
                ins_re.append(x_re)
                ins_im.append(x_im)
                n_re = a_re * x_re - a_im * x_im + bu_ref[rr, re]
                n_im = a_re * x_im + a_im * x_re + bu_ref[rr, im]
                x_re, x_im = n_re, n_im
            xs_ref[r0:r0 + pack_rows, re] = jnp.concatenate(ins_re, axis=0).astype(BF16)
            xs_ref[r0:r0 + pack_rows, im] = jnp.concatenate(ins_im, axis=0).astype(BF16)
        state_ref[:, re] = x_re
        state_ref[:, im] = x_im
        y2_ref[rows, m * fw:(m + 1) * fw] = (
            jnp.dot(xs_ref[rows, m * blk:(m + 1) * blk], cs_ref[m], preferred_element_type=F32)
            + jnp.dot(uf_ref[rows, m * fw:(m + 1) * fw], cd_ref[m], preferred_element_type=F32))

    def gate_offset(part, i):
        rows = rows_of(part)
        cols = [slice(m * fw + i * V7X_LANES, m * fw + (i + 1) * V7X_LANES) for m in range(N_SSM_BLOCKS)]
        y = jnp.concatenate([y2_ref[rows, c] for c in cols], axis=-1)
        u_i = jnp.concatenate([uf_ref[rows, c] for c in cols], axis=-1).astype(F32)
        y = jax.nn.gelu(y + d_ref[...] * u_i)
        gate = jnp.dot(y.astype(BF16), gw_ref[...], preferred_element_type=F32) + gb_ref[...]
        y = (y * jax.nn.sigmoid(gate)).astype(BF16)
        for h in range(perms_per_part):
            yg_ref[part * perms_per_part + h, i * perm_out_rows:(i + 1) * perm_out_rows, :] = (
                y[h * perm_out_rows:(h + 1) * perm_out_rows])

    def unfold_out(part):
        for h in range(perms_per_part):
            t0 = part * part_steps + h * perm_steps
            y_bt = jnp.dot(pout_ref[...], yg_ref[part * perms_per_part + h], preferred_element_type=F32)
            z = zs_ref[:, t0:t0 + perm_steps, :].reshape(perm_in_rows, width).astype(F32)
            y_ref[:, t0:t0 + perm_steps, :] = (
                (y_bt * jax.nn.silu(z)).astype(BF16).reshape(n_batch, perm_steps, width))

    assert fold == N_SSM_BLOCKS
    for part in range(n_parts + 1):
        if part < n_parts:
            fold_in(part)
        for j in range(N_SSM_BLOCKS):
            if part < n_parts:
                state_block(part, j)
            if part >= 1:
                gate_offset(part - 1, j)
        if part >= 1:
            unfold_out(part - 1)


def _ssm_weights(a_re, a_im, bb_re_t, bb_im_t, c_re, c_im):
    fold, gpb, nb, p, n = SSM_FOLD, GROUPS_PER_BLOCK, N_SSM_BLOCKS, SSM_GROUP, SSM_STATE
    hi = lax.Precision.HIGHEST
    a_re = a_re.reshape(SSM_GROUPS, n)
    a_im = a_im.reshape(SSM_GROUPS, n)
    in_lanes, state_lanes = fold * gpb * p, 2 * BLOCK_STATE

    def cmul(xr, xi, yr, yi):
        return xr * yr - xi * yi, xr * yi + xi * yr

    powers = [(jnp.ones_like(a_re), jnp.zeros_like(a_im))]
    for _ in range(fold):
        powers.append(cmul(powers[-1][0], powers[-1][1], a_re, a_im))
    pw = [(r[:, None, :], i[:, None, :]) for r, i in powers]

    in_group = (jnp.arange(in_lanes) // p) % gpb
    state_group = (jnp.arange(state_lanes) // n) % gpb
    same_group = (in_group[:, None] == state_group[None, :]).astype(F32)

    def input_by_state(t):
        t = t.reshape(fold, 2, nb, gpb, p, n).transpose(2, 0, 4, 1, 3, 5).reshape(nb, fold, 1, p, state_lanes)
        t = jnp.broadcast_to(t, (nb, fold, gpb, p, state_lanes)).reshape(nb, in_lanes, state_lanes)
        return (t * same_group).astype(BF16)

    ab = [cmul(pw[fold - 1 - i][0], pw[fold - 1 - i][1], bb_re_t, bb_im_t) for i in range(fold)]
    bw = input_by_state(jnp.stack([jnp.stack(t) for t in ab]))
    ca = [cmul(c_re, c_im, pw[i + 1][0], pw[i + 1][1]) for i in range(fold)]
    cs = jnp.swapaxes(input_by_state(jnp.stack([jnp.stack([r, -im]) for r, im in ca])), 1, 2)
    lags = []
    for lag in range(fold):
        cr, ci = cmul(c_re, c_im, pw[lag][0], pw[lag][1])
        lags.append(jnp.einsum('gpn,gqn->gqp', cr, bb_re_t, precision=hi)
                    - jnp.einsum('gpn,gqn->gqp', ci, bb_im_t, precision=hi))
    zero = jnp.zeros_like(lags[0])
    toep = jnp.stack([jnp.stack([lags[i - k] if i >= k else zero for i in range(fold)]) for k in range(fold)])
    toep = toep.reshape(fold, fold, nb, gpb, p, p).transpose(2, 0, 3, 4, 1, 5).reshape(nb * in_lanes, fold * p)
    col = jnp.arange(in_lanes)
    spread = (jnp.arange(fold * p)[:, None] == ((col // (gpb * p)) * p + col % p)[None, :]).astype(F32)
    cd = jnp.dot(toep, spread, precision=hi).reshape(nb, in_lanes, in_lanes)
    cd = (cd * (in_group[:, None] == in_group[None, :]).astype(F32)).astype(BF16)
    return bw, cs, cd, powers[fold]


def _ssm_branch(u, zs, a_re, a_im, bb_re_t, bb_im_t, c_re, c_im, d_skip, glu_w, glu_b):
    n_batch, s, _ = u.shape
    assert n_batch == V7X_SUBLANES, "one time group of all batches must fill one sublane tile"
    fold, steps, perm_steps = SSM_FOLD, SSM_STEPS, SSM_PERM_STEPS
    g_rows = (perm_steps // fold) * n_batch
    n_rows = perm_steps * n_batch
    r = jnp.arange(g_rows)
    src = (r % n_batch) * perm_steps + fold * (r // n_batch)
    pin = (jnp.arange(n_rows)[None, None, :] == (src[None, :, None] + jnp.arange(fold)[:, None, None])).astype(BF16)
    pout = pin.reshape(fold * g_rows, n_rows).T
    bw, cs, cd, (al_re, al_im) = _ssm_weights(a_re, a_im, bb_re_t, bb_im_t, c_re, c_im)
    row_spec = pl.BlockSpec((n_batch, steps, SSM_WIDTH), lambda i: (0, i, 0))
    const2 = lambda i: (0, 0)
    const3 = lambda i: (0, 0, 0)
    n_state = SSM_GROUPS * SSM_STATE
    f_rows = (steps // fold) * n_batch
    fw_all = fold * SSM_WIDTH
    return pl.pallas_call(
        _ssm_kernel,
        grid=(s // steps,),
        in_specs=[
            row_spec, row_spec,
            pl.BlockSpec(pin.shape, const3),
            pl.BlockSpec(pout.shape, const2),
            pl.BlockSpec(bw.shape, const3),
            pl.BlockSpec(cs.shape, const3),
            pl.BlockSpec(cd.shape, const3),
            pl.BlockSpec((1, n_state), const2),
            pl.BlockSpec((1, n_state), const2),
            pl.BlockSpec((1, SSM_WIDTH), const2),
            pl.BlockSpec((SSM_WIDTH, SSM_WIDTH), const2),
            pl.BlockSpec((1, SSM_WIDTH), const2),
        ],
        out_specs=row_spec,
        out_shape=jax.ShapeDtypeStruct((n_batch, s, SSM_WIDTH), BF16),
        scratch_shapes=[
            pltpu.VMEM((f_rows, fw_all), BF16),
            pltpu.VMEM((f_rows, 2 * n_state), F32),
            pltpu.VMEM((f_rows, 2 * n_state), BF16),
            pltpu.VMEM((f_rows, fw_all), F32),
            pltpu.VMEM((steps // perm_steps, fold * g_rows, SSM_WIDTH), BF16),
            pltpu.VMEM((V7X_SUBLANES, 2 * n_state), F32),
        ],
        compiler_params=_compiler_params(1),
        name="ssm_branch",
    )(u, zs, pin, pout, bw, cs, cd, al_re.reshape(1, n_state), al_im.reshape(1, n_state),
      d_skip.reshape(1, SSM_WIDTH), glu_w.astype(BF16), glu_b.reshape(1, SSM_WIDTH))


def _attn_kernel(lambda_init, qt_ref, k_ref, vt_ref, zat_ref, lq1_ref, lk1_ref, lq2_ref, lk2_ref,
                 subw_ref, y_ref, wq_ref, s_ref, mx_ref, acc_ref, m_ref, l_ref):
    n_tiles, _, tq = qt_ref.shape
    tk = tq
    diag_slot = 2

    def load_queries(q):
        wq_ref[0, :HEAD_DIM] = qt_ref[q, :HEAD_DIM]
        wq_ref[1, HEAD_DIM:] = qt_ref[q, HEAD_DIM:]

    def reset_stats():
        m_ref[...] = jnp.full_like(m_ref, MASK_VALUE)
        l_ref[...] = jnp.zeros_like(l_ref)
        acc_ref[...] = jnp.zeros_like(acc_ref)

    def scores(j, slot, masked):
        kb = k_ref[pl.ds(pl.multiple_of(j * tk, tk), tk), :]
        for c in range(2):
            s = jnp.dot(kb, wq_ref[c], preferred_element_type=F32)
            if masked:
                key_idx = lax.broadcasted_iota(jnp.int32, (tk, tq), 0)
                qry_idx = lax.broadcasted_iota(jnp.int32, (tk, tq), 1)
                s = jnp.where(key_idx <= qry_idx, s, MASK_VALUE)
            s_ref[slot, c] = s
            mx_ref[slot, c] = jnp.max(s, axis=0, keepdims=True)

    def accumulate(j, slot):
        vb = vt_ref[j]
        for c in range(2):
            m_old = m_ref[c]
            m_new = jnp.maximum(m_old, mx_ref[slot, c])
            alpha = jnp.exp2(m_old - m_new)
            p = jnp.exp2(s_ref[slot, c] - m_new)
            l_ref[c] = alpha * l_ref[c] + jnp.sum(p, axis=0, keepdims=True)
            acc_ref[c] = alpha * acc_ref[c] + jnp.dot(vb, p.astype(BF16), preferred_element_type=F32)
            m_ref[c] = m_new

    def finalize(q):
        lam = (jnp.exp(jnp.sum(lq1_ref[...] * lk1_ref[...], axis=-1, keepdims=True))
               - jnp.exp(jnp.sum(lq2_ref[...] * lk2_ref[...], axis=-1, keepdims=True)) + lambda_init)
        out = acc_ref[0] / l_ref[0] - lam * (acc_ref[1] / l_ref[1])
        ms = jnp.mean(out * out, axis=0, keepdims=True)
        out = out * lax.rsqrt(ms + EPS) * subw_ref[...] * (1.0 - lambda_init)
        out = out * jax.nn.silu(zat_ref[q].astype(F32))
        y_ref[pl.ds(pl.multiple_of(q * tq, tq), tq), :] = out.T.astype(BF16)

    zeros = jnp.zeros((HEAD_DIM, tq), BF16)
    wq_ref[0, HEAD_DIM:] = zeros
    wq_ref[1, :HEAD_DIM] = zeros
    load_queries(0)
    reset_stats()
    scores(0, 1, masked=True)

    def tile(q, carry):
        load_queries(q)
        last_block = jnp.maximum(q - 2, 0)

        @pl.when(q % 2 == 0)
        def _():
            scores(q, diag_slot, masked=True)
            accumulate(last_block, 0)
            finalize(q - 1)

        @pl.when(q % 2 == 1)
        def _():
            scores(q, diag_slot, masked=True)
            accumulate(last_block, 1)
            finalize(q - 1)

        reset_stats()
        scores(0, 0, masked=False)
        accumulate(q, diag_slot)

        def pair(i, c):
            j = 2 * i
            scores(j + 1, 1, masked=False)
            accumulate(j, 0)
            scores(j + 2, 0, masked=False)
            accumulate(j + 1, 1)
            return c

        lax.fori_loop(0, (q - 1) // 2, pair, 0)

        @pl.when(q % 2 == 0)
        def _():
            scores(q - 1, 1, masked=False)
            accumulate(q - 2, 0)

        return carry

    lax.fori_loop(1, n_tiles, tile, 0)
    accumulate(n_tiles - 2, (n_tiles - 2) % 2)
    finalize(n_tiles - 1)


def _attention(qt, k, vt, zat, lq1, lk1, lq2, lk2, subln_w, lambda_init):
    b, _, nt, _, tq = qt.shape
    s = nt * tq
    assert nt >= 2
    row64 = pl.BlockSpec((1, HEAD_DIM), lambda bi, hi: (0, 0))
    head_spec = pl.BlockSpec((None, None, nt, V_HEAD_DIM, tq), lambda bi, hi: (bi, hi, 0, 0, 0))
    token_spec = pl.BlockSpec((None, s, V_HEAD_DIM), lambda bi, hi: (bi, 0, hi))
    return pl.pallas_call(
        functools.partial(_attn_kernel, lambda_init),
        grid=(b, N_HEADS),
        in_specs=[
            head_spec, token_spec, head_spec, head_spec,
            row64, row64, row64, row64,
            pl.BlockSpec((V_HEAD_DIM, 1), lambda bi, hi: (0, 0)),
        ],
        out_specs=token_spec,
        out_shape=jax.ShapeDtypeStruct((b, s, ATTN_WIDTH), BF16),
        scratch_shapes=[
            pltpu.VMEM((2, 2 * HEAD_DIM, tq), BF16),
            pltpu.VMEM((3, 2, tq, tq), F32),
            pltpu.VMEM((3, 2, 1, tq), F32),
            pltpu.VMEM((2, V_HEAD_DIM, tq), F32),
            pltpu.VMEM((2, 1, tq), F32),
            pltpu.VMEM((2, 1, tq), F32),
        ],
        compiler_params=_compiler_params(2),
        name="diff_attention",
    )(qt, k, vt, zat, lq1.reshape(1, HEAD_DIM), lk1.reshape(1, HEAD_DIM),
      lq2.reshape(1, HEAD_DIM), lk2.reshape(1, HEAD_DIM), subln_w.reshape(V_HEAD_DIM, 1))


def _out_kernel(x_ref, ys_ref, ya_ref, p_ref, wos_ref, woa_ref, wg_ref, wp_ref, o_ref):
    x2 = (x_ref[...]
          + jnp.dot(ys_ref[...], wos_ref[...], preferred_element_type=F32)
          + jnp.dot(ya_ref[...], woa_ref[...], preferred_element_type=F32))
    gate = jax.nn.sigmoid(jnp.dot(x2.astype(BF16), wg_ref[...], preferred_element_type=F32))
    ple = jnp.dot(p_ref[...].astype(BF16), wp_ref[...], preferred_element_type=F32)
    o_ref[...] = x2 + gate * ple


def _out_proj(x, ys, ya, p, w_out, w_proj, w_gate):
    b, s, d = x.shape
    tm = TOKEN_TILE
    const2 = lambda bi, ti: (0, 0)
    return pl.pallas_call(
        _out_kernel,
        grid=(b, s // tm),
        in_specs=[
            pl.BlockSpec((None, tm, d), lambda bi, ti: (bi, ti, 0)),
            pl.BlockSpec((None, tm, SSM_WIDTH), lambda bi, ti: (bi, ti, 0)),
            pl.BlockSpec((None, tm, ATTN_WIDTH), lambda bi, ti: (bi, ti, 0)),
            pl.BlockSpec((None, tm, PLE_DIM), lambda bi, ti: (bi, ti, 0)),
            pl.BlockSpec((SSM_WIDTH, d), const2),
            pl.BlockSpec((ATTN_WIDTH, d), const2),
            pl.BlockSpec((d, d), const2),
            pl.BlockSpec((PLE_DIM, d), const2),
        ],
        out_specs=pl.BlockSpec((None, tm, d), lambda bi, ti: (bi, ti, 0)),
        out_shape=jax.ShapeDtypeStruct((b, s, d), F32),
        compiler_params=_compiler_params(2),
        name="out_proj",
    )(x, ys, ya, p, w_out[:SSM_WIDTH].astype(BF16), w_out[SSM_WIDTH:].astype(BF16),
      w_gate.astype(BF16), w_proj.astype(BF16))


def kernel(x, p, positions, norm_w, w_in, ssm_lambda_re, ssm_lambda_im, ssm_log_dt, ssm_b_re, ssm_b_im, ssm_c_re, ssm_c_im, ssm_d, glu_w, glu_b, q_norm_w, k_norm_w, lambda_q1, lambda_k1, lambda_q2, lambda_k2, subln_w, w_out, ple_w_proj, ple_w_gate):
    depth = norm_w.shape[0]
    for i in range(depth):
        lambda_init = 0.8 - 0.6 * math.exp(-0.3 * i)
        a_re, a_im, bb_re_t, bb_im_t = _ssm_params(
            ssm_lambda_re[i], ssm_lambda_im[i], ssm_log_dt[i], ssm_b_re[i], ssm_b_im[i])
        u, zs, qt, k, vt, zat = _in_proj(x, positions, norm_w[i], w_in[i], q_norm_w[i], k_norm_w[i])
        ys = _ssm_branch(u, zs, a_re, a_im, bb_re_t, bb_im_t, ssm_c_re[i], ssm_c_im[i],
                         ssm_d[i], glu_w[i], glu_b[i])
        ya = _attention(qt, k, vt, zat, lambda_q1[i], lambda_k1[i], lambda_q2[i], lambda_k2[i],
                        subln_w[i], lambda_init)
        x = _out_proj(x, ys, ya, p[i], w_out[i], ple_w_proj[i], ple_w_gate[i])
    return x
```

```python
import functools
import math

import jax
import jax.numpy as jnp
import numpy as np
from jax import lax
from jax.experimental import pallas as pl
from jax.experimental.pallas import tpu as pltpu

F32 = jnp.float32
BF16 = jnp.bfloat16

D_MODEL = 1024
PLE_DIM = 256
SSM_WIDTH = 512
SSM_GROUP = 16
SSM_GROUPS = 32
SSM_STATE = 64
ATTN_WIDTH = 512
N_HEADS = 4
HEAD_DIM = 64
V_HEAD_DIM = 128
ROT_DIM = 16
ROT_HALF = 8
ROPE_THETA = 500000.0
EPS = 1e-6
LOG2E = math.log2(math.e)

V7X_LANES = 128
V7X_SUBLANES = 8
V7X_VMEM_BYTES = 64 * 1024 * 1024
VMEM_LIMIT_BYTES = V7X_VMEM_BYTES * 7 // 8

TOKEN_TILE = 512
SSM_FOLD = 4
SSM_STEPS = 128
SSM_PART_STEPS = 64
SSM_PERM_STEPS = 32
GROUPS_PER_BLOCK = 8
N_SSM_BLOCKS = SSM_GROUPS // GROUPS_PER_BLOCK
BLOCK_STATE = GROUPS_PER_BLOCK * SSM_STATE
MASK_VALUE = -1e30


def _compiler_params(n_grid_axes):
    return pltpu.CompilerParams(
        dimension_semantics=("arbitrary",) * n_grid_axes,
        vmem_limit_bytes=VMEM_LIMIT_BYTES,
    )


def _ssm_params_kernel(lr_ref, li_ref, logdt_ref, br_ref, bi_ref,
                       ar_ref, ai_ref, bbr_ref, bbi_ref):
    lr = lr_ref[...]
    li = li_ref[...]
    dt = jnp.exp(logdt_ref[...])
    mag = jnp.exp(lr * dt)
    ab_re = mag * jnp.cos(li * dt)
    ab_im = mag * jnp.sin(li * dt)
    nr = ab_re - 1.0
    ni = ab_im
    den = lr * lr + li * li
    coef_re = (nr * lr + ni * li) / den
    coef_im = (ni * lr - nr * li) / den
    br = br_ref[...]
    bi = bi_ref[...]
    ar_ref[...] = ab_re
    ai_ref[...] = ab_im
    bbr_ref[...] = coef_re * br - coef_im * bi
    bbi_ref[...] = coef_re * bi + coef_im * br


def _ssm_params(lam_re, lam_im, log_dt, b_re, b_im):
    g, n, p = SSM_GROUPS, SSM_STATE, SSM_GROUP
    gn = jax.ShapeDtypeStruct((g, 1, n), F32)
    gpn = jax.ShapeDtypeStruct((g, p, n), F32)
    return pl.pallas_call(
        _ssm_params_kernel,
        out_shape=(gn, gn, gpn, gpn),
        name="ssm_params",
    )(lam_re.reshape(g, 1, n), lam_im.reshape(g, 1, n), log_dt.reshape(g, 1, 1),
      jnp.swapaxes(b_re, 1, 2), jnp.swapaxes(b_im, 1, 2))


def _in_proj_kernel(x_ref, nw_ref, wnat_ref, wtr_ref, pos_ref, invf_ref, qnw_ref, knw_ref,
                    u_ref, zs_ref, qt_ref, k_ref, vt_ref, zat_ref):
    x = x_ref[...]
    ms = jnp.mean(x * x, axis=-1, keepdims=True)
    h = (x * lax.rsqrt(ms + EPS) * nw_ref[...]).astype(BF16)
    nat = jnp.dot(h, wnat_ref[...], preferred_element_type=F32)
    u_ref[...] = nat[:, :SSM_WIDTH].astype(BF16)
    zs_ref[...] = nat[:, SSM_WIDTH:].astype(BF16)
    tr = lax.dot_general(wtr_ref[...], h, (((1,), (1,)), ((), ())),
                         preferred_element_type=F32)
    tm = x.shape[0]
    ang = pos_ref[...].astype(F32) * invf_ref[...]
    cos = jnp.cos(ang)
    sin = jnp.sin(ang)

    def norm_rope(t, w_ref):
        t3 = t.reshape(2 * N_HEADS, HEAD_DIM, tm)
        ms3 = jnp.mean(t3 * t3, axis=1, keepdims=True)
        t3 = t3 * lax.rsqrt(ms3 + EPS) * w_ref[...]
        t1 = t3[:, :ROT_HALF, :]
        t2 = t3[:, ROT_HALF:ROT_DIM, :]
        r1 = t1 * cos - t2 * sin
        r2 = t2 * cos + t1 * sin
        t3 = jnp.concatenate([r1, r2, t3[:, ROT_DIM:, :]], axis=1)
        return t3.reshape(ATTN_WIDTH, tm)

    q = norm_rope(tr[0:ATTN_WIDTH], qnw_ref) * (HEAD_DIM ** -0.5 * LOG2E)
    qt_ref[...] = q.reshape(N_HEADS, 2 * HEAD_DIM, tm).astype(BF16)
    k = norm_rope(tr[ATTN_WIDTH:2 * ATTN_WIDTH], knw_ref)
    k_ref[...] = k.T.astype(BF16)
    vt_ref[...] = tr[2 * ATTN_WIDTH:3 * ATTN_WIDTH].reshape(N_HEADS, V_HEAD_DIM, tm).astype(BF16)
    zat_ref[...] = tr[3 * ATTN_WIDTH:].reshape(N_HEADS, V_HEAD_DIM, tm).astype(BF16)


def _in_proj(x, positions, norm_w, w_in, q_norm_w, k_norm_w):
    b, s, d = x.shape
    tm = TOKEN_TILE
    nt = s // tm
    w_nat = w_in[:, :2 * SSM_WIDTH].astype(BF16)
    w_tr = w_in[:, 2 * SSM_WIDTH:].T.astype(BF16)
    inv_freq = ROPE_THETA ** (-jnp.arange(0, ROT_DIM, 2, dtype=F32) / ROT_DIM)
    head_major = jax.ShapeDtypeStruct((b, N_HEADS, nt, V_HEAD_DIM, tm), BF16)
    head_spec = pl.BlockSpec((None, N_HEADS, None, V_HEAD_DIM, tm), lambda bi, ti: (bi, 0, ti, 0, 0))
    ssm_shape = jax.ShapeDtypeStruct((b, s, SSM_WIDTH), BF16)
    ssm_spec = pl.BlockSpec((None, tm, SSM_WIDTH), lambda bi, ti: (bi, ti, 0))
    const2 = lambda bi, ti: (0, 0)
    const3 = lambda bi, ti: (0, 0, 0)
    return pl.pallas_call(
        _in_proj_kernel,
        grid=(b, nt),
        in_specs=[
            pl.BlockSpec((None, tm, d), lambda bi, ti: (bi, ti, 0)),
            pl.BlockSpec((1, d), const2),
            pl.BlockSpec((d, 2 * SSM_WIDTH), const2),
            pl.BlockSpec((4 * ATTN_WIDTH, d), const2),
            pl.BlockSpec((None, 1, tm), lambda bi, ti: (bi, 0, ti)),
            pl.BlockSpec((ROT_HALF, 1), const2),
            pl.BlockSpec((1, HEAD_DIM, 1), const3),
            pl.BlockSpec((1, HEAD_DIM, 1), const3),
        ],
        out_specs=[
            ssm_spec, ssm_spec,
            head_spec,
            pl.BlockSpec((None, tm, ATTN_WIDTH), lambda bi, ti: (bi, ti, 0)),
            head_spec, head_spec,
        ],
        out_shape=(ssm_shape, ssm_shape, head_major,
                   jax.ShapeDtypeStruct((b, s, ATTN_WIDTH), BF16), head_major, head_major),
        compiler_params=_compiler_params(2),
        name="in_proj",
    )(x, norm_w.reshape(1, d), w_nat, w_tr, positions.reshape(b, 1, s),
      inv_freq.reshape(ROT_HALF, 1), q_norm_w.reshape(1, HEAD_DIM, 1), k_norm_w.reshape(1, HEAD_DIM, 1))


def _ssm_kernel(u_ref, zs_ref, pin_ref, pout_ref, bw_ref, cs_ref, cd_ref, ar_ref, ai_ref, d_ref,
                gw_ref, gb_ref, y_ref, uf_ref, bu_ref, xs_ref, y2_ref, yg_ref, state_ref):
    n_batch, steps, width = u_ref.shape
    fold, perm_out_rows, perm_in_rows = pin_ref.shape
    perm_steps = perm_in_rows // n_batch
    part_steps = SSM_PART_STEPS
    perms_per_part = part_steps // perm_steps
    part_rows = perms_per_part * perm_out_rows
    n_parts = steps // part_steps
    blk = 2 * BLOCK_STATE
    fw = fold * V7X_LANES
    pack_rows = 2 * V7X_SUBLANES

    @pl.when(pl.program_id(0) == 0)
    def _():
        state_ref[...] = jnp.zeros_like(state_ref)

    def rows_of(part):
        return slice(part * part_rows, (part + 1) * part_rows)

    def fold_in(part):
        for h in range(perms_per_part):
            t0 = part * part_steps + h * perm_steps
            u_bt = u_ref[:, t0:t0 + perm_steps, :].reshape(perm_in_rows, width)
            r0 = part * part_rows + h * perm_out_rows
            for i in range(fold):
                sel = jnp.dot(pin_ref[i], u_bt, preferred_element_type=F32).astype(BF16)
                for m in range(N_SSM_BLOCKS):
                    uf_ref[r0:r0 + perm_out_rows, m * fw + i * V7X_LANES:m * fw + (i + 1) * V7X_LANES] = (
                        sel[:, m * V7X_LANES:(m + 1) * V7X_LANES])

    def state_block(part, m):
        rows = rows_of(part)
        re = slice(m * blk, m * blk + BLOCK_STATE)
        im = slice(m * blk + BLOCK_STATE, (m + 1) * blk)
        bu_ref[rows, m * blk:(m + 1) * blk] = jnp.dot(
            uf_ref[rows, m * fw:(m + 1) * fw], bw_ref[m], preferred_element_type=F32)
        a_re = jnp.broadcast_to(ar_ref[:, m * BLOCK_STATE:(m + 1) * BLOCK_STATE], (V7X_SUBLANES, BLOCK_STATE))
        a_im = jnp.broadcast_to(ai_ref[:, m * BLOCK_STATE:(m + 1) * BLOCK_STATE], (V7X_SUBLANES, BLOCK_STATE))
        x_re = state_ref[:, re]
        x_im = state_ref[:, im]
        for r0 in range(part * part_rows, (part + 1) * part_rows, pack_rows):
            ins_re, ins_im = [], []
            for half in range(2):
                rr = slice(r0 + half * V7X_SUBLANES, r0 + (half + 1) * V7X_SUBLANES)
                ins_re.append(x_re)
                ins_im.append(x_im)
                n_re = a_re * x_re - a_im * x_im + bu_ref[rr, re]
                n_im = a_re * x_im + a_im * x_re + bu_ref[rr, im]
                x_re, x_im = n_re, n_im
            xs_ref[r0:r0 + pack_rows, re] = jnp.concatenate(ins_re, axis=0).astype(BF16)
            xs_ref[r0:r0 + pack_rows, im] = jnp.concatenate(ins_im, axis=0).astype(BF16)
        state_ref[:, re] = x_re
        state_ref[:, im] = x_im
        y2_ref[rows, m * fw:(m + 1) * fw] = (
            jnp.dot(xs_ref[rows, m * blk:(m + 1) * blk], cs_ref[m], preferred_element_type=F32)
            + jnp.dot(uf_ref[rows, m * fw:(m + 1) * fw], cd_ref[m], preferred_element_type=F32))

    def gate_offset(part, i):
        rows = rows_of(part)
        cols = [slice(m * fw + i * V7X_LANES, m * fw + (i + 1) * V7X_LANES) for m in range(N_SSM_BLOCKS)]
        y = jnp.concatenate([y2_ref[rows, c] for c in cols], axis=-1)
        u_i = jnp.concatenate([uf_ref[rows, c] for c in cols], axis=-1).astype(F32)
        y = jax.nn.gelu(y + d_ref[...] * u_i)
        gate = jnp.dot(y.astype(BF16), gw_ref[...], preferred_element_type=F32) + gb_ref[...]
        y = (y * jax.nn.sigmoid(gate)).astype(BF16)
        for h in range(perms_per_part):
            yg_ref[part * perms_per_part + h, i * perm_out_rows:(i + 1) * perm_out_rows, :] = (
                y[h * perm_out_rows:(h + 1) * perm_out_rows])

    def unfold_out(part):
        for h in range(perms_per_part):
            t0 = part * part_steps + h * perm_steps
            y_bt = jnp.dot(pout_ref[...], yg_ref[part * perms_per_part + h], preferred_element_type=F32)
            z = zs_ref[:, t0:t0 + perm_steps, :].reshape(perm_in_rows, width).astype(F32)
            y_ref[:, t0:t0 + perm_steps, :] = (
                (y_bt * jax.nn.silu(z)).astype(BF16).reshape(n_batch, perm_steps, width))

    assert fold == N_SSM_BLOCKS
    for part in range(n_parts + 1):
        if part < n_parts:
            fold_in(part)
        for j in range(N_SSM_BLOCKS):
            if part < n_parts:
                state_block(part, j)
            if part >= 1:
                gate_offset(part - 1, j)
        if part >= 1:
            unfold_out(part - 1)


def _ssm_weights(a_re, a_im, bb_re_t, bb_im_t, c_re, c_im):
    fold, gpb, nb, p, n = SSM_FOLD, GROUPS_PER_BLOCK, N_SSM_BLOCKS, SSM_GROUP, SSM_STATE
    hi = lax.Precision.HIGHEST
    in_lanes, state_lanes = fold * gpb * p, 2 * BLOCK_STATE
    a = (a_re.reshape(SSM_GROUPS, 1, n), a_im.reshape(SSM_GROUPS, 1, n))

    def cmul(xr, xi, yr, yi):
        return xr * yr - xi * yi, xr * yi + xi * yr

    powers = [(jnp.ones_like(a[0]), jnp.zeros_like(a[1]))]
    for _ in range(fold):
        powers.append(cmul(*powers[-1], *a))
    def stacked(idx):
        return jnp.stack([powers[e][0] for e in idx]), jnp.stack([powers[e][1] for e in idx])

    in_group = (np.arange(in_lanes) // p) % gpb
    state_group = (np.arange(state_lanes) // n) % gpb
    group_of_state = (np.arange(gpb)[:, None] == state_group[None, :]).astype(np.float32)
    group_of_state = group_of_state.reshape(1, 1, gpb, 1, state_lanes)
    same_in_group = (in_group[:, None] == in_group[None, :]).astype(np.float32)

    def input_by_state(t_re, t_im):
        t = jnp.stack([t_re, t_im], axis=1).reshape(fold, 2, nb, gpb, p, n)
        t = t.transpose(2, 0, 4, 1, 3, 5).reshape(nb, fold, 1, p, state_lanes)
        return (t * group_of_state).astype(BF16).reshape(nb, in_lanes, state_lanes)

    bw = input_by_state(*cmul(*stacked(range(fold - 1, -1, -1)), bb_re_t, bb_im_t))
    ca_re, ca_im = cmul(c_re, c_im, *stacked(range(1, fold + 1)))
    cs = jnp.swapaxes(input_by_state(ca_re, -ca_im), 1, 2)
    cl_re, cl_im = cmul(c_re, c_im, *stacked(range(fold)))
    lags = jnp.einsum('lgpn,gqn->lgqp', jnp.concatenate([cl_re, -cl_im], axis=-1),
                      jnp.concatenate([bb_re_t, bb_im_t], axis=-1), precision=hi)
    zero = jnp.zeros_like(lags[0])
    toep = jnp.stack([jnp.stack([lags[i - k] if i >= k else zero for i in range(fold)]) for k in range(fold)])
    toep = toep.reshape(fold, fold, nb, gpb, p, p).transpose(2, 0, 3, 4, 1, 5).reshape(nb * in_lanes, fold * p)
    col = np.arange(in_lanes)
    spread = (np.arange(fold * p)[:, None] == ((col // (gpb * p)) * p + col % p)[None, :]).astype(np.float32)
    cd = jnp.dot(toep, spread, precision=hi).reshape(nb, in_lanes, in_lanes)
    cd = (cd * same_in_group).astype(BF16)
    return bw, cs, cd, (powers[fold][0], powers[fold][1])


def _ssm_branch(u, zs, a_re, a_im, bb_re_t, bb_im_t, c_re, c_im, d_skip, glu_w, glu_b):
    n_batch, s, _ = u.shape
    assert n_batch == V7X_SUBLANES, "one time group of all batches must fill one sublane tile"
    fold, steps, perm_steps = SSM_FOLD, SSM_STEPS, SSM_PERM_STEPS
    g_rows = (perm_steps // fold) * n_batch
    n_rows = perm_steps * n_batch
    r = np.arange(g_rows)
    src = (r % n_batch) * perm_steps + fold * (r // n_batch)
    pin = np.arange(n_rows)[None, None, :] == (src[None, :, None] + np.arange(fold)[:, None, None])
    pout = jnp.asarray(pin.reshape(fold * g_rows, n_rows).T, dtype=BF16)
    pin = jnp.asarray(pin, dtype=BF16)
    bw, cs, cd, (al_re, al_im) = _ssm_weights(a_re, a_im, bb_re_t, bb_im_t, c_re, c_im)
    row_spec = pl.BlockSpec((n_batch, steps, SSM_WIDTH), lambda i: (0, i, 0))
    const2 = lambda i: (0, 0)
    const3 = lambda i: (0, 0, 0)
    n_state = SSM_GROUPS * SSM_STATE
    f_rows = (steps // fold) * n_batch
    fw_all = fold * SSM_WIDTH
    return pl.pallas_call(
        _ssm_kernel,
        grid=(s // steps,),
        in_specs=[
            row_spec, row_spec,
            pl.BlockSpec(pin.shape, const3),
            pl.BlockSpec(pout.shape, const2),
            pl.BlockSpec(bw.shape, const3),
            pl.BlockSpec(cs.shape, const3),
            pl.BlockSpec(cd.shape, const3),
            pl.BlockSpec((1, n_state), const2),
            pl.BlockSpec((1, n_state), const2),
            pl.BlockSpec((1, SSM_WIDTH), const2),
            pl.BlockSpec((SSM_WIDTH, SSM_WIDTH), const2),
            pl.BlockSpec((1, SSM_WIDTH), const2),
        ],
        out_specs=row_spec,
        out_shape=jax.ShapeDtypeStruct((n_batch, s, SSM_WIDTH), BF16),
        scratch_shapes=[
            pltpu.VMEM((f_rows, fw_all), BF16),
            pltpu.VMEM((f_rows, 2 * n_state), F32),
            pltpu.VMEM((f_rows, 2 * n_state), BF16),
            pltpu.VMEM((f_rows, fw_all), F32),
            pltpu.VMEM((steps // perm_steps, fold * g_rows, SSM_WIDTH), BF16),
            pltpu.VMEM((V7X_SUBLANES, 2 * n_state), F32),
        ],
        compiler_params=_compiler_params(1),
        name="ssm_branch",
    )(u, zs, pin, pout, bw, cs, cd, al_re.reshape(1, n_state), al_im.reshape(1, n_state),
      d_skip.reshape(1, SSM_WIDTH), glu_w.astype(BF16), glu_b.reshape(1, SSM_WIDTH))


def _attn_kernel(lambda_init, qt_ref, k_ref, vt_ref, zat_ref, lq1_ref, lk1_ref, lq2_ref, lk2_ref,
                 subw_ref, y_ref, wq_ref, s_ref, mx_ref, acc_ref, m_ref, l_ref):
    n_tiles, _, tq = qt_ref.shape
    tk = tq
    diag_slot = 2

    def load_queries(q):
        wq_ref[0, :HEAD_DIM] = qt_ref[q, :HEAD_DIM]
        wq_ref[1, HEAD_DIM:] = qt_ref[q, HEAD_DIM:]

    def reset_stats():
        m_ref[...] = jnp.full_like(m_ref, MASK_VALUE)
        l_ref[...] = jnp.zeros_like(l_ref)
        acc_ref[...] = jnp.zeros_like(acc_ref)

    def scores(j, slot, masked):
        kb = k_ref[pl.ds(pl.multiple_of(j * tk, tk), tk), :]
        for c in range(2):
            s = jnp.dot(kb, wq_ref[c], preferred_element_type=F32)
            if masked:
                key_idx = lax.broadcasted_iota(jnp.int32, (tk, tq), 0)
                qry_idx = lax.broadcasted_iota(jnp.int32, (tk, tq), 1)
                s = jnp.where(key_idx <= qry_idx, s, MASK_VALUE)
            s_ref[slot, c] = s
            mx_ref[slot, c] = jnp.max(s, axis=0, keepdims=True)

    def accumulate(j, slot):
        vb = vt_ref[j]
        for c in range(2):
            m_old = m_ref[c]
            m_new = jnp.maximum(m_old, mx_ref[slot, c])
            alpha = jnp.exp2(m_old - m_new)
            p = jnp.exp2(s_ref[slot, c] - m_new)
            l_ref[c] = alpha * l_ref[c] + jnp.sum(p, axis=0, keepdims=True)
            acc_ref[c] = alpha * acc_ref[c] + jnp.dot(vb, p.astype(BF16), preferred_element_type=F32)
            m_ref[c] = m_new

    def finalize(q):
        lam = (jnp.exp(jnp.sum(lq1_ref[...] * lk1_ref[...], axis=-1, keepdims=True))
               - jnp.exp(jnp.sum(lq2_ref[...] * lk2_ref[...], axis=-1, keepdims=True)) + lambda_init)
        out = acc_ref[0] / l_ref[0] - lam * (acc_ref[1] / l_ref[1])
        ms = jnp.mean(out * out, axis=0, keepdims=True)
        out = out * lax.rsqrt(ms + EPS) * subw_ref[...] * (1.0 - lambda_init)
        out = out * jax.nn.silu(zat_ref[q].astype(F32))
        y_ref[pl.ds(pl.multiple_of(q * tq, tq), tq), :] = out.T.astype(BF16)

    zeros = jnp.zeros((HEAD_DIM, tq), BF16)
    wq_ref[0, HEAD_DIM:] = zeros
    wq_ref[1, :HEAD_DIM] = zeros
    load_queries(0)
    reset_stats()
    scores(0, 1, masked=True)

    def enter_tile(q, pending_slot):
        load_queries(q)
        scores(q, diag_slot, masked=True)
        accumulate(jnp.maximum(q - 2, 0), pending_slot)
        finalize(q - 1)
        reset_stats()
        scores(0, 0, masked=False)
        accumulate(q, diag_slot)

    def pair(i, c):
        j = 2 * i
        scores(j + 1, 1, masked=False)
        accumulate(j, 0)
        scores(j + 2, 0, masked=False)
        accumulate(j + 1, 1)
        return c

    def two_tiles(k, carry):
        q_odd = 2 * k + 1
        enter_tile(q_odd, 1)
        lax.fori_loop(0, k, pair, 0)
        q_even = q_odd + 1
        enter_tile(q_even, 0)
        lax.fori_loop(0, k, pair, 0)
        scores(q_even - 1, 1, masked=False)
        accumulate(q_even - 2, 0)
        return carry

    assert n_tiles % 2 == 0
    lax.fori_loop(0, (n_tiles - 2) // 2, two_tiles, 0)
    q_last = n_tiles - 1
    enter_tile(q_last, 1)
    lax.fori_loop(0, (q_last - 1) // 2, pair, 0)
    accumulate(q_last - 1, 0)
    finalize(q_last)


def _attention(qt, k, vt, zat, lq1, lk1, lq2, lk2, subln_w, lambda_init):
    b, _, nt, _, tq = qt.shape
    s = nt * tq
    assert nt >= 2
    row64 = pl.BlockSpec((1, HEAD_DIM), lambda bi, hi: (0, 0))
    head_spec = pl.BlockSpec((None, None, nt, V_HEAD_DIM, tq), lambda bi, hi: (bi, hi, 0, 0, 0))
    token_spec = pl.BlockSpec((None, s, V_HEAD_DIM), lambda bi, hi: (bi, 0, hi))
    return pl.pallas_call(
        functools.partial(_attn_kernel, lambda_init),
        grid=(b, N_HEADS),
        in_specs=[
            head_spec, token_spec, head_spec, head_spec,
            row64, row64, row64, row64,
            pl.BlockSpec((V_HEAD_DIM, 1), lambda bi, hi: (0, 0)),
        ],
        out_specs=token_spec,
        out_shape=jax.ShapeDtypeStruct((b, s, ATTN_WIDTH), BF16),
        scratch_shapes=[
            pltpu.VMEM((2, 2 * HEAD_DIM, tq), BF16),
            pltpu.VMEM((3, 2, tq, tq), F32),
            pltpu.VMEM((3, 2, 1, tq), F32),
            pltpu.VMEM((2, V_HEAD_DIM, tq), F32),
            pltpu.VMEM((2, 1, tq), F32),
            pltpu.VMEM((2, 1, tq), F32),
        ],
        compiler_params=_compiler_params(2),
        name="diff_attention",
    )(qt, k, vt, zat, lq1.reshape(1, HEAD_DIM), lk1.reshape(1, HEAD_DIM),
      lq2.reshape(1, HEAD_DIM), lk2.reshape(1, HEAD_DIM), subln_w.reshape(V_HEAD_DIM, 1))


def _out_kernel(x_ref, ys_ref, ya_ref, p_ref, wos_ref, woa_ref, wg_ref, wp_ref, o_ref):
    x2 = (x_ref[...]
          + jnp.dot(ys_ref[...], wos_ref[...], preferred_element_type=F32)
          + jnp.dot(ya_ref[...], woa_ref[...], preferred_element_type=F32))
    gate = jax.nn.sigmoid(jnp.dot(x2.astype(BF16), wg_ref[...], preferred_element_type=F32))
    ple = jnp.dot(p_ref[...].astype(BF16), wp_ref[...], preferred_element_type=F32)
    o_ref[...] = x2 + gate * ple


def _out_proj(x, ys, ya, p, w_out, w_proj, w_gate):
    b, s, d = x.shape
    tm = TOKEN_TILE
    const2 = lambda bi, ti: (0, 0)
    return pl.pallas_call(
        _out_kernel,
        grid=(b, s // tm),
        in_specs=[
            pl.BlockSpec((None, tm, d), lambda bi, ti: (bi, ti, 0)),
            pl.BlockSpec((None, tm, SSM_WIDTH), lambda bi, ti: (bi, ti, 0)),
            pl.BlockSpec((None, tm, ATTN_WIDTH), lambda bi, ti: (bi, ti, 0)),
            pl.BlockSpec((None, tm, PLE_DIM), lambda bi, ti: (bi, ti, 0)),
            pl.BlockSpec((SSM_WIDTH, d), const2),
            pl.BlockSpec((ATTN_WIDTH, d), const2),
            pl.BlockSpec((d, d), const2),
            pl.BlockSpec((PLE_DIM, d), const2),
        ],
        out_specs=pl.BlockSpec((None, tm, d), lambda bi, ti: (bi, ti, 0)),
        out_shape=jax.ShapeDtypeStruct((b, s, d), F32),
        compiler_params=_compiler_params(2),
        name="out_proj",
    )(x, ys, ya, p, w_out[:SSM_WIDTH].astype(BF16), w_out[SSM_WIDTH:].astype(BF16),
      w_gate.astype(BF16), w_proj.astype(BF16))


def kernel(x, p, positions, norm_w, w_in, ssm_lambda_re, ssm_lambda_im, ssm_log_dt, ssm_b_re, ssm_b_im, ssm_c_re, ssm_c_im, ssm_d, glu_w, glu_b, q_norm_w, k_norm_w, lambda_q1, lambda_k1, lambda_q2, lambda_k2, subln_w, w_out, ple_w_proj, ple_w_gate):
    depth = norm_w.shape[0]
    for i in range(depth):
        lambda_init = 0.8 - 0.6 * math.exp(-0.3 * i)
        a_re, a_im, bb_re_t, bb_im_t = _ssm_params(
            ssm_lambda_re[i], ssm_lambda_im[i], ssm_log_dt[i], ssm_b_re[i], ssm_b_im[i])
        u, zs, qt, k, vt, zat = _in_proj(x, positions, norm_w[i], w_in[i], q_norm_w[i], k_norm_w[i])
        ys = _ssm_branch(u, zs, a_re, a_im, bb_re_t, bb_im_t, ssm_c_re[i], ssm_c_im[i],
                         ssm_d[i], glu_w[i], glu_b[i])
        ya = _attention(qt, k, vt, zat, lambda_q1[i], lambda_k1[i], lambda_q2[i], lambda_k2[i],
                        subln_w[i], lambda_init)
        x = _out_proj(x, ys, ya, p[i], w_out[i], ple_w_proj[i], ple_w_gate[i])
    return x
```

```python
import functools
import math

import jax
import jax.numpy as jnp
import numpy as np
from jax import lax
from jax.experimental import pallas as pl
from jax.experimental.pallas import tpu as pltpu

F32 = jnp.float32
BF16 = jnp.bfloat16

D_MODEL = 1024
PLE_DIM = 256
SSM_WIDTH = 512
SSM_GROUP = 16
SSM_GROUPS = 32
SSM_STATE = 64
ATTN_WIDTH = 512
N_HEADS = 4
HEAD_DIM = 64
V_HEAD_DIM = 128
ROT_DIM = 16
ROT_HALF = 8
ROPE_THETA = 500000.0
EPS = 1e-6
LOG2E = math.log2(math.e)

V7X_LANES = 128
V7X_SUBLANES = 8
V7X_VMEM_BYTES = 64 * 1024 * 1024
VMEM_LIMIT_BYTES = V7X_VMEM_BYTES * 7 // 8

TOKEN_TILE = 512
SSM_FOLD = 4
SSM_STEPS = 128
SSM_PART_STEPS = 64
SSM_PERM_STEPS = 32
GROUPS_PER_BLOCK = 8
N_SSM_BLOCKS = SSM_GROUPS // GROUPS_PER_BLOCK
BLOCK_STATE = GROUPS_PER_BLOCK * SSM_STATE
MASK_VALUE = -1e30


def _compiler_params(n_grid_axes):
    return pltpu.CompilerParams(
        dimension_semantics=("arbitrary",) * n_grid_axes,
        vmem_limit_bytes=VMEM_LIMIT_BYTES,
    )


def _ssm_params_kernel(lr_ref, li_ref, logdt_ref, br_ref, bi_ref, cr_ref, ci_ref,
                       ab_ref, ca_ref, cl_ref, bbc_ref, al_ref):
    fold = ab_ref.shape[0]
    n = lr_ref.shape[-1]
    lr = lr_ref[...]
    li = li_ref[...]
    dt = jnp.exp(logdt_ref[...])
    mag = jnp.exp(lr * dt)
    a_re = mag * jnp.cos(li * dt)
    a_im = mag * jnp.sin(li * dt)
    nr = a_re - 1.0
    ni = a_im
    den = lr * lr + li * li
    coef_re = (nr * lr + ni * li) / den
    coef_im = (ni * lr - nr * li) / den
    br = br_ref[...]
    bi = bi_ref[...]
    bb_re = coef_re * br - coef_im * bi
    bb_im = coef_re * bi + coef_im * br
    c_re = cr_ref[...]
    c_im = ci_ref[...]

    def cmul(xr, xi, yr, yi):
        return xr * yr - xi * yi, xr * yi + xi * yr

    powers = [(jnp.ones_like(a_re), jnp.zeros_like(a_im))]
    for _ in range(fold):
        powers.append(cmul(*powers[-1], a_re, a_im))
    for i in range(fold):
        ab_ref[i, 0], ab_ref[i, 1] = cmul(*powers[fold - 1 - i], bb_re, bb_im)
        ca_re, ca_im = cmul(c_re, c_im, *powers[i + 1])
        ca_ref[i, 0] = ca_re
        ca_ref[i, 1] = -ca_im
        cl_re, cl_im = cmul(c_re, c_im, *powers[i])
        cl_ref[i, :, :, :n] = cl_re
        cl_ref[i, :, :, n:] = -cl_im
    bbc_ref[:, :, :n] = bb_re
    bbc_ref[:, :, n:] = bb_im
    al_ref[0] = powers[fold][0]
    al_ref[1] = powers[fold][1]


def _ssm_params(lam_re, lam_im, log_dt, b_re, b_im, c_re, c_im):
    g, n, p, fold = SSM_GROUPS, SSM_STATE, SSM_GROUP, SSM_FOLD
    per_offset = jax.ShapeDtypeStruct((fold, 2, g, p, n), F32)
    return pl.pallas_call(
        _ssm_params_kernel,
        out_shape=(per_offset, per_offset,
                   jax.ShapeDtypeStruct((fold, g, p, 2 * n), F32),
                   jax.ShapeDtypeStruct((g, p, 2 * n), F32),
                   jax.ShapeDtypeStruct((2, g, 1, n), F32)),
        name="ssm_params",
    )(lam_re.reshape(g, 1, n), lam_im.reshape(g, 1, n), log_dt.reshape(g, 1, 1),
      jnp.swapaxes(b_re, 1, 2), jnp.swapaxes(b_im, 1, 2), c_re, c_im)


def _in_proj_kernel(x_ref, nw_ref, wnat_ref, wtr_ref, pos_ref, invf_ref, qnw_ref, knw_ref,
                    u_ref, zs_ref, qt_ref, k_ref, vt_ref, zat_ref):
    x = x_ref[...]
    ms = jnp.mean(x * x, axis=-1, keepdims=True)
    h = (x * lax.rsqrt(ms + EPS) * nw_ref[...]).astype(BF16)
    nat = jnp.dot(h, wnat_ref[...], preferred_element_type=F32)
    u_ref[...] = nat[:, :SSM_WIDTH].astype(BF16)
    zs_ref[...] = nat[:, SSM_WIDTH:].astype(BF16)
    tr = lax.dot_general(wtr_ref[...], h, (((1,), (1,)), ((), ())),
                         preferred_element_type=F32)
    tm = x.shape[0]
    ang = pos_ref[...].astype(F32) * invf_ref[...]
    cos = jnp.cos(ang)
    sin = jnp.sin(ang)

    def norm_rope(t, w_ref):
        t3 = t.reshape(2 * N_HEADS, HEAD_DIM, tm)
        ms3 = jnp.mean(t3 * t3, axis=1, keepdims=True)
        t3 = t3 * lax.rsqrt(ms3 + EPS) * w_ref[...]
        t1 = t3[:, :ROT_HALF, :]
        t2 = t3[:, ROT_HALF:ROT_DIM, :]
        r1 = t1 * cos - t2 * sin
        r2 = t2 * cos + t1 * sin
        t3 = jnp.concatenate([r1, r2, t3[:, ROT_DIM:, :]], axis=1)
        return t3.reshape(ATTN_WIDTH, tm)

    q = norm_rope(tr[0:ATTN_WIDTH], qnw_ref) * (HEAD_DIM ** -0.5 * LOG2E)
    qt_ref[...] = q.reshape(N_HEADS, 2 * HEAD_DIM, tm).astype(BF16)
    k = norm_rope(tr[ATTN_WIDTH:2 * ATTN_WIDTH], knw_ref)
    k_ref[...] = k.T.astype(BF16)
    vt_ref[...] = tr[2 * ATTN_WIDTH:3 * ATTN_WIDTH].reshape(N_HEADS, V_HEAD_DIM, tm).astype(BF16)
    zat_ref[...] = tr[3 * ATTN_WIDTH:].reshape(N_HEADS, V_HEAD_DIM, tm).astype(BF16)


def _in_proj(x, positions, norm_w, w_in, q_norm_w, k_norm_w):
    b, s, d = x.shape
    tm = TOKEN_TILE
    nt = s // tm
    w_nat = w_in[:, :2 * SSM_WIDTH].astype(BF16)
    w_tr = w_in[:, 2 * SSM_WIDTH:].T.astype(BF16)
    inv_freq = ROPE_THETA ** (-jnp.arange(0, ROT_DIM, 2, dtype=F32) / ROT_DIM)
    head_major = jax.ShapeDtypeStruct((b, N_HEADS, nt, V_HEAD_DIM, tm), BF16)
    head_spec = pl.BlockSpec((None, N_HEADS, None, V_HEAD_DIM, tm), lambda bi, ti: (bi, 0, ti, 0, 0))
    ssm_shape = jax.ShapeDtypeStruct((b, s, SSM_WIDTH), BF16)
    ssm_spec = pl.BlockSpec((None, tm, SSM_WIDTH), lambda bi, ti: (bi, ti, 0))
    const2 = lambda bi, ti: (0, 0)
    const3 = lambda bi, ti: (0, 0, 0)
    return pl.pallas_call(
        _in_proj_kernel,
        grid=(b, nt),
        in_specs=[
            pl.BlockSpec((None, tm, d), lambda bi, ti: (bi, ti, 0)),
            pl.BlockSpec((1, d), const2),
            pl.BlockSpec((d, 2 * SSM_WIDTH), const2),
            pl.BlockSpec((4 * ATTN_WIDTH, d), const2),
            pl.BlockSpec((None, 1, tm), lambda bi, ti: (bi, 0, ti)),
            pl.BlockSpec((ROT_HALF, 1), const2),
            pl.BlockSpec((1, HEAD_DIM, 1), const3),
            pl.BlockSpec((1, HEAD_DIM, 1), const3),
        ],
        out_specs=[
            ssm_spec, ssm_spec,
            head_spec,
            pl.BlockSpec((None, tm, ATTN_WIDTH), lambda bi, ti: (bi, ti, 0)),
            head_spec, head_spec,
        ],
        out_shape=(ssm_shape, ssm_shape, head_major,
                   jax.ShapeDtypeStruct((b, s, ATTN_WIDTH), BF16), head_major, head_major),
        compiler_params=_compiler_params(2),
        name="in_proj",
    )(x, norm_w.reshape(1, d), w_nat, w_tr, positions.reshape(b, 1, s),
      inv_freq.reshape(ROT_HALF, 1), q_norm_w.reshape(1, HEAD_DIM, 1), k_norm_w.reshape(1, HEAD_DIM, 1))


def _ssm_kernel(u_ref, zs_ref, pin_ref, pout_ref, ab_ref, ca_ref, cd_ref, ar_ref, ai_ref, d_ref,
                gw_ref, gb_ref, y_ref, uf_ref, bu_ref, xs_ref, y2_ref, yg_ref, state_ref,
                bw_ref, cs_ref, tmp_ref):
    n_batch, steps, width = u_ref.shape
    fold, perm_out_rows, perm_in_rows = pin_ref.shape
    perm_steps = perm_in_rows // n_batch
    part_steps = SSM_PART_STEPS
    perms_per_part = part_steps // perm_steps
    part_rows = perms_per_part * perm_out_rows
    n_parts = steps // part_steps
    blk = 2 * BLOCK_STATE
    fw = fold * V7X_LANES
    pack_rows = 2 * V7X_SUBLANES

    @pl.when(pl.program_id(0) == 0)
    def _():
        state_ref[...] = jnp.zeros_like(state_ref)
        bw_ref[...] = jnp.zeros_like(bw_ref)
        for m in range(N_SSM_BLOCKS):
            tmp_ref[...] = jnp.zeros_like(tmp_ref)
            for i in range(fold):
                for gl in range(GROUPS_PER_BLOCK):
                    g = m * GROUPS_PER_BLOCK + gl
                    in_rows = slice(i * V7X_LANES + gl * SSM_GROUP, i * V7X_LANES + (gl + 1) * SSM_GROUP)
                    for ri in range(2):
                        st_cols = slice(ri * BLOCK_STATE + gl * SSM_STATE, ri * BLOCK_STATE + (gl + 1) * SSM_STATE)
                        bw_ref[m, in_rows, st_cols] = ab_ref[i, ri, g].astype(BF16)
                        tmp_ref[in_rows, st_cols] = ca_ref[i, ri, g]
            cs_ref[m] = tmp_ref[...].T.astype(BF16)

    def rows_of(part):
        return slice(part * part_rows, (part + 1) * part_rows)

    def fold_in(part):
        for h in range(perms_per_part):
            t0 = part * part_steps + h * perm_steps
            u_bt = u_ref[:, t0:t0 + perm_steps, :].reshape(perm_in_rows, width)
            r0 = part * part_rows + h * perm_out_rows
            for i in range(fold):
                sel = jnp.dot(pin_ref[i], u_bt, preferred_element_type=F32).astype(BF16)
                for m in range(N_SSM_BLOCKS):
                    uf_ref[r0:r0 + perm_out_rows, m * fw + i * V7X_LANES:m * fw + (i + 1) * V7X_LANES] = (
                        sel[:, m * V7X_LANES:(m + 1) * V7X_LANES])

    def state_block(part, m):
        rows = rows_of(part)
        re = slice(m * blk, m * blk + BLOCK_STATE)
        im = slice(m * blk + BLOCK_STATE, (m + 1) * blk)
        bu_ref[rows, m * blk:(m + 1) * blk] = jnp.dot(
            uf_ref[rows, m * fw:(m + 1) * fw], bw_ref[m], preferred_element_type=F32)
        a_re = jnp.broadcast_to(ar_ref[:, m * BLOCK_STATE:(m + 1) * BLOCK_STATE], (V7X_SUBLANES, BLOCK_STATE))
        a_im = jnp.broadcast_to(ai_ref[:, m * BLOCK_STATE:(m + 1) * BLOCK_STATE], (V7X_SUBLANES, BLOCK_STATE))
        x_re = state_ref[:, re]
        x_im = state_ref[:, im]
        for r0 in range(part * part_rows, (part + 1) * part_rows, pack_rows):
            ins_re, ins_im = [], []
            for half in range(2):
                rr = slice(r0 + half * V7X_SUBLANES, r0 + (half + 1) * V7X_SUBLANES)
                ins_re.append(x_re)
                ins_im.append(x_im)
                n_re = a_re * x_re - a_im * x_im + bu_ref[rr, re]
                n_im = a_re * x_im + a_im * x_re + bu_ref[rr, im]
                x_re, x_im = n_re, n_im
            xs_ref[r0:r0 + pack_rows, re] = jnp.concatenate(ins_re, axis=0).astype(BF16)
            xs_ref[r0:r0 + pack_rows, im] = jnp.concatenate(ins_im, axis=0).astype(BF16)
        state_ref[:, re] = x_re
        state_ref[:, im] = x_im
        y2_ref[rows, m * fw:(m + 1) * fw] = (
            jnp.dot(xs_ref[rows, m * blk:(m + 1) * blk], cs_ref[m], preferred_element_type=F32)
            + jnp.dot(uf_ref[rows, m * fw:(m + 1) * fw], cd_ref[m], preferred_element_type=F32))

    def gate_offset(part, i):
        rows = rows_of(part)
        cols = [slice(m * fw + i * V7X_LANES, m * fw + (i + 1) * V7X_LANES) for m in range(N_SSM_BLOCKS)]
        y = jnp.concatenate([y2_ref[rows, c] for c in cols], axis=-1)
        u_i = jnp.concatenate([uf_ref[rows, c] for c in cols], axis=-1).astype(F32)
        y = jax.nn.gelu(y + d_ref[...] * u_i)
        gate = jnp.dot(y.astype(BF16), gw_ref[...], preferred_element_type=F32) + gb_ref[...]
        y = (y * jax.nn.sigmoid(gate)).astype(BF16)
        for h in range(perms_per_part):
            yg_ref[part * perms_per_part + h, i * perm_out_rows:(i + 1) * perm_out_rows, :] = (
                y[h * perm_out_rows:(h + 1) * perm_out_rows])

    def unfold_out(part):
        for h in range(perms_per_part):
            t0 = part * part_steps + h * perm_steps
            y_bt = jnp.dot(pout_ref[...], yg_ref[part * perms_per_part + h], preferred_element_type=F32)
            z = zs_ref[:, t0:t0 + perm_steps, :].reshape(perm_in_rows, width).astype(F32)
            y_ref[:, t0:t0 + perm_steps, :] = (
                (y_bt * jax.nn.silu(z)).astype(BF16).reshape(n_batch, perm_steps, width))

    assert fold == N_SSM_BLOCKS
    for part in range(n_parts + 1):
        if part < n_parts:
            fold_in(part)
        for j in range(N_SSM_BLOCKS):
            if part < n_parts:
                state_block(part, j)
            if part >= 1:
                gate_offset(part - 1, j)
        if part >= 1:
            unfold_out(part - 1)


def _ssm_direct_weights(cl, bbc):
    fold, gpb, nb, p = SSM_FOLD, GROUPS_PER_BLOCK, N_SSM_BLOCKS, SSM_GROUP
    in_lanes = fold * gpb * p
    lags = jnp.einsum('lgpn,gqn->lgqp', cl, bbc, precision=lax.Precision.HIGHEST)
    zero = jnp.zeros_like(lags[0])
    toep = jnp.stack([jnp.stack([lags[i - k] if i >= k else zero for i in range(fold)]) for k in range(fold)])
    toep = toep.reshape(fold, fold, nb, gpb, p, p).transpose(2, 0, 3, 4, 1, 5).reshape(nb * in_lanes, fold * p)
    col = np.arange(in_lanes)
    spread = (np.arange(fold * p)[:, None] == ((col // (gpb * p)) * p + col % p)[None, :]).astype(np.float32)
    in_group = (col // p) % gpb
    same_group = (in_group[:, None] == in_group[None, :]).astype(np.float32)
    cd = jnp.dot(toep, spread, precision=lax.Precision.HIGHEST).reshape(nb, in_lanes, in_lanes)
    return (cd * same_group).astype(BF16)


def _ssm_branch(u, zs, ab, ca, cl, bbc, al, d_skip, glu_w, glu_b):
    n_batch, s, _ = u.shape
    assert n_batch == V7X_SUBLANES, "one time group of all batches must fill one sublane tile"
    fold, steps, perm_steps = SSM_FOLD, SSM_STEPS, SSM_PERM_STEPS
    g_rows = (perm_steps // fold) * n_batch
    n_rows = perm_steps * n_batch
    r = np.arange(g_rows)
    src = (r % n_batch) * perm_steps + fold * (r // n_batch)
    pin = np.arange(n_rows)[None, None, :] == (src[None, :, None] + np.arange(fold)[:, None, None])
    pout = jnp.asarray(pin.reshape(fold * g_rows, n_rows).T, dtype=BF16)
    pin = jnp.asarray(pin, dtype=BF16)
    cd = _ssm_direct_weights(cl, bbc)
    row_spec = pl.BlockSpec((n_batch, steps, SSM_WIDTH), lambda i: (0, i, 0))
    const2 = lambda i: (0, 0)
    const3 = lambda i: (0, 0, 0)
    n_state = SSM_GROUPS * SSM_STATE
    f_rows = (steps // fold) * n_batch
    fw_all = fold * SSM_WIDTH
    return pl.pallas_call(
        _ssm_kernel,
        grid=(s // steps,),
        in_specs=[
            row_spec, row_spec,
            pl.BlockSpec(pin.shape, const3),
            pl.BlockSpec(pout.shape, const2),
            pl.BlockSpec(ab.shape, lambda i: (0, 0, 0, 0, 0)),
            pl.BlockSpec(ca.shape, lambda i: (0, 0, 0, 0, 0)),
            pl.BlockSpec(cd.shape, const3),
            pl.BlockSpec((1, n_state), const2),
            pl.BlockSpec((1, n_state), const2),
            pl.BlockSpec((1, SSM_WIDTH), const2),
            pl.BlockSpec((SSM_WIDTH, SSM_WIDTH), const2),
            pl.BlockSpec((1, SSM_WIDTH), const2),
        ],
        out_specs=row_spec,
        out_shape=jax.ShapeDtypeStruct((n_batch, s, SSM_WIDTH), BF16),
        scratch_shapes=[
            pltpu.VMEM((f_rows, fw_all), BF16),
            pltpu.VMEM((f_rows, 2 * n_state), F32),
            pltpu.VMEM((f_rows, 2 * n_state), BF16),
            pltpu.VMEM((f_rows, fw_all), F32),
            pltpu.VMEM((steps // perm_steps, fold * g_rows, SSM_WIDTH), BF16),
            pltpu.VMEM((V7X_SUBLANES, 2 * n_state), F32),
            pltpu.VMEM((N_SSM_BLOCKS, fold * V7X_LANES, 2 * BLOCK_STATE), BF16),
            pltpu.VMEM((N_SSM_BLOCKS, 2 * BLOCK_STATE, fold * V7X_LANES), BF16),
            pltpu.VMEM((fold * V7X_LANES, 2 * BLOCK_STATE), F32),
        ],
        compiler_params=_compiler_params(1),
        name="ssm_branch",
    )(u, zs, pin, pout, ab, ca, cd, al[0].reshape(1, n_state), al[1].reshape(1, n_state),
      d_skip.reshape(1, SSM_WIDTH), glu_w.astype(BF16), glu_b.reshape(1, SSM_WIDTH))


def _attn_kernel(lambda_init, qt_ref, k_ref, vt_ref, zat_ref, lq1_ref, lk1_ref, lq2_ref, lk2_ref,
                 subw_ref, y_ref, wq_ref, s_ref, mx_ref, acc_ref, m_ref, l_ref):
    n_tiles, _, tq = qt_ref.shape
    tk = tq
    diag_slot = 2

    def load_queries(q):
        wq_ref[0, :HEAD_DIM] = qt_ref[q, :HEAD_DIM]
        wq_ref[1, HEAD_DIM:] = qt_ref[q, HEAD_DIM:]

    def reset_stats():
        m_ref[...] = jnp.full_like(m_ref, MASK_VALUE)
        l_ref[...] = jnp.zeros_like(l_ref)
        acc_ref[...] = jnp.zeros_like(acc_ref)

    def scores(j, slot, masked):
        kb = k_ref[pl.ds(pl.multiple_of(j * tk, tk), tk), :]
        for c in range(2):
            s = jnp.dot(kb, wq_ref[c], preferred_element_type=F32)
            if masked:
                key_idx = lax.broadcasted_iota(jnp.int32, (tk, tq), 0)
                qry_idx = lax.broadcasted_iota(jnp.int32, (tk, tq), 1)
                s = jnp.where(key_idx <= qry_idx, s, MASK_VALUE)
            s_ref[slot, c] = s
            mx_ref[slot, c] = jnp.max(s, axis=0, keepdims=True)

    def accumulate(j, slot):
        vb = vt_ref[j]
        for c in range(2):
            m_old = m_ref[c]
            m_new = jnp.maximum(m_old, mx_ref[slot, c])
            alpha = jnp.exp2(m_old - m_new)
            p = jnp.exp2(s_ref[slot, c] - m_new)
            l_ref[c] = alpha * l_ref[c] + jnp.sum(p, axis=0, keepdims=True)
            acc_ref[c] = alpha * acc_ref[c] + jnp.dot(vb, p.astype(BF16), preferred_element_type=F32)
            m_ref[c] = m_new

    def finalize(q):
        lam = (jnp.exp(jnp.sum(lq1_ref[...] * lk1_ref[...], axis=-1, keepdims=True))
               - jnp.exp(jnp.sum(lq2_ref[...] * lk2_ref[...], axis=-1, keepdims=True)) + lambda_init)
        out = acc_ref[0] / l_ref[0] - lam * (acc_ref[1] / l_ref[1])
        ms = jnp.mean(out * out, axis=0, keepdims=True)
        out = out * lax.rsqrt(ms + EPS) * subw_ref[...] * (1.0 - lambda_init)
        out = out * jax.nn.silu(zat_ref[q].astype(F32))
        y_ref[pl.ds(pl.multiple_of(q * tq, tq), tq), :] = out.T.astype(BF16)

    zeros = jnp.zeros((HEAD_DIM, tq), BF16)
    wq_ref[0, HEAD_DIM:] = zeros
    wq_ref[1, :HEAD_DIM] = zeros
    load_queries(0)
    reset_stats()
    scores(0, 1, masked=True)

    def enter_tile(q, pending_slot):
        load_queries(q)
        scores(q, diag_slot, masked=True)
        accumulate(jnp.maximum(q - 2, 0), pending_slot)
        finalize(q - 1)
        reset_stats()
        scores(0, 0, masked=False)
        accumulate(q, diag_slot)

    def pair(i, c):
        j = 2 * i
        scores(j + 1, 1, masked=False)
        accumulate(j, 0)
        scores(j + 2, 0, masked=False)
        accumulate(j + 1, 1)
        return c

    def two_tiles(k, carry):
        q_odd = 2 * k + 1
        enter_tile(q_odd, 1)
        lax.fori_loop(0, k, pair, 0)
        q_even = q_odd + 1
        enter_tile(q_even, 0)
        lax.fori_loop(0, k, pair, 0)
        scores(q_even - 1, 1, masked=False)
        accumulate(q_even - 2, 0)
        return carry

    assert n_tiles % 2 == 0
    lax.fori_loop(0, (n_tiles - 2) // 2, two_tiles, 0)
    q_last = n_tiles - 1
    enter_tile(q_last, 1)
    lax.fori_loop(0, (q_last - 1) // 2, pair, 0)
    accumulate(q_last - 1, 0)
    finalize(q_last)


def _attention(qt, k, vt, zat, lq1, lk1, lq2, lk2, subln_w, lambda_init):
    b, _, nt, _, tq = qt.shape
    s = nt * tq
    assert nt >= 2
    row64 = pl.BlockSpec((1, HEAD_DIM), lambda bi, hi: (0, 0))
    head_spec = pl.BlockSpec((None, None, nt, V_HEAD_DIM, tq), lambda bi, hi: (bi, hi, 0, 0, 0))
    token_spec = pl.BlockSpec((None, s, V_HEAD_DIM), lambda bi, hi: (bi, 0, hi))
    return pl.pallas_call(
        functools.partial(_attn_kernel, lambda_init),
        grid=(b, N_HEADS),
        in_specs=[
            head_spec, token_spec, head_spec, head_spec,
            row64, row64, row64, row64,
            pl.BlockSpec((V_HEAD_DIM, 1), lambda bi, hi: (0, 0)),
        ],
        out_specs=token_spec,
        out_shape=jax.ShapeDtypeStruct((b, s, ATTN_WIDTH), BF16),
        scratch_shapes=[
            pltpu.VMEM((2, 2 * HEAD_DIM, tq), BF16),
            pltpu.VMEM((3, 2, tq, tq), F32),
            pltpu.VMEM((3, 2, 1, tq), F32),
            pltpu.VMEM((2, V_HEAD_DIM, tq), F32),
            pltpu.VMEM((2, 1, tq), F32),
            pltpu.VMEM((2, 1, tq), F32),
        ],
        compiler_params=_compiler_params(2),
        name="diff_attention",
    )(qt, k, vt, zat, lq1.reshape(1, HEAD_DIM), lk1.reshape(1, HEAD_DIM),
      lq2.reshape(1, HEAD_DIM), lk2.reshape(1, HEAD_DIM), subln_w.reshape(V_HEAD_DIM, 1))


def _out_kernel(x_ref, ys_ref, ya_ref, p_ref, wos_ref, woa_ref, wg_ref, wp_ref, o_ref):
    x2 = (x_ref[...]
          + jnp.dot(ys_ref[...], wos_ref[...], preferred_element_type=F32)
          + jnp.dot(ya_ref[...], woa_ref[...], preferred_element_type=F32))
    gate = jax.nn.sigmoid(jnp.dot(x2.astype(BF16), wg_ref[...], preferred_element_type=F32))
    ple = jnp.dot(p_ref[...].astype(BF16), wp_ref[...], preferred_element_type=F32)
    o_ref[...] = x2 + gate * ple


def _out_proj(x, ys, ya, p, w_out, w_proj, w_gate):
    b, s, d = x.shape
    tm = TOKEN_TILE
    const2 = lambda bi, ti: (0, 0)
    return pl.pallas_call(
        _out_kernel,
        grid=(b, s // tm),
        in_specs=[
            pl.BlockSpec((None, tm, d), lambda bi, ti: (bi, ti, 0)),
            pl.BlockSpec((None, tm, SSM_WIDTH), lambda bi, ti: (bi, ti, 0)),
            pl.BlockSpec((None, tm, ATTN_WIDTH), lambda bi, ti: (bi, ti, 0)),
            pl.BlockSpec((None, tm, PLE_DIM), lambda bi, ti: (bi, ti, 0)),
            pl.BlockSpec((SSM_WIDTH, d), const2),
            pl.BlockSpec((ATTN_WIDTH, d), const2),
            pl.BlockSpec((d, d), const2),
            pl.BlockSpec((PLE_DIM, d), const2),
        ],
        out_specs=pl.BlockSpec((None, tm, d), lambda bi, ti: (bi, ti, 0)),
        out_shape=jax.ShapeDtypeStruct((b, s, d), F32),
        compiler_params=_compiler_params(2),
        name="out_proj",
    )(x, ys, ya, p, w_out[:SSM_WIDTH].astype(BF16), w_out[SSM_WIDTH:].astype(BF16),
      w_gate.astype(BF16), w_proj.astype(BF16))


def kernel(x, p, positions, norm_w, w_in, ssm_lambda_re, ssm_lambda_im, ssm_log_dt, ssm_b_re, ssm_b_im, ssm_c_re, ssm_c_im, ssm_d, glu_w, glu_b, q_norm_w, k_norm_w, lambda_q1, lambda_k1, lambda_q2, lambda_k2, subln_w, w_out, ple_w_proj, ple_w_gate):
    depth = norm_w.shape[0]
    for i in range(depth):
        lambda_init = 0.8 - 0.6 * math.exp(-0.3 * i)
        ab, ca, cl, bbc, al = _ssm_params(ssm_lambda_re[i], ssm_lambda_im[i], ssm_log_dt[i],
                                          ssm_b_re[i], ssm_b_im[i], ssm_c_re[i], ssm_c_im[i])
        u, zs, qt, k, vt, zat = _in_proj(x, positions, norm_w[i], w_in[i], q_norm_w[i], k_norm_w[i])
        ys = _ssm_branch(u, zs, ab, ca, cl, bbc, al, ssm_d[i], glu_w[i], glu_b[i])
        ya = _attention(qt, k, vt, zat, lambda_q1[i], lambda_k1[i], lambda_q2[i], lambda_k2[i],
                        subln_w[i], lambda_init)
        x = _out_proj(x, ys, ya, p[i], w_out[i], ple_w_proj[i], ple_w_gate[i])
    return x
```

```python
import functools
import math

import jax
import jax.numpy as jnp
import numpy as np
from jax import lax
from jax.experimental import pallas as pl
from jax.experimental.pallas import tpu as pltpu

F32 = jnp.float32
BF16 = jnp.bfloat16

D_MODEL = 1024
PLE_DIM = 256
SSM_WIDTH = 512
SSM_GROUP = 16
SSM_GROUPS = 32
SSM_STATE = 64
ATTN_WIDTH = 512
N_HEADS = 4
HEAD_DIM = 64
V_HEAD_DIM = 128
ROT_DIM = 16
ROT_HALF = 8
ROPE_THETA = 500000.0
EPS = 1e-6
LOG2E = math.log2(math.e)

V7X_LANES = 128
V7X_SUBLANES = 8
V7X_VMEM_BYTES = 64 * 1024 * 1024
VMEM_LIMIT_BYTES = V7X_VMEM_BYTES * 7 // 8

TOKEN_TILE = 1024
ATTN_TILE = 512
OUT_TOKEN_TILE = 1024
SSM_FOLD = 4
SSM_STEPS = 128
SSM_PART_STEPS = 64
SSM_PERM_STEPS = 32
GROUPS_PER_BLOCK = 8
N_SSM_BLOCKS = SSM_GROUPS // GROUPS_PER_BLOCK
BLOCK_STATE = GROUPS_PER_BLOCK * SSM_STATE
MASK_VALUE = -1e30


def _compiler_params(n_grid_axes):
    return pltpu.CompilerParams(
        dimension_semantics=("arbitrary",) * n_grid_axes,
        vmem_limit_bytes=VMEM_LIMIT_BYTES,
    )


def _ssm_params_kernel(lr_ref, li_ref, logdt_ref, br_ref, bi_ref, cr_ref, ci_ref,
                       ab_ref, ca_ref, cl_ref, bbc_ref, al_ref):
    fold = ab_ref.shape[0]
    n = lr_ref.shape[-1]
    lr = lr_ref[...]
    li = li_ref[...]
    dt = jnp.exp(logdt_ref[...])
    mag = jnp.exp(lr * dt)
    a_re = mag * jnp.cos(li * dt)
    a_im = mag * jnp.sin(li * dt)
    nr = a_re - 1.0
    ni = a_im
    den = lr * lr + li * li
    coef_re = (nr * lr + ni * li) / den
    coef_im = (ni * lr - nr * li) / den
    br = br_ref[...]
    bi = bi_ref[...]
    bb_re = coef_re * br - coef_im * bi
    bb_im = coef_re * bi + coef_im * br
    c_re = cr_ref[...]
    c_im = ci_ref[...]

    def cmul(xr, xi, yr, yi):
        return xr * yr - xi * yi, xr * yi + xi * yr

    powers = [(jnp.ones_like(a_re), jnp.zeros_like(a_im))]
    for _ in range(fold):
        powers.append(cmul(*powers[-1], a_re, a_im))
    for i in range(fold):
        ab_ref[i, 0], ab_ref[i, 1] = cmul(*powers[fold - 1 - i], bb_re, bb_im)
        ca_re, ca_im = cmul(c_re, c_im, *powers[i + 1])
        ca_ref[i, 0] = ca_re
        ca_ref[i, 1] = -ca_im
        cl_re, cl_im = cmul(c_re, c_im, *powers[i])
        cl_ref[i, :, :, :n] = cl_re
        cl_ref[i, :, :, n:] = -cl_im
    bbc_ref[:, :, :n] = bb_re
    bbc_ref[:, :, n:] = bb_im
    al_ref[0] = powers[fold][0]
    al_ref[1] = powers[fold][1]


def _ssm_params(lam_re, lam_im, log_dt, b_re, b_im, c_re, c_im):
    g, n, p, fold = SSM_GROUPS, SSM_STATE, SSM_GROUP, SSM_FOLD
    per_offset = jax.ShapeDtypeStruct((fold, 2, g, p, n), F32)
    return pl.pallas_call(
        _ssm_params_kernel,
        out_shape=(per_offset, per_offset,
                   jax.ShapeDtypeStruct((fold, g, p, 2 * n), F32),
                   jax.ShapeDtypeStruct((g, p, 2 * n), F32),
                   jax.ShapeDtypeStruct((2, g, 1, n), F32)),
        name="ssm_params",
    )(lam_re.reshape(g, 1, n), lam_im.reshape(g, 1, n), log_dt.reshape(g, 1, 1),
      jnp.swapaxes(b_re, 1, 2), jnp.swapaxes(b_im, 1, 2), c_re, c_im)


def _in_proj_kernel(x_ref, nw_ref, wnat_ref, wtr_ref, pos_ref, invf_ref, qnw_ref, knw_ref,
                    u_ref, zs_ref, qt_ref, k_ref, vt_ref, zat_ref):
    x = x_ref[...]
    ms = jnp.mean(x * x, axis=-1, keepdims=True)
    h = (x * lax.rsqrt(ms + EPS) * nw_ref[...]).astype(BF16)
    nat = jnp.dot(h, wnat_ref[...], preferred_element_type=F32)
    u_ref[...] = nat[:, :SSM_WIDTH].astype(BF16)
    zs_ref[...] = nat[:, SSM_WIDTH:].astype(BF16)
    tr = lax.dot_general(wtr_ref[...], h, (((1,), (1,)), ((), ())),
                         preferred_element_type=F32)
    tm = x.shape[0]
    ang = pos_ref[...].astype(F32) * invf_ref[...]
    cos = jnp.cos(ang)
    sin = jnp.sin(ang)

    def norm_rope(t, w_ref):
        t3 = t.reshape(2 * N_HEADS, HEAD_DIM, tm)
        ms3 = jnp.mean(t3 * t3, axis=1, keepdims=True)
        t3 = t3 * lax.rsqrt(ms3 + EPS) * w_ref[...]
        t1 = t3[:, :ROT_HALF, :]
        t2 = t3[:, ROT_HALF:ROT_DIM, :]
        r1 = t1 * cos - t2 * sin
        r2 = t2 * cos + t1 * sin
        t3 = jnp.concatenate([r1, r2, t3[:, ROT_DIM:, :]], axis=1)
        return t3.reshape(ATTN_WIDTH, tm)

    def store_head_major(ref, t):
        t = t.reshape(N_HEADS, V_HEAD_DIM, tm).astype(BF16)
        ta = ref.shape[-1]
        for j in range(tm // ta):
            ref[:, j] = t[:, :, j * ta:(j + 1) * ta]

    q = norm_rope(tr[0:ATTN_WIDTH], qnw_ref) * (HEAD_DIM ** -0.5 * LOG2E)
    store_head_major(qt_ref, q)
    k = norm_rope(tr[ATTN_WIDTH:2 * ATTN_WIDTH], knw_ref)
    k_ref[...] = k.T.astype(BF16)
    store_head_major(vt_ref, tr[2 * ATTN_WIDTH:3 * ATTN_WIDTH])
    store_head_major(zat_ref, tr[3 * ATTN_WIDTH:])


def _in_proj(x, positions, norm_w, w_in, q_norm_w, k_norm_w):
    b, s, d = x.shape
    tm, ta = TOKEN_TILE, ATTN_TILE
    nt, tiles_per_step = s // ta, tm // ta
    w_nat = w_in[:, :2 * SSM_WIDTH].astype(BF16)
    w_tr = w_in[:, 2 * SSM_WIDTH:].T.astype(BF16)
    inv_freq = ROPE_THETA ** (-jnp.arange(0, ROT_DIM, 2, dtype=F32) / ROT_DIM)
    head_major = jax.ShapeDtypeStruct((b, N_HEADS, nt, V_HEAD_DIM, ta), BF16)
    head_spec = pl.BlockSpec((None, N_HEADS, tiles_per_step, V_HEAD_DIM, ta), lambda bi, ti: (bi, 0, ti, 0, 0))
    ssm_shape = jax.ShapeDtypeStruct((b, s, SSM_WIDTH), BF16)
    ssm_spec = pl.BlockSpec((None, tm, SSM_WIDTH), lambda bi, ti: (bi, ti, 0))
    const2 = lambda bi, ti: (0, 0)
    const3 = lambda bi, ti: (0, 0, 0)
    return pl.pallas_call(
        _in_proj_kernel,
        grid=(b, s // tm),
        in_specs=[
            pl.BlockSpec((None, tm, d), lambda bi, ti: (bi, ti, 0)),
            pl.BlockSpec((1, d), const2),
            pl.BlockSpec((d, 2 * SSM_WIDTH), const2),
            pl.BlockSpec((4 * ATTN_WIDTH, d), const2),
            pl.BlockSpec((None, 1, tm), lambda bi, ti: (bi, 0, ti)),
            pl.BlockSpec((ROT_HALF, 1), const2),
            pl.BlockSpec((1, HEAD_DIM, 1), const3),
            pl.BlockSpec((1, HEAD_DIM, 1), const3),
        ],
        out_specs=[
            ssm_spec, ssm_spec,
            head_spec,
            pl.BlockSpec((None, tm, ATTN_WIDTH), lambda bi, ti: (bi, ti, 0)),
            head_spec, head_spec,
        ],
        out_shape=(ssm_shape, ssm_shape, head_major,
                   jax.ShapeDtypeStruct((b, s, ATTN_WIDTH), BF16), head_major, head_major),
        compiler_params=_compiler_params(2),
        name="in_proj",
    )(x, norm_w.reshape(1, d), w_nat, w_tr, positions.reshape(b, 1, s),
      inv_freq.reshape(ROT_HALF, 1), q_norm_w.reshape(1, HEAD_DIM, 1), k_norm_w.reshape(1, HEAD_DIM, 1))


def _ssm_kernel(u_ref, zs_ref, pin_ref, pout_ref, ab_ref, ca_ref, cd_ref, ar_ref, ai_ref, d_ref,
                gw_ref, gb_ref, y_ref, uf_ref, bu_ref, xs_ref, y2_ref, yg_ref, state_ref,
                bw_ref, cs_ref, tmp_ref):
    n_batch, steps, width = u_ref.shape
    fold, perm_out_rows, perm_in_rows = pin_ref.shape
    perm_steps = perm_in_rows // n_batch
    part_steps = SSM_PART_STEPS
    perms_per_part = part_steps // perm_steps
    part_rows = perms_per_part * perm_out_rows
    n_parts = steps // part_steps
    blk = 2 * BLOCK_STATE
    fw = fold * V7X_LANES
    pack_rows = 2 * V7X_SUBLANES

    @pl.when(pl.program_id(0) == 0)
    def _():
        state_ref[...] = jnp.zeros_like(state_ref)
        bw_ref[...] = jnp.zeros_like(bw_ref)
        for m in range(N_SSM_BLOCKS):
            tmp_ref[...] = jnp.zeros_like(tmp_ref)
            for i in range(fold):
                for gl in range(GROUPS_PER_BLOCK):
                    g = m * GROUPS_PER_BLOCK + gl
                    in_rows = slice(i * V7X_LANES + gl * SSM_GROUP, i * V7X_LANES + (gl + 1) * SSM_GROUP)
                    for ri in range(2):
                        st_cols = slice(ri * BLOCK_STATE + gl * SSM_STATE, ri * BLOCK_STATE + (gl + 1) * SSM_STATE)
                        bw_ref[m, in_rows, st_cols] = ab_ref[i, ri, g].astype(BF16)
                        tmp_ref[in_rows, st_cols] = ca_ref[i, ri, g]
            cs_ref[m] = tmp_ref[...].T.astype(BF16)

    def rows_of(part):
        return slice(part * part_rows, (part + 1) * part_rows)

    def fold_in(part):
        for h in range(perms_per_part):
            t0 = part * part_steps + h * perm_steps
            u_bt = u_ref[:, t0:t0 + perm_steps, :].reshape(perm_in_rows, width)
            r0 = part * part_rows + h * perm_out_rows
            for i in range(fold):
                sel = jnp.dot(pin_ref[i], u_bt, preferred_element_type=F32).astype(BF16)
                for m in range(N_SSM_BLOCKS):
                    uf_ref[r0:r0 + perm_out_rows, m * fw + i * V7X_LANES:m * fw + (i + 1) * V7X_LANES] = (
                        sel[:, m * V7X_LANES:(m + 1) * V7X_LANES])

    def state_block(part, m):
        rows = rows_of(part)
        re = slice(m * blk, m * blk + BLOCK_STATE)
        im = slice(m * blk + BLOCK_STATE, (m + 1) * blk)
        bu_ref[rows, m * blk:(m + 1) * blk] = jnp.dot(
            uf_ref[rows, m * fw:(m + 1) * fw], bw_ref[m], preferred_element_type=F32)
        a_re = jnp.broadcast_to(ar_ref[:, m * BLOCK_STATE:(m + 1) * BLOCK_STATE], (V7X_SUBLANES, BLOCK_STATE))
        a_im = jnp.broadcast_to(ai_ref[:, m * BLOCK_STATE:(m + 1) * BLOCK_STATE], (V7X_SUBLANES, BLOCK_STATE))
        x_re = state_ref[:, re]
        x_im = state_ref[:, im]
        for r0 in range(part * part_rows, (part + 1) * part_rows, pack_rows):
            ins_re, ins_im = [], []
            for half in range(2):
                rr = slice(r0 + half * V7X_SUBLANES, r0 + (half + 1) * V7X_SUBLANES)
                ins_re.append(x_re)
                ins_im.append(x_im)
                n_re = a_re * x_re - a_im * x_im + bu_ref[rr, re]
                n_im = a_re * x_im + a_im * x_re + bu_ref[rr, im]
                x_re, x_im = n_re, n_im
            xs_ref[r0:r0 + pack_rows, re] = jnp.concatenate(ins_re, axis=0).astype(BF16)
            xs_ref[r0:r0 + pack_rows, im] = jnp.concatenate(ins_im, axis=0).astype(BF16)
        state_ref[:, re] = x_re
        state_ref[:, im] = x_im
        y2_ref[rows, m * fw:(m + 1) * fw] = (
            jnp.dot(xs_ref[rows, m * blk:(m + 1) * blk], cs_ref[m], preferred_element_type=F32)
            + jnp.dot(uf_ref[rows, m * fw:(m + 1) * fw], cd_ref[m], preferred_element_type=F32))

    def gate_offset(part, i):
        rows = rows_of(part)
        cols = [slice(m * fw + i * V7X_LANES, m * fw + (i + 1) * V7X_LANES) for m in range(N_SSM_BLOCKS)]
        y = jnp.concatenate([y2_ref[rows, c] for c in cols], axis=-1)
        u_i = jnp.concatenate([uf_ref[rows, c] for c in cols], axis=-1).astype(F32)
        y = jax.nn.gelu(y + d_ref[...] * u_i)
        gate = jnp.dot(y.astype(BF16), gw_ref[...], preferred_element_type=F32) + gb_ref[...]
        y = (y * jax.nn.sigmoid(gate)).astype(BF16)
        for h in range(perms_per_part):
            yg_ref[part * perms_per_part + h, i * perm_out_rows:(i + 1) * perm_out_rows, :] = (
                y[h * perm_out_rows:(h + 1) * perm_out_rows])

    def unfold_out(part):
        for h in range(perms_per_part):
            t0 = part * part_steps + h * perm_steps
            y_bt = jnp.dot(pout_ref[...], yg_ref[part * perms_per_part + h], preferred_element_type=F32)
            z = zs_ref[:, t0:t0 + perm_steps, :].reshape(perm_in_rows, width).astype(F32)
            y_ref[:, t0:t0 + perm_steps, :] = (
                (y_bt * jax.nn.silu(z)).astype(BF16).reshape(n_batch, perm_steps, width))

    assert fold == N_SSM_BLOCKS
    for part in range(n_parts + 1):
        if part < n_parts:
            fold_in(part)
        for j in range(N_SSM_BLOCKS):
            if part < n_parts:
                state_block(part, j)
            if part >= 1:
                gate_offset(part - 1, j)
        if part >= 1:
            unfold_out(part - 1)


def _ssm_direct_weights(cl, bbc):
    fold, gpb, nb, p = SSM_FOLD, GROUPS_PER_BLOCK, N_SSM_BLOCKS, SSM_GROUP
    in_lanes = fold * gpb * p
    lags = jnp.einsum('lgpn,gqn->lgqp', cl, bbc, precision=lax.Precision.HIGHEST)
    zero = jnp.zeros_like(lags[0])
    toep = jnp.stack([jnp.stack([lags[i - k] if i >= k else zero for i in range(fold)]) for k in range(fold)])
    toep = toep.reshape(fold, fold, nb, gpb, p, p).transpose(2, 0, 3, 4, 1, 5).reshape(nb * in_lanes, fold * p)
    col = np.arange(in_lanes)
    spread = (np.arange(fold * p)[:, None] == ((col // (gpb * p)) * p + col % p)[None, :]).astype(np.float32)
    in_group = (col // p) % gpb
    same_group = (in_group[:, None] == in_group[None, :]).astype(np.float32)
    cd = jnp.dot(toep, spread, precision=lax.Precision.HIGHEST).reshape(nb, in_lanes, in_lanes)
    return (cd * same_group).astype(BF16)


def _ssm_branch(u, zs, ab, ca, cl, bbc, al, d_skip, glu_w, glu_b):
    n_batch, s, _ = u.shape
    assert n_batch == V7X_SUBLANES, "one time group of all batches must fill one sublane tile"
    fold, steps, perm_steps = SSM_FOLD, SSM_STEPS, SSM_PERM_STEPS
    g_rows = (perm_steps // fold) * n_batch
    n_rows = perm_steps * n_batch
    r = np.arange(g_rows)
    src = (r % n_batch) * perm_steps + fold * (r // n_batch)
    pin = np.arange(n_rows)[None, None, :] == (src[None, :, None] + np.arange(fold)[:, None, None])
    pout = jnp.asarray(pin.reshape(fold * g_rows, n_rows).T, dtype=BF16)
    pin = jnp.asarray(pin, dtype=BF16)
    cd = _ssm_direct_weights(cl, bbc)
    row_spec = pl.BlockSpec((n_batch, steps, SSM_WIDTH), lambda i: (0, i, 0))
    const2 = lambda i: (0, 0)
    const3 = lambda i: (0, 0, 0)
    n_state = SSM_GROUPS * SSM_STATE
    f_rows = (steps // fold) * n_batch
    fw_all = fold * SSM_WIDTH
    return pl.pallas_call(
        _ssm_kernel,
        grid=(s // steps,),
        in_specs=[
            row_spec, row_spec,
            pl.BlockSpec(pin.shape, const3),
            pl.BlockSpec(pout.shape, const2),
            pl.BlockSpec(ab.shape, lambda i: (0, 0, 0, 0, 0)),
            pl.BlockSpec(ca.shape, lambda i: (0, 0, 0, 0, 0)),
            pl.BlockSpec(cd.shape, const3),
            pl.BlockSpec((1, n_state), const2),
            pl.BlockSpec((1, n_state), const2),
            pl.BlockSpec((1, SSM_WIDTH), const2),
            pl.BlockSpec((SSM_WIDTH, SSM_WIDTH), const2),
            pl.BlockSpec((1, SSM_WIDTH), const2),
        ],
        out_specs=row_spec,
        out_shape=jax.ShapeDtypeStruct((n_batch, s, SSM_WIDTH), BF16),
        scratch_shapes=[
            pltpu.VMEM((f_rows, fw_all), BF16),
            pltpu.VMEM((f_rows, 2 * n_state), F32),
            pltpu.VMEM((f_rows, 2 * n_state), BF16),
            pltpu.VMEM((f_rows, fw_all), F32),
            pltpu.VMEM((steps // perm_steps, fold * g_rows, SSM_WIDTH), BF16),
            pltpu.VMEM((V7X_SUBLANES, 2 * n_state), F32),
            pltpu.VMEM((N_SSM_BLOCKS, fold * V7X_LANES, 2 * BLOCK_STATE), BF16),
            pltpu.VMEM((N_SSM_BLOCKS, 2 * BLOCK_STATE, fold * V7X_LANES), BF16),
            pltpu.VMEM((fold * V7X_LANES, 2 * BLOCK_STATE), F32),
        ],
        compiler_params=_compiler_params(1),
        name="ssm_branch",
    )(u, zs, pin, pout, ab, ca, cd, al[0].reshape(1, n_state), al[1].reshape(1, n_state),
      d_skip.reshape(1, SSM_WIDTH), glu_w.astype(BF16), glu_b.reshape(1, SSM_WIDTH))


def _attn_kernel(lambda_init, qt_ref, k_ref, vt_ref, zat_ref, lq1_ref, lk1_ref, lq2_ref, lk2_ref,
                 subw_ref, y_ref, wq_ref, s_ref, mx_ref, acc_ref, m_ref, l_ref):
    n_tiles, _, tq = qt_ref.shape
    tk = tq
    diag_slot = 2

    def load_queries(q):
        wq_ref[0, :HEAD_DIM] = qt_ref[q, :HEAD_DIM]
        wq_ref[1, HEAD_DIM:] = qt_ref[q, HEAD_DIM:]

    def reset_stats():
        m_ref[...] = jnp.full_like(m_ref, MASK_VALUE)
        l_ref[...] = jnp.zeros_like(l_ref)
        acc_ref[...] = jnp.zeros_like(acc_ref)

    def scores(j, slot, masked):
        kb = k_ref[pl.ds(pl.multiple_of(j * tk, tk), tk), :]
        for c in range(2):
            s = jnp.dot(kb, wq_ref[c], preferred_element_type=F32)
            if masked:
                key_idx = lax.broadcasted_iota(jnp.int32, (tk, tq), 0)
                qry_idx = lax.broadcasted_iota(jnp.int32, (tk, tq), 1)
                s = jnp.where(key_idx <= qry_idx, s, MASK_VALUE)
            s_ref[slot, c] = s
            mx_ref[slot, c] = jnp.max(s, axis=0, keepdims=True)

    def accumulate(j, slot):
        vb = vt_ref[j]
        for c in range(2):
            m_old = m_ref[c]
            m_new = jnp.maximum(m_old, mx_ref[slot, c])
            alpha = jnp.exp2(m_old - m_new)
            p = jnp.exp2(s_ref[slot, c] - m_new)
            l_ref[c] = alpha * l_ref[c] + jnp.sum(p, axis=0, keepdims=True)
            acc_ref[c] = alpha * acc_ref[c] + jnp.dot(vb, p.astype(BF16), preferred_element_type=F32)
            m_ref[c] = m_new

    def finalize(q):
        lam = (jnp.exp(jnp.sum(lq1_ref[...] * lk1_ref[...], axis=-1, keepdims=True))
               - jnp.exp(jnp.sum(lq2_ref[...] * lk2_ref[...], axis=-1, keepdims=True)) + lambda_init)
        out = acc_ref[0] * (1.0 / l_ref[0]) - acc_ref[1] * (lam / l_ref[1])
        ms = jnp.mean(out * out, axis=0, keepdims=True)
        out = out * (lax.rsqrt(ms + EPS) * (1.0 - lambda_init)) * subw_ref[...]
        out = out * jax.nn.silu(zat_ref[q].astype(F32))
        y_ref[pl.ds(pl.multiple_of(q * tq, tq), tq), :] = out.T.astype(BF16)

    zeros = jnp.zeros((HEAD_DIM, tq), BF16)
    wq_ref[0, HEAD_DIM:] = zeros
    wq_ref[1, :HEAD_DIM] = zeros
    load_queries(0)
    reset_stats()
    scores(0, 1, masked=True)

    def enter_tile(q, pending_slot):
        load_queries(q)
        scores(q, diag_slot, masked=True)
        accumulate(jnp.maximum(q - 2, 0), pending_slot)
        finalize(q - 1)
        reset_stats()
        scores(0, 0, masked=False)
        accumulate(q, diag_slot)

    def pair(i, c):
        j = 2 * i
        scores(j + 1, 1, masked=False)
        accumulate(j, 0)
        scores(j + 2, 0, masked=False)
        accumulate(j + 1, 1)
        return c

    def two_tiles(k, carry):
        q_odd = 2 * k + 1
        enter_tile(q_odd, 1)
        lax.fori_loop(0, k, pair, 0)
        q_even = q_odd + 1
        enter_tile(q_even, 0)
        lax.fori_loop(0, k, pair, 0)
        scores(q_even - 1, 1, masked=False)
        accumulate(q_even - 2, 0)
        return carry

    assert n_tiles % 2 == 0
    lax.fori_loop(0, (n_tiles - 2) // 2, two_tiles, 0)
    q_last = n_tiles - 1
    enter_tile(q_last, 1)
    lax.fori_loop(0, (q_last - 1) // 2, pair, 0)
    accumulate(q_last - 1, 0)
    finalize(q_last)


def _attention(qt, k, vt, zat, lq1, lk1, lq2, lk2, subln_w, lambda_init):
    b, _, nt, _, tq = qt.shape
    s = nt * tq
    assert nt >= 2
    row64 = pl.BlockSpec((1, HEAD_DIM), lambda bi, hi: (0, 0))
    head_spec = pl.BlockSpec((None, None, nt, V_HEAD_DIM, tq), lambda bi, hi: (bi, hi, 0, 0, 0))
    token_spec = pl.BlockSpec((None, s, V_HEAD_DIM), lambda bi, hi: (bi, 0, hi))
    return pl.pallas_call(
        functools.partial(_attn_kernel, lambda_init),
        grid=(b, N_HEADS),
        in_specs=[
            head_spec, token_spec, head_spec, head_spec,
            row64, row64, row64, row64,
            pl.BlockSpec((V_HEAD_DIM, 1), lambda bi, hi: (0, 0)),
        ],
        out_specs=token_spec,
        out_shape=jax.ShapeDtypeStruct((b, s, ATTN_WIDTH), BF16),
        scratch_shapes=[
            pltpu.VMEM((2, 2 * HEAD_DIM, tq), BF16),
            pltpu.VMEM((3, 2, tq, tq), F32),
            pltpu.VMEM((3, 2, 1, tq), F32),
            pltpu.VMEM((2, V_HEAD_DIM, tq), F32),
            pltpu.VMEM((2, 1, tq), F32),
            pltpu.VMEM((2, 1, tq), F32),
        ],
        compiler_params=_compiler_params(2),
        name="diff_attention",
    )(qt, k, vt, zat, lq1.reshape(1, HEAD_DIM), lk1.reshape(1, HEAD_DIM),
      lq2.reshape(1, HEAD_DIM), lk2.reshape(1, HEAD_DIM), subln_w.reshape(V_HEAD_DIM, 1))


def _out_kernel(x_ref, ys_ref, ya_ref, p_ref, wos_ref, woa_ref, wg_ref, wp_ref, o_ref):
    x2 = (x_ref[...]
          + jnp.dot(ys_ref[...], wos_ref[...], preferred_element_type=F32)
          + jnp.dot(ya_ref[...], woa_ref[...], preferred_element_type=F32))
    gate = jax.nn.sigmoid(jnp.dot(x2.astype(BF16), wg_ref[...], preferred_element_type=F32))
    ple = jnp.dot(p_ref[...].astype(BF16), wp_ref[...], preferred_element_type=F32)
    o_ref[...] = x2 + gate * ple


def _out_proj(x, ys, ya, p, w_out, w_proj, w_gate):
    b, s, d = x.shape
    tm = OUT_TOKEN_TILE
    const2 = lambda bi, ti: (0, 0)
    return pl.pallas_call(
        _out_kernel,
        grid=(b, s // tm),
        in_specs=[
            pl.BlockSpec((None, tm, d), lambda bi, ti: (bi, ti, 0)),
            pl.BlockSpec((None, tm, SSM_WIDTH), lambda bi, ti: (bi, ti, 0)),
            pl.BlockSpec((None, tm, ATTN_WIDTH), lambda bi, ti: (bi, ti, 0)),
            pl.BlockSpec((None, tm, PLE_DIM), lambda bi, ti: (bi, ti, 0)),
            pl.BlockSpec((SSM_WIDTH, d), const2),
            pl.BlockSpec((ATTN_WIDTH, d), const2),
            pl.BlockSpec((d, d), const2),
            pl.BlockSpec((PLE_DIM, d), const2),
        ],
        out_specs=pl.BlockSpec((None, tm, d), lambda bi, ti: (bi, ti, 0)),
        out_shape=jax.ShapeDtypeStruct((b, s, d), F32),
        compiler_params=_compiler_params(2),
        name="out_proj",
    )(x, ys, ya, p, w_out[:SSM_WIDTH].astype(BF16), w_out[SSM_WIDTH:].astype(BF16),
      w_gate.astype(BF16), w_proj.astype(BF16))


def kernel(x, p, positions, norm_w, w_in, ssm_lambda_re, ssm_lambda_im, ssm_log_dt, ssm_b_re, ssm_b_im, ssm_c_re, ssm_c_im, ssm_d, glu_w, glu_b, q_norm_w, k_norm_w, lambda_q1, lambda_k1, lambda_q2, lambda_k2, subln_w, w_out, ple_w_proj, ple_w_gate):
    depth = norm_w.shape[0]
    for i in range(depth):
        lambda_init = 0.8 - 0.6 * math.exp(-0.3 * i)
        ab, ca, cl, bbc, al = _ssm_params(ssm_lambda_re[i], ssm_lambda_im[i], ssm_log_dt[i],
                                          ssm_b_re[i], ssm_b_im[i], ssm_c_re[i], ssm_c_im[i])
        u, zs, qt, k, vt, zat = _in_proj(x, positions, norm_w[i], w_in[i], q_norm_w[i], k_norm_w[i])
        ys = _ssm_branch(u, zs, ab, ca, cl, bbc, al, ssm_d[i], glu_w[i], glu_b[i])
        ya = _attention(qt, k, vt, zat, lambda_q1[i], lambda_k1[i], lambda_q2[i], lambda_k2[i],
                        subln_w[i], lambda_init)
        x = _out_proj(x, ys, ya, p[i], w_out[i], ple_w_proj[i], ple_w_gate[i])
    return x
```

```python
import functools
import math

import jax
import jax.numpy as jnp
import numpy as np
from jax import lax
from jax.experimental import pallas as pl
from jax.experimental.pallas import tpu as pltpu

F32 = jnp.float32
BF16 = jnp.bfloat16

D_MODEL = 1024
PLE_DIM = 256
SSM_WIDTH = 512
SSM_GROUP = 16
SSM_GROUPS = 32
SSM_STATE = 64
ATTN_WIDTH = 512
N_HEADS = 4
HEAD_DIM = 64
V_HEAD_DIM = 128
ROT_DIM = 16
ROT_HALF = 8
ROPE_THETA = 500000.0
EPS = 1e-6
LOG2E = math.log2(math.e)

V7X_LANES = 128
V7X_SUBLANES = 8
V7X_VMEM_BYTES = 64 * 1024 * 1024
VMEM_LIMIT_BYTES = V7X_VMEM_BYTES * 7 // 8

TOKEN_TILE = 1024
ATTN_TILE = 512
OUT_TOKEN_TILE = 1024
SSM_FOLD = 4
SSM_STEPS = 128
SSM_PART_STEPS = 64
SSM_PERM_STEPS = 32
GROUPS_PER_BLOCK = 8
N_SSM_BLOCKS = SSM_GROUPS // GROUPS_PER_BLOCK
BLOCK_STATE = GROUPS_PER_BLOCK * SSM_STATE
MASK_VALUE = -1e30


def _gelu_tanh(x):
    c0 = math.sqrt(2.0 / math.pi)
    inner = x * (c0 + (c0 * 0.044715) * (x * x))
    half = 0.5 * x
    return half + half * jnp.tanh(inner)


def _compiler_params(n_grid_axes):
    return pltpu.CompilerParams(
        dimension_semantics=("arbitrary",) * n_grid_axes,
        vmem_limit_bytes=VMEM_LIMIT_BYTES,
    )


def _ssm_params_kernel(lr_ref, li_ref, logdt_ref, br_ref, bi_ref, cr_ref, ci_ref,
                       ab_ref, ca_ref, cl_ref, bbc_ref, al_ref):
    fold = ab_ref.shape[0]
    n = lr_ref.shape[-1]
    lr = lr_ref[...]
    li = li_ref[...]
    dt = jnp.exp(logdt_ref[...])
    mag = jnp.exp(lr * dt)
    a_re = mag * jnp.cos(li * dt)
    a_im = mag * jnp.sin(li * dt)
    nr = a_re - 1.0
    ni = a_im
    den = lr * lr + li * li
    coef_re = (nr * lr + ni * li) / den
    coef_im = (ni * lr - nr * li) / den
    br = br_ref[...]
    bi = bi_ref[...]
    bb_re = coef_re * br - coef_im * bi
    bb_im = coef_re * bi + coef_im * br
    c_re = cr_ref[...]
    c_im = ci_ref[...]

    def cmul(xr, xi, yr, yi):
        return xr * yr - xi * yi, xr * yi + xi * yr

    powers = [(jnp.ones_like(a_re), jnp.zeros_like(a_im))]
    for _ in range(fold):
        powers.append(cmul(*powers[-1], a_re, a_im))
    for i in range(fold):
        ab_ref[i, 0], ab_ref[i, 1] = cmul(*powers[fold - 1 - i], bb_re, bb_im)
        ca_re, ca_im = cmul(c_re, c_im, *powers[i + 1])
        ca_ref[i, 0] = ca_re
        ca_ref[i, 1] = -ca_im
        cl_re, cl_im = cmul(c_re, c_im, *powers[i])
        cl_ref[i, :, :, :n] = cl_re
        cl_ref[i, :, :, n:] = -cl_im
    bbc_ref[:, :, :n] = bb_re
    bbc_ref[:, :, n:] = bb_im
    al_ref[0] = powers[fold][0]
    al_ref[1] = powers[fold][1]


def _ssm_params(lam_re, lam_im, log_dt, b_re, b_im, c_re, c_im):
    g, n, p, fold = SSM_GROUPS, SSM_STATE, SSM_GROUP, SSM_FOLD
    per_offset = jax.ShapeDtypeStruct((fold, 2, g, p, n), F32)
    return pl.pallas_call(
        _ssm_params_kernel,
        out_shape=(per_offset, per_offset,
                   jax.ShapeDtypeStruct((fold, g, p, 2 * n), F32),
                   jax.ShapeDtypeStruct((g, p, 2 * n), F32),
                   jax.ShapeDtypeStruct((2, g, 1, n), F32)),
        name="ssm_params",
    )(lam_re.reshape(g, 1, n), lam_im.reshape(g, 1, n), log_dt.reshape(g, 1, 1),
      jnp.swapaxes(b_re, 1, 2), jnp.swapaxes(b_im, 1, 2), c_re, c_im)


def _in_proj_kernel(x_ref, nw_ref, wnat_ref, wtr_ref, pos_ref, invf_ref, qnw_ref, knw_ref,
                    u_ref, zs_ref, qt_ref, k_ref, vt_ref, zat_ref):
    x = x_ref[...]
    ms = jnp.mean(x * x, axis=-1, keepdims=True)
    h = (x * lax.rsqrt(ms + EPS) * nw_ref[...]).astype(BF16)
    nat = jnp.dot(h, wnat_ref[...], preferred_element_type=F32)
    u_ref[...] = nat[:, :SSM_WIDTH].astype(BF16)
    zs_ref[...] = nat[:, SSM_WIDTH:].astype(BF16)
    tr = lax.dot_general(wtr_ref[...], h, (((1,), (1,)), ((), ())),
                         preferred_element_type=F32)
    tm = x.shape[0]
    ang = pos_ref[...].astype(F32) * invf_ref[...]
    cos = jnp.cos(ang)
    sin = jnp.sin(ang)

    def norm_rope(t, w_ref):
        t3 = t.reshape(2 * N_HEADS, HEAD_DIM, tm)
        ms3 = jnp.mean(t3 * t3, axis=1, keepdims=True)
        t3 = t3 * lax.rsqrt(ms3 + EPS) * w_ref[...]
        t1 = t3[:, :ROT_HALF, :]
        t2 = t3[:, ROT_HALF:ROT_DIM, :]
        r1 = t1 * cos - t2 * sin
        r2 = t2 * cos + t1 * sin
        t3 = jnp.concatenate([r1, r2, t3[:, ROT_DIM:, :]], axis=1)
        return t3.reshape(ATTN_WIDTH, tm)

    def store_head_major(ref, t):
        t = t.reshape(N_HEADS, V_HEAD_DIM, tm).astype(BF16)
        ta = ref.shape[-1]
        for j in range(tm // ta):
            ref[:, j] = t[:, :, j * ta:(j + 1) * ta]

    q = norm_rope(tr[0:ATTN_WIDTH], qnw_ref) * (HEAD_DIM ** -0.5 * LOG2E)
    store_head_major(qt_ref, q)
    k = norm_rope(tr[ATTN_WIDTH:2 * ATTN_WIDTH], knw_ref)
    k_ref[...] = k.T.astype(BF16)
    store_head_major(vt_ref, tr[2 * ATTN_WIDTH:3 * ATTN_WIDTH])
    store_head_major(zat_ref, tr[3 * ATTN_WIDTH:])


def _in_proj(x, positions, norm_w, w_in, q_norm_w, k_norm_w):
    b, s, d = x.shape
    tm, ta = TOKEN_TILE, ATTN_TILE
    nt, tiles_per_step = s // ta, tm // ta
    w_nat = w_in[:, :2 * SSM_WIDTH].astype(BF16)
    w_tr = w_in[:, 2 * SSM_WIDTH:].astype(BF16).T
    inv_freq = ROPE_THETA ** (-jnp.arange(0, ROT_DIM, 2, dtype=F32) / ROT_DIM)
    head_major = jax.ShapeDtypeStruct((b, N_HEADS, nt, V_HEAD_DIM, ta), BF16)
    head_spec = pl.BlockSpec((None, N_HEADS, tiles_per_step, V_HEAD_DIM, ta), lambda bi, ti: (bi, 0, ti, 0, 0))
    ssm_shape = jax.ShapeDtypeStruct((b, s, SSM_WIDTH), BF16)
    ssm_spec = pl.BlockSpec((None, tm, SSM_WIDTH), lambda bi, ti: (bi, ti, 0))
    const2 = lambda bi, ti: (0, 0)
    const3 = lambda bi, ti: (0, 0, 0)
    return pl.pallas_call(
        _in_proj_kernel,
        grid=(b, s // tm),
        in_specs=[
            pl.BlockSpec((None, tm, d), lambda bi, ti: (bi, ti, 0)),
            pl.BlockSpec((1, d), const2),
            pl.BlockSpec((d, 2 * SSM_WIDTH), const2),
            pl.BlockSpec((4 * ATTN_WIDTH, d), const2),
            pl.BlockSpec((None, 1, tm), lambda bi, ti: (bi, 0, ti)),
            pl.BlockSpec((ROT_HALF, 1), const2),
            pl.BlockSpec((1, HEAD_DIM, 1), const3),
            pl.BlockSpec((1, HEAD_DIM, 1), const3),
        ],
        out_specs=[
            ssm_spec, ssm_spec,
            head_spec,
            pl.BlockSpec((None, tm, ATTN_WIDTH), lambda bi, ti: (bi, ti, 0)),
            head_spec, head_spec,
        ],
        out_shape=(ssm_shape, ssm_shape, head_major,
                   jax.ShapeDtypeStruct((b, s, ATTN_WIDTH), BF16), head_major, head_major),
        compiler_params=_compiler_params(2),
        name="in_proj",
    )(x, norm_w.reshape(1, d), w_nat, w_tr, positions.reshape(b, 1, s),
      inv_freq.reshape(ROT_HALF, 1), q_norm_w.reshape(1, HEAD_DIM, 1), k_norm_w.reshape(1, HEAD_DIM, 1))


def _ssm_kernel(u_ref, zs_ref, pin_ref, pout_ref, ab_ref, ca_ref, cd_ref, ar_ref, ai_ref, d_ref,
                gw_ref, gb_ref, y_ref, uf_ref, bu_ref, xs_ref, y2_ref, yg_ref, state_ref,
                bw_ref, cs_ref, tmp_ref):
    n_batch, steps, width = u_ref.shape
    fold, perm_out_rows, perm_in_rows = pin_ref.shape
    perm_steps = perm_in_rows // n_batch
    part_steps = SSM_PART_STEPS
    perms_per_part = part_steps // perm_steps
    part_rows = perms_per_part * perm_out_rows
    n_parts = steps // part_steps
    blk = 2 * BLOCK_STATE
    fw = fold * V7X_LANES
    pack_rows = 2 * V7X_SUBLANES

    @pl.when(pl.program_id(0) == 0)
    def _():
        state_ref[...] = jnp.zeros_like(state_ref)
        bw_ref[...] = jnp.zeros_like(bw_ref)
        for m in range(N_SSM_BLOCKS):
            tmp_ref[...] = jnp.zeros_like(tmp_ref)
            for i in range(fold):
                for gl in range(GROUPS_PER_BLOCK):
                    g = m * GROUPS_PER_BLOCK + gl
                    in_rows = slice(i * V7X_LANES + gl * SSM_GROUP, i * V7X_LANES + (gl + 1) * SSM_GROUP)
                    for ri in range(2):
                        st_cols = slice(ri * BLOCK_STATE + gl * SSM_STATE, ri * BLOCK_STATE + (gl + 1) * SSM_STATE)
                        bw_ref[m, in_rows, st_cols] = ab_ref[i, ri, g].astype(BF16)
                        tmp_ref[in_rows, st_cols] = ca_ref[i, ri, g]
            cs_ref[m] = tmp_ref[...].T.astype(BF16)

    def rows_of(part):
        return slice(part * part_rows, (part + 1) * part_rows)

    def fold_in(part):
        for h in range(perms_per_part):
            t0 = part * part_steps + h * perm_steps
            u_bt = u_ref[:, t0:t0 + perm_steps, :].reshape(perm_in_rows, width)
            r0 = part * part_rows + h * perm_out_rows
            for i in range(fold):
                sel = jnp.dot(pin_ref[i], u_bt, preferred_element_type=F32).astype(BF16)
                for m in range(N_SSM_BLOCKS):
                    uf_ref[r0:r0 + perm_out_rows, m * fw + i * V7X_LANES:m * fw + (i + 1) * V7X_LANES] = (
                        sel[:, m * V7X_LANES:(m + 1) * V7X_LANES])

    def state_block(part, m):
        rows = rows_of(part)
        re = slice(m * blk, m * blk + BLOCK_STATE)
        im = slice(m * blk + BLOCK_STATE, (m + 1) * blk)
        bu_ref[rows, m * blk:(m + 1) * blk] = jnp.dot(
            uf_ref[rows, m * fw:(m + 1) * fw], bw_ref[m], preferred_element_type=F32)
        a_re = jnp.broadcast_to(ar_ref[:, m * BLOCK_STATE:(m + 1) * BLOCK_STATE], (V7X_SUBLANES, BLOCK_STATE))
        a_im = jnp.broadcast_to(ai_ref[:, m * BLOCK_STATE:(m + 1) * BLOCK_STATE], (V7X_SUBLANES, BLOCK_STATE))
        x_re = state_ref[:, re]
        x_im = state_ref[:, im]
        for r0 in range(part * part_rows, (part + 1) * part_rows, pack_rows):
            ins_re, ins_im = [], []
            for half in range(2):
                rr = slice(r0 + half * V7X_SUBLANES, r0 + (half + 1) * V7X_SUBLANES)
                ins_re.append(x_re)
                ins_im.append(x_im)
                n_re = a_re * x_re - a_im * x_im + bu_ref[rr, re]
                n_im = a_re * x_im + a_im * x_re + bu_ref[rr, im]
                x_re, x_im = n_re, n_im
            xs_ref[r0:r0 + pack_rows, re] = jnp.concatenate(ins_re, axis=0).astype(BF16)
            xs_ref[r0:r0 + pack_rows, im] = jnp.concatenate(ins_im, axis=0).astype(BF16)
        state_ref[:, re] = x_re
        state_ref[:, im] = x_im
        y2_ref[rows, m * fw:(m + 1) * fw] = (
            jnp.dot(xs_ref[rows, m * blk:(m + 1) * blk], cs_ref[m], preferred_element_type=F32)
            + jnp.dot(uf_ref[rows, m * fw:(m + 1) * fw], cd_ref[m], preferred_element_type=F32))

    def gate_offset(part, i):
        rows = rows_of(part)
        cols = [slice(m * fw + i * V7X_LANES, m * fw + (i + 1) * V7X_LANES) for m in range(N_SSM_BLOCKS)]
        y = jnp.concatenate([y2_ref[rows, c] for c in cols], axis=-1)
        u_i = jnp.concatenate([uf_ref[rows, c] for c in cols], axis=-1).astype(F32)
        y = _gelu_tanh(y + d_ref[...] * u_i)
        gate = jnp.dot(y.astype(BF16), gw_ref[...], preferred_element_type=F32) + gb_ref[...]
        y = (y * jax.nn.sigmoid(gate)).astype(BF16)
        for h in range(perms_per_part):
            yg_ref[part * perms_per_part + h, i * perm_out_rows:(i + 1) * perm_out_rows, :] = (
                y[h * perm_out_rows:(h + 1) * perm_out_rows])

    def unfold_out(part):
        for h in range(perms_per_part):
            t0 = part * part_steps + h * perm_steps
            y_bt = jnp.dot(pout_ref[...], yg_ref[part * perms_per_part + h], preferred_element_type=F32)
            z = zs_ref[:, t0:t0 + perm_steps, :].reshape(perm_in_rows, width).astype(F32)
            y_ref[:, t0:t0 + perm_steps, :] = (
                (y_bt * jax.nn.silu(z)).astype(BF16).reshape(n_batch, perm_steps, width))

    assert fold == N_SSM_BLOCKS
    for part in range(n_parts + 1):
        if part < n_parts:
            fold_in(part)
        for j in range(N_SSM_BLOCKS):
            if part < n_parts:
                state_block(part, j)
            if part >= 1:
                gate_offset(part - 1, j)
        if part >= 1:
            unfold_out(part - 1)


def _ssm_direct_weights(cl, bbc):
    fold, gpb, nb, p = SSM_FOLD, GROUPS_PER_BLOCK, N_SSM_BLOCKS, SSM_GROUP
    in_lanes = fold * gpb * p
    lags = jnp.einsum('lgpn,gqn->lgqp', cl, bbc, precision=lax.Precision.HIGHEST)
    zero = jnp.zeros_like(lags[0])
    toep = jnp.stack([jnp.stack([lags[i - k] if i >= k else zero for i in range(fold)]) for k in range(fold)])
    toep = toep.reshape(fold, fold, nb, gpb, p, p).transpose(2, 0, 3, 4, 1, 5).reshape(nb * in_lanes, fold * p)
    col = np.arange(in_lanes)
    spread = (np.arange(fold * p)[:, None] == ((col // (gpb * p)) * p + col % p)[None, :]).astype(np.float32)
    in_group = (col // p) % gpb
    same_group = (in_group[:, None] == in_group[None, :]).astype(np.float32)
    cd = jnp.dot(toep, spread, precision=lax.Precision.HIGHEST).reshape(nb, in_lanes, in_lanes)
    return (cd * same_group).astype(BF16)


def _ssm_branch(u, zs, ab, ca, cl, bbc, al, d_skip, glu_w, glu_b):
    n_batch, s, _ = u.shape
    assert n_batch == V7X_SUBLANES, "one time group of all batches must fill one sublane tile"
    fold, steps, perm_steps = SSM_FOLD, SSM_STEPS, SSM_PERM_STEPS
    g_rows = (perm_steps // fold) * n_batch
    n_rows = perm_steps * n_batch
    r = np.arange(g_rows)
    src = (r % n_batch) * perm_steps + fold * (r // n_batch)
    pin = np.arange(n_rows)[None, None, :] == (src[None, :, None] + np.arange(fold)[:, None, None])
    pout = jnp.asarray(pin.reshape(fold * g_rows, n_rows).T, dtype=BF16)
    pin = jnp.asarray(pin, dtype=BF16)
    cd = _ssm_direct_weights(cl, bbc)
    row_spec = pl.BlockSpec((n_batch, steps, SSM_WIDTH), lambda i: (0, i, 0))
    const2 = lambda i: (0, 0)
    const3 = lambda i: (0, 0, 0)
    n_state = SSM_GROUPS * SSM_STATE
    f_rows = (steps // fold) * n_batch
    fw_all = fold * SSM_WIDTH
    return pl.pallas_call(
        _ssm_kernel,
        grid=(s // steps,),
        in_specs=[
            row_spec, row_spec,
            pl.BlockSpec(pin.shape, const3),
            pl.BlockSpec(pout.shape, const2),
            pl.BlockSpec(ab.shape, lambda i: (0, 0, 0, 0, 0)),
            pl.BlockSpec(ca.shape, lambda i: (0, 0, 0, 0, 0)),
            pl.BlockSpec(cd.shape, const3),
            pl.BlockSpec((1, n_state), const2),
            pl.BlockSpec((1, n_state), const2),
            pl.BlockSpec((1, SSM_WIDTH), const2),
            pl.BlockSpec((SSM_WIDTH, SSM_WIDTH), const2),
            pl.BlockSpec((1, SSM_WIDTH), const2),
        ],
        out_specs=row_spec,
        out_shape=jax.ShapeDtypeStruct((n_batch, s, SSM_WIDTH), BF16),
        scratch_shapes=[
            pltpu.VMEM((f_rows, fw_all), BF16),
            pltpu.VMEM((f_rows, 2 * n_state), F32),
            pltpu.VMEM((f_rows, 2 * n_state), BF16),
            pltpu.VMEM((f_rows, fw_all), F32),
            pltpu.VMEM((steps // perm_steps, fold * g_rows, SSM_WIDTH), BF16),
            pltpu.VMEM((V7X_SUBLANES, 2 * n_state), F32),
            pltpu.VMEM((N_SSM_BLOCKS, fold * V7X_LANES, 2 * BLOCK_STATE), BF16),
            pltpu.VMEM((N_SSM_BLOCKS, 2 * BLOCK_STATE, fold * V7X_LANES), BF16),
            pltpu.VMEM((fold * V7X_LANES, 2 * BLOCK_STATE), F32),
        ],
        compiler_params=_compiler_params(1),
        name="ssm_branch",
    )(u, zs, pin, pout, ab, ca, cd, al[0].reshape(1, n_state), al[1].reshape(1, n_state),
      d_skip.reshape(1, SSM_WIDTH), glu_w.astype(BF16), glu_b.reshape(1, SSM_WIDTH))


def _attn_kernel(lambda_init, qt_ref, k_ref, vt_ref, zat_ref, lq1_ref, lk1_ref, lq2_ref, lk2_ref,
                 subw_ref, y_ref, wq_ref, s_ref, mx_ref, acc_ref, m_ref, l_ref):
    n_tiles, _, tq = qt_ref.shape
    tk = tq
    diag_slot = 2

    def load_queries(q):
        wq_ref[0, :HEAD_DIM] = qt_ref[q, :HEAD_DIM]
        wq_ref[1, HEAD_DIM:] = qt_ref[q, HEAD_DIM:]

    def reset_stats():
        m_ref[...] = jnp.full_like(m_ref, MASK_VALUE)
        l_ref[...] = jnp.zeros_like(l_ref)
        acc_ref[...] = jnp.zeros_like(acc_ref)

    def scores(j, slot):
        kb = k_ref[pl.ds(pl.multiple_of(j * tk, tk), tk), :]
        for c in range(2):
            s = jnp.dot(kb, wq_ref[c], preferred_element_type=F32)
            s_ref[slot, c] = s
            mx_ref[slot, c] = jnp.max(s, axis=0, keepdims=True)

    def diagonal_scores(q, slot):
        h = tk // 2
        row0 = pl.multiple_of(q * tk, tk)
        k_lo = k_ref[pl.ds(row0, h), :]
        k_hi = k_ref[pl.ds(row0 + h, h), :]
        causal = (lax.broadcasted_iota(jnp.int32, (h, h), 0) <= lax.broadcasted_iota(jnp.int32, (h, h), 1))
        for c in range(2):
            s_lo = jnp.dot(k_lo, wq_ref[c], preferred_element_type=F32)
            s_hi = jnp.dot(k_hi, wq_ref[c, :, h:], preferred_element_type=F32)
            s_ll = jnp.where(causal, s_lo[:, :h], MASK_VALUE)
            s_hh = jnp.where(causal, s_hi, MASK_VALUE)
            s_ref[slot, c, :h, :h] = s_ll
            s_ref[slot, c, :h, h:] = s_lo[:, h:]
            s_ref[slot, c, h:, :h] = jnp.full((h, h), MASK_VALUE, F32)
            s_ref[slot, c, h:, h:] = s_hh
            mx_ref[slot, c, :, :h] = jnp.max(s_ll, axis=0, keepdims=True)
            mx_ref[slot, c, :, h:] = jnp.maximum(jnp.max(s_lo[:, h:], axis=0, keepdims=True),
                                                 jnp.max(s_hh, axis=0, keepdims=True))

    def accumulate(j, slot):
        vb = vt_ref[j]
        for c in range(2):
            m_old = m_ref[c]
            m_new = jnp.maximum(m_old, mx_ref[slot, c])
            alpha = jnp.exp2(m_old - m_new)
            p = jnp.exp2(s_ref[slot, c] - m_new)
            l_ref[c] = alpha * l_ref[c] + jnp.sum(p, axis=0, keepdims=True)
            acc_ref[c] = alpha * acc_ref[c] + jnp.dot(vb, p.astype(BF16), preferred_element_type=F32)
            m_ref[c] = m_new

    def finalize(q):
        lam = (jnp.exp(jnp.sum(lq1_ref[...] * lk1_ref[...], axis=-1, keepdims=True))
               - jnp.exp(jnp.sum(lq2_ref[...] * lk2_ref[...], axis=-1, keepdims=True)) + lambda_init)
        out = acc_ref[0] * (1.0 / l_ref[0]) - acc_ref[1] * (lam / l_ref[1])
        ms = jnp.mean(out * out, axis=0, keepdims=True)
        out = out * (lax.rsqrt(ms + EPS) * (1.0 - lambda_init)) * subw_ref[...]
        out = out * jax.nn.silu(zat_ref[q].astype(F32))
        y_ref[pl.ds(pl.multiple_of(q * tq, tq), tq), :] = out.T.astype(BF16)

    zeros = jnp.zeros((HEAD_DIM, tq), BF16)
    wq_ref[0, HEAD_DIM:] = zeros
    wq_ref[1, :HEAD_DIM] = zeros
    load_queries(0)
    reset_stats()
    diagonal_scores(0, 1)

    def enter_tile(q, pending_slot):
        load_queries(q)
        diagonal_scores(q, diag_slot)
        accumulate(jnp.maximum(q - 2, 0), pending_slot)
        finalize(q - 1)
        reset_stats()
        scores(0, 0)
        accumulate(q, diag_slot)

    def pair(i, c):
        j = 2 * i
        scores(j + 1, 1)
        accumulate(j, 0)
        scores(j + 2, 0)
        accumulate(j + 1, 1)
        return c

    def two_tiles(k, carry):
        q_odd = 2 * k + 1
        enter_tile(q_odd, 1)
        lax.fori_loop(0, k, pair, 0)
        q_even = q_odd + 1
        enter_tile(q_even, 0)
        lax.fori_loop(0, k, pair, 0)
        scores(q_even - 1, 1)
        accumulate(q_even - 2, 0)
        return carry

    assert n_tiles % 2 == 0
    lax.fori_loop(0, (n_tiles - 2) // 2, two_tiles, 0)
    q_last = n_tiles - 1
    enter_tile(q_last, 1)
    lax.fori_loop(0, (q_last - 1) // 2, pair, 0)
    accumulate(q_last - 1, 0)
    finalize(q_last)


def _attention(qt, k, vt, zat, lq1, lk1, lq2, lk2, subln_w, lambda_init):
    b, _, nt, _, tq = qt.shape
    s = nt * tq
    assert nt >= 2
    row64 = pl.BlockSpec((1, HEAD_DIM), lambda bi, hi: (0, 0))
    head_spec = pl.BlockSpec((None, None, nt, V_HEAD_DIM, tq), lambda bi, hi: (bi, hi, 0, 0, 0))
    token_spec = pl.BlockSpec((None, s, V_HEAD_DIM), lambda bi, hi: (bi, 0, hi))
    return pl.pallas_call(
        functools.partial(_attn_kernel, lambda_init),
        grid=(b, N_HEADS),
        in_specs=[
            head_spec, token_spec, head_spec, head_spec,
            row64, row64, row64, row64,
            pl.BlockSpec((V_HEAD_DIM, 1), lambda bi, hi: (0, 0)),
        ],
        out_specs=token_spec,
        out_shape=jax.ShapeDtypeStruct((b, s, ATTN_WIDTH), BF16),
        scratch_shapes=[
            pltpu.VMEM((2, 2 * HEAD_DIM, tq), BF16),
            pltpu.VMEM((3, 2, tq, tq), F32),
            pltpu.VMEM((3, 2, 1, tq), F32),
            pltpu.VMEM((2, V_HEAD_DIM, tq), F32),
            pltpu.VMEM((2, 1, tq), F32),
            pltpu.VMEM((2, 1, tq), F32),
        ],
        compiler_params=_compiler_params(2),
        name="diff_attention",
    )(qt, k, vt, zat, lq1.reshape(1, HEAD_DIM), lk1.reshape(1, HEAD_DIM),
      lq2.reshape(1, HEAD_DIM), lk2.reshape(1, HEAD_DIM), subln_w.reshape(V_HEAD_DIM, 1))


def _out_kernel(x_ref, ys_ref, ya_ref, p_ref, wos_ref, woa_ref, wg_ref, wp_ref, o_ref):
    x2 = (x_ref[...]
          + jnp.dot(ys_ref[...], wos_ref[...], preferred_element_type=F32)
          + jnp.dot(ya_ref[...], woa_ref[...], preferred_element_type=F32))
    gate = jax.nn.sigmoid(jnp.dot(x2.astype(BF16), wg_ref[...], preferred_element_type=F32))
    ple = jnp.dot(p_ref[...].astype(BF16), wp_ref[...], preferred_element_type=F32)
    o_ref[...] = x2 + gate * ple


def _out_proj(x, ys, ya, p, w_out, w_proj, w_gate):
    b, s, d = x.shape
    tm = OUT_TOKEN_TILE
    const2 = lambda bi, ti: (0, 0)
    return pl.pallas_call(
        _out_kernel,
        grid=(b, s // tm),
        in_specs=[
            pl.BlockSpec((None, tm, d), lambda bi, ti: (bi, ti, 0)),
            pl.BlockSpec((None, tm, SSM_WIDTH), lambda bi, ti: (bi, ti, 0)),
            pl.BlockSpec((None, tm, ATTN_WIDTH), lambda bi, ti: (bi, ti, 0)),
            pl.BlockSpec((None, tm, PLE_DIM), lambda bi, ti: (bi, ti, 0)),
            pl.BlockSpec((SSM_WIDTH, d), const2),
            pl.BlockSpec((ATTN_WIDTH, d), const2),
            pl.BlockSpec((d, d), const2),
            pl.BlockSpec((PLE_DIM, d), const2),
        ],
        out_specs=pl.BlockSpec((None, tm, d), lambda bi, ti: (bi, ti, 0)),
        out_shape=jax.ShapeDtypeStruct((b, s, d), F32),
        compiler_params=_compiler_params(2),
        name="out_proj",
    )(x, ys, ya, p, w_out[:SSM_WIDTH].astype(BF16), w_out[SSM_WIDTH:].astype(BF16),
      w_gate.astype(BF16), w_proj.astype(BF16))


def kernel(x, p, positions, norm_w, w_in, ssm_lambda_re, ssm_lambda_im, ssm_log_dt, ssm_b_re, ssm_b_im, ssm_c_re, ssm_c_im, ssm_d, glu_w, glu_b, q_norm_w, k_norm_w, lambda_q1, lambda_k1, lambda_q2, lambda_k2, subln_w, w_out, ple_w_proj, ple_w_gate):
    depth = norm_w.shape[0]
    for i in range(depth):
        lambda_init = 0.8 - 0.6 * math.exp(-0.3 * i)
        ab, ca, cl, bbc, al = _ssm_params(ssm_lambda_re[i], ssm_lambda_im[i], ssm_log_dt[i],
                                          ssm_b_re[i], ssm_b_im[i], ssm_c_re[i], ssm_c_im[i])
        u, zs, qt, k, vt, zat = _in_proj(x, positions, norm_w[i], w_in[i], q_norm_w[i], k_norm_w[i])
        ys = _ssm_branch(u, zs, ab, ca, cl, bbc, al, ssm_d[i], glu_w[i], glu_b[i])
        ya = _attention(qt, k, vt, zat, lambda_q1[i], lambda_k1[i], lambda_q2[i], lambda_k2[i],
                        subln_w[i], lambda_init)
        x = _out_proj(x, ys, ya, p[i], w_out[i], ple_w_proj[i], ple_w_gate[i])
    return x
```

```python
import functools
import math

import jax
import jax.numpy as jnp
import numpy as np
from jax import lax
from jax.experimental import pallas as pl
from jax.experimental.pallas import tpu as pltpu

F32 = jnp.float32
BF16 = jnp.bfloat16

D_MODEL = 1024
PLE_DIM = 256
SSM_WIDTH = 512
SSM_GROUP = 16
SSM_GROUPS = 32
SSM_STATE = 64
ATTN_WIDTH = 512
N_HEADS = 4
HEAD_DIM = 64
V_HEAD_DIM = 128
ROT_DIM = 16
ROT_HALF = 8
ROPE_THETA = 500000.0
EPS = 1e-6
LOG2E = math.log2(math.e)

V7X_LANES = 128
V7X_SUBLANES = 8
V7X_VMEM_BYTES = 64 * 1024 * 1024
VMEM_LIMIT_BYTES = V7X_VMEM_BYTES * 7 // 8

TOKEN_TILE = 1024
ATTN_TILE = 512
OUT_TOKEN_TILE = 1024
SSM_FOLD = 4
SSM_STEPS = 128
SSM_PART_STEPS = 64
SSM_PERM_STEPS = 32
GROUPS_PER_BLOCK = 8
N_SSM_BLOCKS = SSM_GROUPS // GROUPS_PER_BLOCK
BLOCK_STATE = GROUPS_PER_BLOCK * SSM_STATE
MASK_VALUE = -1e30


def _gelu_tanh(x):
    c0 = math.sqrt(2.0 / math.pi)
    inner = x * (c0 + (c0 * 0.044715) * (x * x))
    half = 0.5 * x
    return half + half * jnp.tanh(inner)


def _compiler_params(n_grid_axes):
    return pltpu.CompilerParams(
        dimension_semantics=("arbitrary",) * n_grid_axes,
        vmem_limit_bytes=VMEM_LIMIT_BYTES,
    )


def _ssm_params_kernel(lr_ref, li_ref, logdt_ref, br_ref, bi_ref, cr_ref, ci_ref,
                       ab_ref, ca_ref, cl_ref, bbc_ref, al_ref):
    fold = ab_ref.shape[0]
    n = lr_ref.shape[-1]
    lr = lr_ref[...]
    li = li_ref[...]
    dt = jnp.exp(logdt_ref[...])
    mag = jnp.exp(lr * dt)
    a_re = mag * jnp.cos(li * dt)
    a_im = mag * jnp.sin(li * dt)
    nr = a_re - 1.0
    ni = a_im
    den = lr * lr + li * li
    coef_re = (nr * lr + ni * li) / den
    coef_im = (ni * lr - nr * li) / den
    br = br_ref[...]
    bi = bi_ref[...]
    bb_re = coef_re * br - coef_im * bi
    bb_im = coef_re * bi + coef_im * br
    c_re = cr_ref[...]
    c_im = ci_ref[...]

    def cmul(xr, xi, yr, yi):
        return xr * yr - xi * yi, xr * yi + xi * yr

    powers = [(jnp.ones_like(a_re), jnp.zeros_like(a_im))]
    for _ in range(fold):
        powers.append(cmul(*powers[-1], a_re, a_im))
    for i in range(fold):
        ab_ref[i, 0], ab_ref[i, 1] = cmul(*powers[fold - 1 - i], bb_re, bb_im)
        ca_re, ca_im = cmul(c_re, c_im, *powers[i + 1])
        ca_ref[i, 0] = ca_re
        ca_ref[i, 1] = -ca_im
        cl_re, cl_im = cmul(c_re, c_im, *powers[i])
        cl_ref[i, :, :, :n] = cl_re
        cl_ref[i, :, :, n:] = -cl_im
    bbc_ref[:, :, :n] = bb_re
    bbc_ref[:, :, n:] = bb_im
    al_ref[0] = powers[fold][0]
    al_ref[1] = powers[fold][1]


def _ssm_params(lam_re, lam_im, log_dt, b_re, b_im, c_re, c_im):
    g, n, p, fold = SSM_GROUPS, SSM_STATE, SSM_GROUP, SSM_FOLD
    per_offset = jax.ShapeDtypeStruct((fold, 2, g, p, n), F32)
    return pl.pallas_call(
        _ssm_params_kernel,
        out_shape=(per_offset, per_offset,
                   jax.ShapeDtypeStruct((fold, g, p, 2 * n), F32),
                   jax.ShapeDtypeStruct((g, p, 2 * n), F32),
                   jax.ShapeDtypeStruct((2, g, 1, n), F32)),
        name="ssm_params",
    )(lam_re.reshape(g, 1, n), lam_im.reshape(g, 1, n), log_dt.reshape(g, 1, 1),
      jnp.swapaxes(b_re, 1, 2), jnp.swapaxes(b_im, 1, 2), c_re, c_im)


def _in_proj_kernel(x_ref, nw_ref, w_ref, pos_ref, invf_ref, qnw_ref, knw_ref,
                    u_ref, zs_ref, qt_ref, k_ref, vt_ref, zat_ref, wnat_ref, wtr_ref):
    @pl.when((pl.program_id(0) == 0) & (pl.program_id(1) == 0))
    def _():
        n_nat = wnat_ref.shape[1]
        wnat_ref[...] = w_ref[:, :n_nat].astype(BF16)
        chunk = wtr_ref.shape[1] // 2
        for c0 in range(0, wtr_ref.shape[0], chunk):
            wtr_ref[c0:c0 + chunk, :] = w_ref[:, n_nat + c0:n_nat + c0 + chunk].T.astype(BF16)

    x = x_ref[...]
    ms = jnp.mean(x * x, axis=-1, keepdims=True)
    h = (x * lax.rsqrt(ms + EPS) * nw_ref[...]).astype(BF16)
    nat = jnp.dot(h, wnat_ref[...], preferred_element_type=F32)
    u_ref[...] = nat[:, :SSM_WIDTH].astype(BF16)
    zs_ref[...] = nat[:, SSM_WIDTH:].astype(BF16)
    tr = lax.dot_general(wtr_ref[...], h, (((1,), (1,)), ((), ())),
                         preferred_element_type=F32)
    tm = x.shape[0]
    ang = pos_ref[...].astype(F32) * invf_ref[...]
    cos = jnp.cos(ang)
    sin = jnp.sin(ang)

    def norm_rope(t, w_ref):
        t3 = t.reshape(2 * N_HEADS, HEAD_DIM, tm)
        ms3 = jnp.mean(t3 * t3, axis=1, keepdims=True)
        t3 = t3 * lax.rsqrt(ms3 + EPS) * w_ref[...]
        t1 = t3[:, :ROT_HALF, :]
        t2 = t3[:, ROT_HALF:ROT_DIM, :]
        r1 = t1 * cos - t2 * sin
        r2 = t2 * cos + t1 * sin
        t3 = jnp.concatenate([r1, r2, t3[:, ROT_DIM:, :]], axis=1)
        return t3.reshape(ATTN_WIDTH, tm)

    def store_head_major(ref, t):
        t = t.reshape(N_HEADS, V_HEAD_DIM, tm).astype(BF16)
        ta = ref.shape[-1]
        for j in range(tm // ta):
            ref[:, j] = t[:, :, j * ta:(j + 1) * ta]

    q = norm_rope(tr[0:ATTN_WIDTH], qnw_ref) * (HEAD_DIM ** -0.5 * LOG2E)
    store_head_major(qt_ref, q)
    k = norm_rope(tr[ATTN_WIDTH:2 * ATTN_WIDTH], knw_ref)
    k_ref[...] = k.T.astype(BF16)
    store_head_major(vt_ref, tr[2 * ATTN_WIDTH:3 * ATTN_WIDTH])
    store_head_major(zat_ref, tr[3 * ATTN_WIDTH:])


def _in_proj(x, positions, norm_w, w_in, q_norm_w, k_norm_w):
    b, s, d = x.shape
    tm, ta = TOKEN_TILE, ATTN_TILE
    nt, tiles_per_step = s // ta, tm // ta
    inv_freq = ROPE_THETA ** (-jnp.arange(0, ROT_DIM, 2, dtype=F32) / ROT_DIM)
    head_major = jax.ShapeDtypeStruct((b, N_HEADS, nt, V_HEAD_DIM, ta), BF16)
    head_spec = pl.BlockSpec((None, N_HEADS, tiles_per_step, V_HEAD_DIM, ta), lambda bi, ti: (bi, 0, ti, 0, 0))
    ssm_shape = jax.ShapeDtypeStruct((b, s, SSM_WIDTH), BF16)
    ssm_spec = pl.BlockSpec((None, tm, SSM_WIDTH), lambda bi, ti: (bi, ti, 0))
    const2 = lambda bi, ti: (0, 0)
    const3 = lambda bi, ti: (0, 0, 0)
    return pl.pallas_call(
        _in_proj_kernel,
        grid=(b, s // tm),
        in_specs=[
            pl.BlockSpec((None, tm, d), lambda bi, ti: (bi, ti, 0)),
            pl.BlockSpec((1, d), const2),
            pl.BlockSpec(w_in.shape, const2),
            pl.BlockSpec((None, 1, tm), lambda bi, ti: (bi, 0, ti)),
            pl.BlockSpec((ROT_HALF, 1), const2),
            pl.BlockSpec((1, HEAD_DIM, 1), const3),
            pl.BlockSpec((1, HEAD_DIM, 1), const3),
        ],
        out_specs=[
            ssm_spec, ssm_spec,
            head_spec,
            pl.BlockSpec((None, tm, ATTN_WIDTH), lambda bi, ti: (bi, ti, 0)),
            head_spec, head_spec,
        ],
        out_shape=(ssm_shape, ssm_shape, head_major,
                   jax.ShapeDtypeStruct((b, s, ATTN_WIDTH), BF16), head_major, head_major),
        scratch_shapes=[
            pltpu.VMEM((d, 2 * SSM_WIDTH), BF16),
            pltpu.VMEM((4 * ATTN_WIDTH, d), BF16),
        ],
        compiler_params=_compiler_params(2),
        name="in_proj",
    )(x, norm_w.reshape(1, d), w_in, positions.reshape(b, 1, s),
      inv_freq.reshape(ROT_HALF, 1), q_norm_w.reshape(1, HEAD_DIM, 1), k_norm_w.reshape(1, HEAD_DIM, 1))


def _ssm_kernel(u_ref, zs_ref, pin_ref, pout_ref, ab_ref, ca_ref, cd_ref, ar_ref, ai_ref, d_ref,
                gw_ref, gb_ref, y_ref, uf_ref, bu_ref, xs_ref, y2_ref, yg_ref, state_ref,
                bw_ref, cs_ref, tmp_ref):
    n_batch, steps, width = u_ref.shape
    fold, perm_out_rows, perm_in_rows = pin_ref.shape
    perm_steps = perm_in_rows // n_batch
    part_steps = SSM_PART_STEPS
    perms_per_part = part_steps // perm_steps
    part_rows = perms_per_part * perm_out_rows
    n_parts = steps // part_steps
    blk = 2 * BLOCK_STATE
    fw = fold * V7X_LANES
    pack_rows = 2 * V7X_SUBLANES

    @pl.when(pl.program_id(0) == 0)
    def _():
        state_ref[...] = jnp.zeros_like(state_ref)
        bw_ref[...] = jnp.zeros_like(bw_ref)
        for m in range(N_SSM_BLOCKS):
            tmp_ref[...] = jnp.zeros_like(tmp_ref)
            for i in range(fold):
                for gl in range(GROUPS_PER_BLOCK):
                    g = m * GROUPS_PER_BLOCK + gl
                    in_rows = slice(i * V7X_LANES + gl * SSM_GROUP, i * V7X_LANES + (gl + 1) * SSM_GROUP)
                    for ri in range(2):
                        st_cols = slice(ri * BLOCK_STATE + gl * SSM_STATE, ri * BLOCK_STATE + (gl + 1) * SSM_STATE)
                        bw_ref[m, in_rows, st_cols] = ab_ref[i, ri, g].astype(BF16)
                        tmp_ref[in_rows, st_cols] = ca_ref[i, ri, g]
            cs_ref[m] = tmp_ref[...].T.astype(BF16)

    def rows_of(part):
        return slice(part * part_rows, (part + 1) * part_rows)

    def fold_in(part):
        for h in range(perms_per_part):
            t0 = part * part_steps + h * perm_steps
            u_bt = u_ref[:, t0:t0 + perm_steps, :].reshape(perm_in_rows, width)
            r0 = part * part_rows + h * perm_out_rows
            for i in range(fold):
                sel = jnp.dot(pin_ref[i], u_bt, preferred_element_type=F32).astype(BF16)
                for m in range(N_SSM_BLOCKS):
                    uf_ref[r0:r0 + perm_out_rows, m * fw + i * V7X_LANES:m * fw + (i + 1) * V7X_LANES] = (
                        sel[:, m * V7X_LANES:(m + 1) * V7X_LANES])

    def state_block(part, m):
        rows = rows_of(part)
        re = slice(m * blk, m * blk + BLOCK_STATE)
        im = slice(m * blk + BLOCK_STATE, (m + 1) * blk)
        bu_ref[rows, m * blk:(m + 1) * blk] = jnp.dot(
            uf_ref[rows, m * fw:(m + 1) * fw], bw_ref[m], preferred_element_type=F32)
        a_re = jnp.broadcast_to(ar_ref[:, m * BLOCK_STATE:(m + 1) * BLOCK_STATE], (V7X_SUBLANES, BLOCK_STATE))
        a_im = jnp.broadcast_to(ai_ref[:, m * BLOCK_STATE:(m + 1) * BLOCK_STATE], (V7X_SUBLANES, BLOCK_STATE))
        x_re = state_ref[:, re]
        x_im = state_ref[:, im]
        for r0 in range(part * part_rows, (part + 1) * part_rows, pack_rows):
            ins_re, ins_im = [], []
            for half in range(2):
                rr = slice(r0 + half * V7X_SUBLANES, r0 + (half + 1) * V7X_SUBLANES)
                ins_re.append(x_re)
                ins_im.append(x_im)
                n_re = a_re * x_re - a_im * x_im + bu_ref[rr, re]
                n_im = a_re * x_im + a_im * x_re + bu_ref[rr, im]
                x_re, x_im = n_re, n_im
            xs_ref[r0:r0 + pack_rows, re] = jnp.concatenate(ins_re, axis=0).astype(BF16)
            xs_ref[r0:r0 + pack_rows, im] = jnp.concatenate(ins_im, axis=0).astype(BF16)
        state_ref[:, re] = x_re
        state_ref[:, im] = x_im
        y2_ref[rows, m * fw:(m + 1) * fw] = (
            jnp.dot(xs_ref[rows, m * blk:(m + 1) * blk], cs_ref[m], preferred_element_type=F32)
            + jnp.dot(uf_ref[rows, m * fw:(m + 1) * fw], cd_ref[m], preferred_element_type=F32))

    def gate_offset(part, i):
        rows = rows_of(part)
        cols = [slice(m * fw + i * V7X_LANES, m * fw + (i + 1) * V7X_LANES) for m in range(N_SSM_BLOCKS)]
        y = jnp.concatenate([y2_ref[rows, c] for c in cols], axis=-1)
        u_i = jnp.concatenate([uf_ref[rows, c] for c in cols], axis=-1).astype(F32)
        y = _gelu_tanh(y + d_ref[...] * u_i)
        gate = jnp.dot(y.astype(BF16), gw_ref[...], preferred_element_type=F32) + gb_ref[...]
        y = (y * jax.nn.sigmoid(gate)).astype(BF16)
        for h in range(perms_per_part):
            yg_ref[part * perms_per_part + h, i * perm_out_rows:(i + 1) * perm_out_rows, :] = (
                y[h * perm_out_rows:(h + 1) * perm_out_rows])

    def unfold_out(part):
        for h in range(perms_per_part):
            t0 = part * part_steps + h * perm_steps
            y_bt = jnp.dot(pout_ref[...], yg_ref[part * perms_per_part + h], preferred_element_type=F32)
            z = zs_ref[:, t0:t0 + perm_steps, :].reshape(perm_in_rows, width).astype(F32)
            y_ref[:, t0:t0 + perm_steps, :] = (
                (y_bt * jax.nn.silu(z)).astype(BF16).reshape(n_batch, perm_steps, width))

    assert fold == N_SSM_BLOCKS
    for part in range(n_parts + 1):
        if part < n_parts:
            fold_in(part)
        for j in range(N_SSM_BLOCKS):
            if part < n_parts:
                state_block(part, j)
            if part >= 1:
                gate_offset(part - 1, j)
        if part >= 1:
            unfold_out(part - 1)


def _ssm_direct_weights(cl, bbc):
    fold, gpb, nb, p = SSM_FOLD, GROUPS_PER_BLOCK, N_SSM_BLOCKS, SSM_GROUP
    in_lanes = fold * gpb * p
    lags = jnp.einsum('lgpn,gqn->lgqp', cl, bbc, precision=lax.Precision.HIGHEST)
    zero = jnp.zeros_like(lags[0])
    toep = jnp.stack([jnp.stack([lags[i - k] if i >= k else zero for i in range(fold)]) for k in range(fold)])
    toep = toep.reshape(fold, fold, nb, gpb, p, p).transpose(2, 0, 3, 4, 1, 5).reshape(nb * in_lanes, fold * p)
    col = np.arange(in_lanes)
    spread = (np.arange(fold * p)[:, None] == ((col // (gpb * p)) * p + col % p)[None, :]).astype(np.float32)
    in_group = (col // p) % gpb
    same_group = (in_group[:, None] == in_group[None, :]).astype(np.float32)
    cd = jnp.dot(toep, spread, precision=lax.Precision.HIGHEST).reshape(nb, in_lanes, in_lanes)
    return (cd * same_group).astype(BF16)


def _ssm_branch(u, zs, ab, ca, cl, bbc, al, d_skip, glu_w, glu_b):
    n_batch, s, _ = u.shape
    assert n_batch == V7X_SUBLANES, "one time group of all batches must fill one sublane tile"
    fold, steps, perm_steps = SSM_FOLD, SSM_STEPS, SSM_PERM_STEPS
    g_rows = (perm_steps // fold) * n_batch
    n_rows = perm_steps * n_batch
    r = np.arange(g_rows)
    src = (r % n_batch) * perm_steps + fold * (r // n_batch)
    pin = np.arange(n_rows)[None, None, :] == (src[None, :, None] + np.arange(fold)[:, None, None])
    pout = jnp.asarray(pin.reshape(fold * g_rows, n_rows).T, dtype=BF16)
    pin = jnp.asarray(pin, dtype=BF16)
    cd = _ssm_direct_weights(cl, bbc)
    row_spec = pl.BlockSpec((n_batch, steps, SSM_WIDTH), lambda i: (0, i, 0))
    const2 = lambda i: (0, 0)
    const3 = lambda i: (0, 0, 0)
    n_state = SSM_GROUPS * SSM_STATE
    f_rows = (steps // fold) * n_batch
    fw_all = fold * SSM_WIDTH
    return pl.pallas_call(
        _ssm_kernel,
        grid=(s // steps,),
        in_specs=[
            row_spec, row_spec,
            pl.BlockSpec(pin.shape, const3),
            pl.BlockSpec(pout.shape, const2),
            pl.BlockSpec(ab.shape, lambda i: (0, 0, 0, 0, 0)),
            pl.BlockSpec(ca.shape, lambda i: (0, 0, 0, 0, 0)),
            pl.BlockSpec(cd.shape, const3),
            pl.BlockSpec((1, n_state), const2),
            pl.BlockSpec((1, n_state), const2),
            pl.BlockSpec((1, SSM_WIDTH), const2),
            pl.BlockSpec((SSM_WIDTH, SSM_WIDTH), const2),
            pl.BlockSpec((1, SSM_WIDTH), const2),
        ],
        out_specs=row_spec,
        out_shape=jax.ShapeDtypeStruct((n_batch, s, SSM_WIDTH), BF16),
        scratch_shapes=[
            pltpu.VMEM((f_rows, fw_all), BF16),
            pltpu.VMEM((f_rows, 2 * n_state), F32),
            pltpu.VMEM((f_rows, 2 * n_state), BF16),
            pltpu.VMEM((f_rows, fw_all), F32),
            pltpu.VMEM((steps // perm_steps, fold * g_rows, SSM_WIDTH), BF16),
            pltpu.VMEM((V7X_SUBLANES, 2 * n_state), F32),
            pltpu.VMEM((N_SSM_BLOCKS, fold * V7X_LANES, 2 * BLOCK_STATE), BF16),
            pltpu.VMEM((N_SSM_BLOCKS, 2 * BLOCK_STATE, fold * V7X_LANES), BF16),
            pltpu.VMEM((fold * V7X_LANES, 2 * BLOCK_STATE), F32),
        ],
        compiler_params=_compiler_params(1),
        name="ssm_branch",
    )(u, zs, pin, pout, ab, ca, cd, al[0].reshape(1, n_state), al[1].reshape(1, n_state),
      d_skip.reshape(1, SSM_WIDTH), glu_w.astype(BF16), glu_b.reshape(1, SSM_WIDTH))


def _attn_kernel(lambda_init, qt_ref, k_ref, vt_ref, zat_ref, lq1_ref, lk1_ref, lq2_ref, lk2_ref,
                 subw_ref, y_ref, wq_ref, s_ref, mx_ref, acc_ref, m_ref, l_ref):
    n_tiles, _, tq = qt_ref.shape
    tk = tq
    diag_slot = 2

    def load_queries(q):
        wq_ref[0, :HEAD_DIM] = qt_ref[q, :HEAD_DIM]
        wq_ref[1, HEAD_DIM:] = qt_ref[q, HEAD_DIM:]

    def reset_stats():
        m_ref[...] = jnp.full_like(m_ref, MASK_VALUE)
        l_ref[...] = jnp.zeros_like(l_ref)
        acc_ref[...] = jnp.zeros_like(acc_ref)

    def scores(j, slot):
        kb = k_ref[pl.ds(pl.multiple_of(j * tk, tk), tk), :]
        for c in range(2):
            s = jnp.dot(kb, wq_ref[c], preferred_element_type=F32)
            s_ref[slot, c] = s
            mx_ref[slot, c] = jnp.max(s, axis=0, keepdims=True)

    def diagonal_scores(q, slot):
        h = tk // 2
        row0 = pl.multiple_of(q * tk, tk)
        k_lo = k_ref[pl.ds(row0, h), :]
        k_hi = k_ref[pl.ds(row0 + h, h), :]
        causal = (lax.broadcasted_iota(jnp.int32, (h, h), 0) <= lax.broadcasted_iota(jnp.int32, (h, h), 1))
        for c in range(2):
            s_lo = jnp.dot(k_lo, wq_ref[c], preferred_element_type=F32)
            s_hi = jnp.dot(k_hi, wq_ref[c, :, h:], preferred_element_type=F32)
            s_ll = jnp.where(causal, s_lo[:, :h], MASK_VALUE)
            s_hh = jnp.where(causal, s_hi, MASK_VALUE)
            s_ref[slot, c, :h, :h] = s_ll
            s_ref[slot, c, :h, h:] = s_lo[:, h:]
            s_ref[slot, c, h:, :h] = jnp.full((h, h), MASK_VALUE, F32)
            s_ref[slot, c, h:, h:] = s_hh
            mx_ref[slot, c, :, :h] = jnp.max(s_ll, axis=0, keepdims=True)
            mx_ref[slot, c, :, h:] = jnp.maximum(jnp.max(s_lo[:, h:], axis=0, keepdims=True),
                                                 jnp.max(s_hh, axis=0, keepdims=True))

    def accumulate(j, slot):
        vb = vt_ref[j]
        for c in range(2):
            m_old = m_ref[c]
            m_new = jnp.maximum(m_old, mx_ref[slot, c])
            alpha = jnp.exp2(m_old - m_new)
            p = jnp.exp2(s_ref[slot, c] - m_new)
            l_ref[c] = alpha * l_ref[c] + jnp.sum(p, axis=0, keepdims=True)
            acc_ref[c] = alpha * acc_ref[c] + jnp.dot(vb, p.astype(BF16), preferred_element_type=F32)
            m_ref[c] = m_new

    def finalize(q):
        lam = (jnp.exp(jnp.sum(lq1_ref[...] * lk1_ref[...], axis=-1, keepdims=True))
               - jnp.exp(jnp.sum(lq2_ref[...] * lk2_ref[...], axis=-1, keepdims=True)) + lambda_init)
        out = acc_ref[0] * (1.0 / l_ref[0]) - acc_ref[1] * (lam / l_ref[1])
        ms = jnp.mean(out * out, axis=0, keepdims=True)
        out = out * (lax.rsqrt(ms + EPS) * (1.0 - lambda_init)) * subw_ref[...]
        out = out * jax.nn.silu(zat_ref[q].astype(F32))
        y_ref[pl.ds(pl.multiple_of(q * tq, tq), tq), :] = out.T.astype(BF16)

    zeros = jnp.zeros((HEAD_DIM, tq), BF16)
    wq_ref[0, HEAD_DIM:] = zeros
    wq_ref[1, :HEAD_DIM] = zeros
    load_queries(0)
    reset_stats()
    diagonal_scores(0, 1)

    def enter_tile(q, pending_slot):
        load_queries(q)
        diagonal_scores(q, diag_slot)
        accumulate(jnp.maximum(q - 2, 0), pending_slot)
        finalize(q - 1)
        reset_stats()
        scores(0, 0)
        accumulate(q, diag_slot)

    def pair(i, c):
        j = 2 * i
        scores(j + 1, 1)
        accumulate(j, 0)
        scores(j + 2, 0)
        accumulate(j + 1, 1)
        return c

    def two_tiles(k, carry):
        q_odd = 2 * k + 1
        enter_tile(q_odd, 1)
        lax.fori_loop(0, k, pair, 0)
        q_even = q_odd + 1
        enter_tile(q_even, 0)
        lax.fori_loop(0, k, pair, 0)
        scores(q_even - 1, 1)
        accumulate(q_even - 2, 0)
        return carry

    assert n_tiles % 2 == 0
    lax.fori_loop(0, (n_tiles - 2) // 2, two_tiles, 0)
    q_last = n_tiles - 1
    enter_tile(q_last, 1)
    lax.fori_loop(0, (q_last - 1) // 2, pair, 0)
    accumulate(q_last - 1, 0)
    finalize(q_last)


def _attention(qt, k, vt, zat, lq1, lk1, lq2, lk2, subln_w, lambda_init):
    b, _, nt, _, tq = qt.shape
    s = nt * tq
    assert nt >= 2
    row64 = pl.BlockSpec((1, HEAD_DIM), lambda bi, hi: (0, 0))
    head_spec = pl.BlockSpec((None, None, nt, V_HEAD_DIM, tq), lambda bi, hi: (bi, hi, 0, 0, 0))
    token_spec = pl.BlockSpec((None, s, V_HEAD_DIM), lambda bi, hi: (bi, 0, hi))
    return pl.pallas_call(
        functools.partial(_attn_kernel, lambda_init),
        grid=(b, N_HEADS),
        in_specs=[
            head_spec, token_spec, head_spec, head_spec,
            row64, row64, row64, row64,
            pl.BlockSpec((V_HEAD_DIM, 1), lambda bi, hi: (0, 0)),
        ],
        out_specs=token_spec,
        out_shape=jax.ShapeDtypeStruct((b, s, ATTN_WIDTH), BF16),
        scratch_shapes=[
            pltpu.VMEM((2, 2 * HEAD_DIM, tq), BF16),
            pltpu.VMEM((3, 2, tq, tq), F32),
            pltpu.VMEM((3, 2, 1, tq), F32),
            pltpu.VMEM((2, V_HEAD_DIM, tq), F32),
            pltpu.VMEM((2, 1, tq), F32),
            pltpu.VMEM((2, 1, tq), F32),
        ],
        compiler_params=_compiler_params(2),
        name="diff_attention",
    )(qt, k, vt, zat, lq1.reshape(1, HEAD_DIM), lk1.reshape(1, HEAD_DIM),
      lq2.reshape(1, HEAD_DIM), lk2.reshape(1, HEAD_DIM), subln_w.reshape(V_HEAD_DIM, 1))


def _out_kernel(x_ref, ys_ref, ya_ref, p_ref, wos_ref, woa_ref, wg_ref, wp_ref, o_ref):
    x2 = (x_ref[...]
          + jnp.dot(ys_ref[...], wos_ref[...], preferred_element_type=F32)
          + jnp.dot(ya_ref[...], woa_ref[...], preferred_element_type=F32))
    gate = jax.nn.sigmoid(jnp.dot(x2.astype(BF16), wg_ref[...], preferred_element_type=F32))
    ple = jnp.dot(p_ref[...].astype(BF16), wp_ref[...], preferred_element_type=F32)
    o_ref[...] = x2 + gate * ple


def _out_proj(x, ys, ya, p, w_out, w_proj, w_gate):
    b, s, d = x.shape
    tm = OUT_TOKEN_TILE
    const2 = lambda bi, ti: (0, 0)
    return pl.pallas_call(
        _out_kernel,
        grid=(b, s // tm),
        in_specs=[
            pl.BlockSpec((None, tm, d), lambda bi, ti: (bi, ti, 0)),
            pl.BlockSpec((None, tm, SSM_WIDTH), lambda bi, ti: (bi, ti, 0)),
            pl.BlockSpec((None, tm, ATTN_WIDTH), lambda bi, ti: (bi, ti, 0)),
            pl.BlockSpec((None, tm, PLE_DIM), lambda bi, ti: (bi, ti, 0)),
            pl.BlockSpec((SSM_WIDTH, d), const2),
            pl.BlockSpec((ATTN_WIDTH, d), const2),
            pl.BlockSpec((d, d), const2),
            pl.BlockSpec((PLE_DIM, d), const2),
        ],
        out_specs=pl.BlockSpec((None, tm, d), lambda bi, ti: (bi, ti, 0)),
        out_shape=jax.ShapeDtypeStruct((b, s, d), F32),
        compiler_params=_compiler_params(2),
        name="out_proj",
    )(x, ys, ya, p, w_out[:SSM_WIDTH].astype(BF16), w_out[SSM_WIDTH:].astype(BF16),
      w_gate.astype(BF16), w_proj.astype(BF16))


def kernel(x, p, positions, norm_w, w_in, ssm_lambda_re, ssm_lambda_im, ssm_log_dt, ssm_b_re, ssm_b_im, ssm_c_re, ssm_c_im, ssm_d, glu_w, glu_b, q_norm_w, k_norm_w, lambda_q1, lambda_k1, lambda_q2, lambda_k2, subln_w, w_out, ple_w_proj, ple_w_gate):
    depth = norm_w.shape[0]
    for i in range(depth):
        lambda_init = 0.8 - 0.6 * math.exp(-0.3 * i)
        ab, ca, cl, bbc, al = _ssm_params(ssm_lambda_re[i], ssm_lambda_im[i], ssm_log_dt[i],
                                          ssm_b_re[i], ssm_b_im[i], ssm_c_re[i], ssm_c_im[i])
        u, zs, qt, k, vt, zat = _in_proj(x, positions, norm_w[i], w_in[i], q_norm_w[i], k_norm_w[i])
        ys = _ssm_branch(u, zs, ab, ca, cl, bbc, al, ssm_d[i], glu_w[i], glu_b[i])
        ya = _attention(qt, k, vt, zat, lambda_q1[i], lambda_k1[i], lambda_q2[i], lambda_k2[i],
                        subln_w[i], lambda_init)
        x = _out_proj(x, ys, ya, p[i], w_out[i], ple_w_proj[i], ple_w_gate[i])
    return x
```

```python
import functools
import math

import jax
import jax.numpy as jnp
import numpy as np
from jax import lax
from jax.experimental import pallas as pl
from jax.experimental.pallas import tpu as pltpu

F32 = jnp.float32
BF16 = jnp.bfloat16

D_MODEL = 1024
PLE_DIM = 256
SSM_WIDTH = 512
SSM_GROUP = 16
SSM_GROUPS = 32
SSM_STATE = 64
ATTN_WIDTH = 512
N_HEADS = 4
HEAD_DIM = 64
V_HEAD_DIM = 128
ROT_DIM = 16
ROT_HALF = 8
ROPE_THETA = 500000.0
EPS = 1e-6
LOG2E = math.log2(math.e)

V7X_LANES = 128
V7X_SUBLANES = 8
V7X_VMEM_BYTES = 64 * 1024 * 1024
VMEM_LIMIT_BYTES = V7X_VMEM_BYTES * 7 // 8

TOKEN_TILE = 1024
ATTN_TILE = 512
OUT_TOKEN_TILE = 1024
SSM_FOLD = 4
SSM_STEPS = 128
SSM_PART_STEPS = 64
SSM_PERM_STEPS = 32
GROUPS_PER_BLOCK = 8
N_SSM_BLOCKS = SSM_GROUPS // GROUPS_PER_BLOCK
BLOCK_STATE = GROUPS_PER_BLOCK * SSM_STATE
MASK_VALUE = -1e30
SUM_ROWS = 16


def _gelu_tanh(x):
    c0 = math.sqrt(2.0 / math.pi)
    inner = x * (c0 + (c0 * 0.044715) * (x * x))
    half = 0.5 * x
    return half + half * jnp.tanh(inner)


def _compiler_params(n_grid_axes):
    return pltpu.CompilerParams(
        dimension_semantics=("arbitrary",) * n_grid_axes,
        vmem_limit_bytes=VMEM_LIMIT_BYTES,
    )


def _ssm_params_kernel(lr_ref, li_ref, logdt_ref, br_ref, bi_ref, cr_ref, ci_ref,
                       ab_ref, ca_ref, cl_ref, bbc_ref, al_ref):
    fold = ab_ref.shape[0]
    n = lr_ref.shape[-1]
    lr = lr_ref[...]
    li = li_ref[...]
    dt = jnp.exp(logdt_ref[...])
    mag = jnp.exp(lr * dt)
    a_re = mag * jnp.cos(li * dt)
    a_im = mag * jnp.sin(li * dt)
    nr = a_re - 1.0
    ni = a_im
    den = lr * lr + li * li
    coef_re = (nr * lr + ni * li) / den
    coef_im = (ni * lr - nr * li) / den
    br = br_ref[...]
    bi = bi_ref[...]
    bb_re = coef_re * br - coef_im * bi
    bb_im = coef_re * bi + coef_im * br
    c_re = cr_ref[...]
    c_im = ci_ref[...]

    def cmul(xr, xi, yr, yi):
        return xr * yr - xi * yi, xr * yi + xi * yr

    powers = [(jnp.ones_like(a_re), jnp.zeros_like(a_im))]
    for _ in range(fold):
        powers.append(cmul(*powers[-1], a_re, a_im))
    for i in range(fold):
        ab_ref[i, 0], ab_ref[i, 1] = cmul(*powers[fold - 1 - i], bb_re, bb_im)
        ca_re, ca_im = cmul(c_re, c_im, *powers[i + 1])
        ca_ref[i, 0] = ca_re
        ca_ref[i, 1] = -ca_im
        cl_re, cl_im = cmul(c_re, c_im, *powers[i])
        cl_ref[i, :, :, :n] = cl_re
        cl_ref[i, :, :, n:] = -cl_im
    bbc_ref[:, :, :n] = bb_re
    bbc_ref[:, :, n:] = bb_im
    al_ref[0] = powers[fold][0]
    al_ref[1] = powers[fold][1]


def _ssm_params(lam_re, lam_im, log_dt, b_re, b_im, c_re, c_im):
    g, n, p, fold = SSM_GROUPS, SSM_STATE, SSM_GROUP, SSM_FOLD
    per_offset = jax.ShapeDtypeStruct((fold, 2, g, p, n), F32)
    return pl.pallas_call(
        _ssm_params_kernel,
        out_shape=(per_offset, per_offset,
                   jax.ShapeDtypeStruct((fold, g, p, 2 * n), F32),
                   jax.ShapeDtypeStruct((g, p, 2 * n), F32),
                   jax.ShapeDtypeStruct((2, g, 1, n), F32)),
        name="ssm_params",
    )(lam_re.reshape(g, 1, n), lam_im.reshape(g, 1, n), log_dt.reshape(g, 1, 1),
      jnp.swapaxes(b_re, 1, 2), jnp.swapaxes(b_im, 1, 2), c_re, c_im)


def _in_proj_kernel(x_ref, nw_ref, w_ref, pos_ref, invf_ref, qnw_ref, knw_ref,
                    u_ref, zs_ref, qt_ref, k_ref, vt_ref, zat_ref, wnat_ref, wtr_ref):
    @pl.when((pl.program_id(0) == 0) & (pl.program_id(1) == 0))
    def _():
        n_nat = wnat_ref.shape[1]
        wnat_ref[...] = w_ref[:, :n_nat].astype(BF16)
        chunk = wtr_ref.shape[1] // 2
        for c0 in range(0, wtr_ref.shape[0], chunk):
            wtr_ref[c0:c0 + chunk, :] = w_ref[:, n_nat + c0:n_nat + c0 + chunk].T.astype(BF16)

    x = x_ref[...]
    ms = jnp.mean(x * x, axis=-1, keepdims=True)
    h = (x * lax.rsqrt(ms + EPS) * nw_ref[...]).astype(BF16)
    nat = jnp.dot(h, wnat_ref[...], preferred_element_type=F32)
    u_ref[...] = nat[:, :SSM_WIDTH].astype(BF16)
    zs_ref[...] = nat[:, SSM_WIDTH:].astype(BF16)
    tr = lax.dot_general(wtr_ref[...], h, (((1,), (1,)), ((), ())),
                         preferred_element_type=F32)
    tm = x.shape[0]
    ang = pos_ref[...].astype(F32) * invf_ref[...]
    cos = jnp.cos(ang)
    sin = jnp.sin(ang)

    def norm_rope(t, w_ref):
        t3 = t.reshape(2 * N_HEADS, HEAD_DIM, tm)
        ms3 = jnp.mean(t3 * t3, axis=1, keepdims=True)
        t3 = t3 * lax.rsqrt(ms3 + EPS) * w_ref[...]
        t1 = t3[:, :ROT_HALF, :]
        t2 = t3[:, ROT_HALF:ROT_DIM, :]
        r1 = t1 * cos - t2 * sin
        r2 = t2 * cos + t1 * sin
        t3 = jnp.concatenate([r1, r2, t3[:, ROT_DIM:, :]], axis=1)
        return t3.reshape(ATTN_WIDTH, tm)

    def store_head_major(ref, t):
        t = t.reshape(N_HEADS, V_HEAD_DIM, tm).astype(BF16)
        ta = ref.shape[-1]
        for j in range(tm // ta):
            ref[:, j] = t[:, :, j * ta:(j + 1) * ta]

    q = norm_rope(tr[0:ATTN_WIDTH], qnw_ref) * (HEAD_DIM ** -0.5 * LOG2E)
    store_head_major(qt_ref, q)
    k = norm_rope(tr[ATTN_WIDTH:2 * ATTN_WIDTH], knw_ref)
    k_ref[...] = k.T.astype(BF16)
    store_head_major(vt_ref, tr[2 * ATTN_WIDTH:3 * ATTN_WIDTH])
    store_head_major(zat_ref, tr[3 * ATTN_WIDTH:])


def _in_proj(x, positions, norm_w, w_in, q_norm_w, k_norm_w):
    b, s, d = x.shape
    tm, ta = TOKEN_TILE, ATTN_TILE
    nt, tiles_per_step = s // ta, tm // ta
    inv_freq = ROPE_THETA ** (-jnp.arange(0, ROT_DIM, 2, dtype=F32) / ROT_DIM)
    head_major = jax.ShapeDtypeStruct((b, N_HEADS, nt, V_HEAD_DIM, ta), BF16)
    head_spec = pl.BlockSpec((None, N_HEADS, tiles_per_step, V_HEAD_DIM, ta), lambda bi, ti: (bi, 0, ti, 0, 0))
    ssm_shape = jax.ShapeDtypeStruct((b, s, SSM_WIDTH), BF16)
    ssm_spec = pl.BlockSpec((None, tm, SSM_WIDTH), lambda bi, ti: (bi, ti, 0))
    const2 = lambda bi, ti: (0, 0)
    const3 = lambda bi, ti: (0, 0, 0)
    return pl.pallas_call(
        _in_proj_kernel,
        grid=(b, s // tm),
        in_specs=[
            pl.BlockSpec((None, tm, d), lambda bi, ti: (bi, ti, 0)),
            pl.BlockSpec((1, d), const2),
            pl.BlockSpec(w_in.shape, const2),
            pl.BlockSpec((None, 1, tm), lambda bi, ti: (bi, 0, ti)),
            pl.BlockSpec((ROT_HALF, 1), const2),
            pl.BlockSpec((1, HEAD_DIM, 1), const3),
            pl.BlockSpec((1, HEAD_DIM, 1), const3),
        ],
        out_specs=[
            ssm_spec, ssm_spec,
            head_spec,
            pl.BlockSpec((None, tm, ATTN_WIDTH), lambda bi, ti: (bi, ti, 0)),
            head_spec, head_spec,
        ],
        out_shape=(ssm_shape, ssm_shape, head_major,
                   jax.ShapeDtypeStruct((b, s, ATTN_WIDTH), BF16), head_major, head_major),
        scratch_shapes=[
            pltpu.VMEM((d, 2 * SSM_WIDTH), BF16),
            pltpu.VMEM((4 * ATTN_WIDTH, d), BF16),
        ],
        compiler_params=_compiler_params(2),
        name="in_proj",
    )(x, norm_w.reshape(1, d), w_in, positions.reshape(b, 1, s),
      inv_freq.reshape(ROT_HALF, 1), q_norm_w.reshape(1, HEAD_DIM, 1), k_norm_w.reshape(1, HEAD_DIM, 1))


def _ssm_kernel(u_ref, zs_ref, pin_ref, pout_ref, ab_ref, ca_ref, cd_ref, ar_ref, ai_ref, d_ref,
                gw_ref, gb_ref, y_ref, uf_ref, bu_ref, xs_ref, y2_ref, yg_ref, state_ref,
                bw_ref, cs_ref, tmp_ref):
    n_batch, steps, width = u_ref.shape
    fold, perm_out_rows, perm_in_rows = pin_ref.shape
    perm_steps = perm_in_rows // n_batch
    part_steps = SSM_PART_STEPS
    perms_per_part = part_steps // perm_steps
    part_rows = perms_per_part * perm_out_rows
    n_parts = steps // part_steps
    blk = 2 * BLOCK_STATE
    fw = fold * V7X_LANES
    pack_rows = 2 * V7X_SUBLANES

    @pl.when(pl.program_id(0) == 0)
    def _():
        state_ref[...] = jnp.zeros_like(state_ref)
        bw_ref[...] = jnp.zeros_like(bw_ref)
        for m in range(N_SSM_BLOCKS):
            tmp_ref[...] = jnp.zeros_like(tmp_ref)
            for i in range(fold):
                for gl in range(GROUPS_PER_BLOCK):
                    g = m * GROUPS_PER_BLOCK + gl
                    in_rows = slice(i * V7X_LANES + gl * SSM_GROUP, i * V7X_LANES + (gl + 1) * SSM_GROUP)
                    for ri in range(2):
                        st_cols = slice(ri * BLOCK_STATE + gl * SSM_STATE, ri * BLOCK_STATE + (gl + 1) * SSM_STATE)
                        bw_ref[m, in_rows, st_cols] = ab_ref[i, ri, g].astype(BF16)
                        tmp_ref[in_rows, st_cols] = ca_ref[i, ri, g]
            cs_ref[m] = tmp_ref[...].T.astype(BF16)

    def rows_of(part):
        return slice(part * part_rows, (part + 1) * part_rows)

    def fold_in(part):
        for h in range(perms_per_part):
            t0 = part * part_steps + h * perm_steps
            u_bt = u_ref[:, t0:t0 + perm_steps, :].reshape(perm_in_rows, width)
            r0 = part * part_rows + h * perm_out_rows
            for i in range(fold):
                sel = jnp.dot(pin_ref[i], u_bt, preferred_element_type=F32).astype(BF16)
                for m in range(N_SSM_BLOCKS):
                    uf_ref[r0:r0 + perm_out_rows, m * fw + i * V7X_LANES:m * fw + (i + 1) * V7X_LANES] = (
                        sel[:, m * V7X_LANES:(m + 1) * V7X_LANES])

    def state_block(part, m):
        rows = rows_of(part)
        re = slice(m * blk, m * blk + BLOCK_STATE)
        im = slice(m * blk + BLOCK_STATE, (m + 1) * blk)
        bu_ref[rows, m * blk:(m + 1) * blk] = jnp.dot(
            uf_ref[rows, m * fw:(m + 1) * fw], bw_ref[m], preferred_element_type=F32)
        a_re = jnp.broadcast_to(ar_ref[:, m * BLOCK_STATE:(m + 1) * BLOCK_STATE], (V7X_SUBLANES, BLOCK_STATE))
        a_im = jnp.broadcast_to(ai_ref[:, m * BLOCK_STATE:(m + 1) * BLOCK_STATE], (V7X_SUBLANES, BLOCK_STATE))
        x_re = state_ref[:, re]
        x_im = state_ref[:, im]
        for r0 in range(part * part_rows, (part + 1) * part_rows, pack_rows):
            ins_re, ins_im = [], []
            for half in range(2):
                rr = slice(r0 + half * V7X_SUBLANES, r0 + (half + 1) * V7X_SUBLANES)
                ins_re.append(x_re)
                ins_im.append(x_im)
                n_re = a_re * x_re - a_im * x_im + bu_ref[rr, re]
                n_im = a_re * x_im + a_im * x_re + bu_ref[rr, im]
                x_re, x_im = n_re, n_im
            xs_ref[r0:r0 + pack_rows, re] = jnp.concatenate(ins_re, axis=0).astype(BF16)
            xs_ref[r0:r0 + pack_rows, im] = jnp.concatenate(ins_im, axis=0).astype(BF16)
        state_ref[:, re] = x_re
        state_ref[:, im] = x_im
        y2_ref[rows, m * fw:(m + 1) * fw] = (
            jnp.dot(xs_ref[rows, m * blk:(m + 1) * blk], cs_ref[m], preferred_element_type=F32)
            + jnp.dot(uf_ref[rows, m * fw:(m + 1) * fw], cd_ref[m], preferred_element_type=F32))

    def gate_offset(part, i):
        rows = rows_of(part)
        cols = [slice(m * fw + i * V7X_LANES, m * fw + (i + 1) * V7X_LANES) for m in range(N_SSM_BLOCKS)]
        y = jnp.concatenate([y2_ref[rows, c] for c in cols], axis=-1)
        u_i = jnp.concatenate([uf_ref[rows, c] for c in cols], axis=-1).astype(F32)
        y = _gelu_tanh(y + d_ref[...] * u_i)
        gate = jnp.dot(y.astype(BF16), gw_ref[...], preferred_element_type=F32) + gb_ref[...]
        y = (y * jax.nn.sigmoid(gate)).astype(BF16)
        for h in range(perms_per_part):
            yg_ref[part * perms_per_part + h, i * perm_out_rows:(i + 1) * perm_out_rows, :] = (
                y[h * perm_out_rows:(h + 1) * perm_out_rows])

    def unfold_out(part):
        for h in range(perms_per_part):
            t0 = part * part_steps + h * perm_steps
            y_bt = jnp.dot(pout_ref[...], yg_ref[part * perms_per_part + h], preferred_element_type=F32)
            z = zs_ref[:, t0:t0 + perm_steps, :].reshape(perm_in_rows, width).astype(F32)
            y_ref[:, t0:t0 + perm_steps, :] = (
                (y_bt * jax.nn.silu(z)).astype(BF16).reshape(n_batch, perm_steps, width))

    assert fold == N_SSM_BLOCKS
    for part in range(n_parts + 1):
        if part < n_parts:
            fold_in(part)
        for j in range(N_SSM_BLOCKS):
            if part < n_parts:
                state_block(part, j)
            if part >= 1:
                gate_offset(part - 1, j)
        if part >= 1:
            unfold_out(part - 1)


def _ssm_direct_weights(cl, bbc):
    fold, gpb, nb, p = SSM_FOLD, GROUPS_PER_BLOCK, N_SSM_BLOCKS, SSM_GROUP
    in_lanes = fold * gpb * p
    lags = jnp.einsum('lgpn,gqn->lgqp', cl, bbc, precision=lax.Precision.HIGHEST)
    zero = jnp.zeros_like(lags[0])
    toep = jnp.stack([jnp.stack([lags[i - k] if i >= k else zero for i in range(fold)]) for k in range(fold)])
    toep = toep.reshape(fold, fold, nb, gpb, p, p).transpose(2, 0, 3, 4, 1, 5).reshape(nb * in_lanes, fold * p)
    col = np.arange(in_lanes)
    spread = (np.arange(fold * p)[:, None] == ((col // (gpb * p)) * p + col % p)[None, :]).astype(np.float32)
    in_group = (col // p) % gpb
    same_group = (in_group[:, None] == in_group[None, :]).astype(np.float32)
    cd = jnp.dot(toep, spread, precision=lax.Precision.HIGHEST).reshape(nb, in_lanes, in_lanes)
    return (cd * same_group).astype(BF16)


def _ssm_branch(u, zs, ab, ca, cl, bbc, al, d_skip, glu_w, glu_b):
    n_batch, s, _ = u.shape
    assert n_batch == V7X_SUBLANES, "one time group of all batches must fill one sublane tile"
    fold, steps, perm_steps = SSM_FOLD, SSM_STEPS, SSM_PERM_STEPS
    g_rows = (perm_steps // fold) * n_batch
    n_rows = perm_steps * n_batch
    r = np.arange(g_rows)
    src = (r % n_batch) * perm_steps + fold * (r // n_batch)
    pin = np.arange(n_rows)[None, None, :] == (src[None, :, None] + np.arange(fold)[:, None, None])
    pout = jnp.asarray(pin.reshape(fold * g_rows, n_rows).T, dtype=BF16)
    pin = jnp.asarray(pin, dtype=BF16)
    cd = _ssm_direct_weights(cl, bbc)
    row_spec = pl.BlockSpec((n_batch, steps, SSM_WIDTH), lambda i: (0, i, 0))
    const2 = lambda i: (0, 0)
    const3 = lambda i: (0, 0, 0)
    n_state = SSM_GROUPS * SSM_STATE
    f_rows = (steps // fold) * n_batch
    fw_all = fold * SSM_WIDTH
    return pl.pallas_call(
        _ssm_kernel,
        grid=(s // steps,),
        in_specs=[
            row_spec, row_spec,
            pl.BlockSpec(pin.shape, const3),
            pl.BlockSpec(pout.shape, const2),
            pl.BlockSpec(ab.shape, lambda i: (0, 0, 0, 0, 0)),
            pl.BlockSpec(ca.shape, lambda i: (0, 0, 0, 0, 0)),
            pl.BlockSpec(cd.shape, const3),
            pl.BlockSpec((1, n_state), const2),
            pl.BlockSpec((1, n_state), const2),
            pl.BlockSpec((1, SSM_WIDTH), const2),
            pl.BlockSpec((SSM_WIDTH, SSM_WIDTH), const2),
            pl.BlockSpec((1, SSM_WIDTH), const2),
        ],
        out_specs=row_spec,
        out_shape=jax.ShapeDtypeStruct((n_batch, s, SSM_WIDTH), BF16),
        scratch_shapes=[
            pltpu.VMEM((f_rows, fw_all), BF16),
            pltpu.VMEM((f_rows, 2 * n_state), F32),
            pltpu.VMEM((f_rows, 2 * n_state), BF16),
            pltpu.VMEM((f_rows, fw_all), F32),
            pltpu.VMEM((steps // perm_steps, fold * g_rows, SSM_WIDTH), BF16),
            pltpu.VMEM((V7X_SUBLANES, 2 * n_state), F32),
            pltpu.VMEM((N_SSM_BLOCKS, fold * V7X_LANES, 2 * BLOCK_STATE), BF16),
            pltpu.VMEM((N_SSM_BLOCKS, 2 * BLOCK_STATE, fold * V7X_LANES), BF16),
            pltpu.VMEM((fold * V7X_LANES, 2 * BLOCK_STATE), F32),
        ],
        compiler_params=_compiler_params(1),
        name="ssm_branch",
    )(u, zs, pin, pout, ab, ca, cd, al[0].reshape(1, n_state), al[1].reshape(1, n_state),
      d_skip.reshape(1, SSM_WIDTH), glu_w.astype(BF16), glu_b.reshape(1, SSM_WIDTH))


def _attn_kernel(lambda_init, qt_ref, k_ref, vt_ref, zat_ref, lq1_ref, lk1_ref, lq2_ref, lk2_ref,
                 subw_ref, y_ref, wq_ref, s_ref, mx_ref, acc_ref, m_ref):
    n_tiles, _, tq = qt_ref.shape
    tk = tq
    diag_slot = 2

    def load_queries(q):
        wq_ref[0, :HEAD_DIM] = qt_ref[q, :HEAD_DIM]
        wq_ref[1, HEAD_DIM:] = qt_ref[q, HEAD_DIM:]

    def reset_stats():
        m_ref[...] = jnp.full_like(m_ref, MASK_VALUE)
        acc_ref[...] = jnp.zeros_like(acc_ref)

    def scores(j, slot):
        kb = k_ref[pl.ds(pl.multiple_of(j * tk, tk), tk), :]
        for c in range(2):
            s = jnp.dot(kb, wq_ref[c], preferred_element_type=F32)
            s_ref[slot, c] = s
            mx_ref[slot, c] = jnp.max(s, axis=0, keepdims=True)

    def diagonal_scores(q, slot):
        h = tk // 2
        row0 = pl.multiple_of(q * tk, tk)
        k_lo = k_ref[pl.ds(row0, h), :]
        k_hi = k_ref[pl.ds(row0 + h, h), :]
        causal = (lax.broadcasted_iota(jnp.int32, (h, h), 0) <= lax.broadcasted_iota(jnp.int32, (h, h), 1))
        for c in range(2):
            s_lo = jnp.dot(k_lo, wq_ref[c], preferred_element_type=F32)
            s_hi = jnp.dot(k_hi, wq_ref[c, :, h:], preferred_element_type=F32)
            s_ll = jnp.where(causal, s_lo[:, :h], MASK_VALUE)
            s_hh = jnp.where(causal, s_hi, MASK_VALUE)
            s_ref[slot, c, :h, :h] = s_ll
            s_ref[slot, c, :h, h:] = s_lo[:, h:]
            s_ref[slot, c, h:, :h] = jnp.full((h, h), MASK_VALUE, F32)
            s_ref[slot, c, h:, h:] = s_hh
            mx_ref[slot, c, :, :h] = jnp.max(s_ll, axis=0, keepdims=True)
            mx_ref[slot, c, :, h:] = jnp.maximum(jnp.max(s_lo[:, h:], axis=0, keepdims=True),
                                                 jnp.max(s_hh, axis=0, keepdims=True))

    def accumulate(j, slot):
        vb = jnp.concatenate([vt_ref[j], jnp.ones((SUM_ROWS, tk), BF16)], axis=0)
        for c in range(2):
            m_old = m_ref[c]
            m_new = jnp.maximum(m_old, mx_ref[slot, c])
            alpha = jnp.exp2(m_old - m_new)
            p = jnp.exp2(s_ref[slot, c] - m_new)
            acc_ref[c] = alpha * acc_ref[c] + jnp.dot(vb, p.astype(BF16), preferred_element_type=F32)
            m_ref[c] = m_new

    def finalize(q):
        lam = (jnp.exp(jnp.sum(lq1_ref[...] * lk1_ref[...], axis=-1, keepdims=True))
               - jnp.exp(jnp.sum(lq2_ref[...] * lk2_ref[...], axis=-1, keepdims=True)) + lambda_init)
        l0 = acc_ref[0, V_HEAD_DIM:V_HEAD_DIM + 1]
        l1 = acc_ref[1, V_HEAD_DIM:V_HEAD_DIM + 1]
        out = acc_ref[0, :V_HEAD_DIM] * (1.0 / l0) - acc_ref[1, :V_HEAD_DIM] * (lam / l1)
        ms = jnp.mean(out * out, axis=0, keepdims=True)
        out = out * (lax.rsqrt(ms + EPS) * (1.0 - lambda_init)) * subw_ref[...]
        out = out * jax.nn.silu(zat_ref[q].astype(F32))
        y_ref[pl.ds(pl.multiple_of(q * tq, tq), tq), :] = out.T.astype(BF16)

    zeros = jnp.zeros((HEAD_DIM, tq), BF16)
    wq_ref[0, HEAD_DIM:] = zeros
    wq_ref[1, :HEAD_DIM] = zeros
    load_queries(0)
    reset_stats()
    diagonal_scores(0, 1)

    def enter_tile(q, pending_slot):
        load_queries(q)
        diagonal_scores(q, diag_slot)
        accumulate(jnp.maximum(q - 2, 0), pending_slot)
        finalize(q - 1)
        reset_stats()
        scores(0, 0)
        accumulate(q, diag_slot)

    def pair(i, c):
        j = 2 * i
        scores(j + 1, 1)
        accumulate(j, 0)
        scores(j + 2, 0)
        accumulate(j + 1, 1)
        return c

    def two_tiles(k, carry):
        q_odd = 2 * k + 1
        enter_tile(q_odd, 1)
        lax.fori_loop(0, k, pair, 0)
        q_even = q_odd + 1
        enter_tile(q_even, 0)
        lax.fori_loop(0, k, pair, 0)
        scores(q_even - 1, 1)
        accumulate(q_even - 2, 0)
        return carry

    assert n_tiles % 2 == 0
    lax.fori_loop(0, (n_tiles - 2) // 2, two_tiles, 0)
    q_last = n_tiles - 1
    enter_tile(q_last, 1)
    lax.fori_loop(0, (q_last - 1) // 2, pair, 0)
    accumulate(q_last - 1, 0)
    finalize(q_last)


def _attention(qt, k, vt, zat, lq1, lk1, lq2, lk2, subln_w, lambda_init):
    b, _, nt, _, tq = qt.shape
    s = nt * tq
    assert nt >= 2
    row64 = pl.BlockSpec((1, HEAD_DIM), lambda bi, hi: (0, 0))
    head_spec = pl.BlockSpec((None, None, nt, V_HEAD_DIM, tq), lambda bi, hi: (bi, hi, 0, 0, 0))
    token_spec = pl.BlockSpec((None, s, V_HEAD_DIM), lambda bi, hi: (bi, 0, hi))
    return pl.pallas_call(
        functools.partial(_attn_kernel, lambda_init),
        grid=(b, N_HEADS),
        in_specs=[
            head_spec, token_spec, head_spec, head_spec,
            row64, row64, row64, row64,
            pl.BlockSpec((V_HEAD_DIM, 1), lambda bi, hi: (0, 0)),
        ],
        out_specs=token_spec,
        out_shape=jax.ShapeDtypeStruct((b, s, ATTN_WIDTH), BF16),
        scratch_shapes=[
            pltpu.VMEM((2, 2 * HEAD_DIM, tq), BF16),
            pltpu.VMEM((3, 2, tq, tq), F32),
            pltpu.VMEM((3, 2, 1, tq), F32),
            pltpu.VMEM((2, V_HEAD_DIM + SUM_ROWS, tq), F32),
            pltpu.VMEM((2, 1, tq), F32),
        ],
        compiler_params=_compiler_params(2),
        name="diff_attention",
    )(qt, k, vt, zat, lq1.reshape(1, HEAD_DIM), lk1.reshape(1, HEAD_DIM),
      lq2.reshape(1, HEAD_DIM), lk2.reshape(1, HEAD_DIM), subln_w.reshape(V_HEAD_DIM, 1))


def _out_kernel(x_ref, ys_ref, ya_ref, p_ref, wos_ref, woa_ref, wg_ref, wp_ref, o_ref):
    x2 = (x_ref[...]
          + jnp.dot(ys_ref[...], wos_ref[...], preferred_element_type=F32)
          + jnp.dot(ya_ref[...], woa_ref[...], preferred_element_type=F32))
    gate = jax.nn.sigmoid(jnp.dot(x2.astype(BF16), wg_ref[...], preferred_element_type=F32))
    ple = jnp.dot(p_ref[...].astype(BF16), wp_ref[...], preferred_element_type=F32)
    o_ref[...] = x2 + gate * ple


def _out_proj(x, ys, ya, p, w_out, w_proj, w_gate):
    b, s, d = x.shape
    tm = OUT_TOKEN_TILE
    const2 = lambda bi, ti: (0, 0)
    return pl.pallas_call(
        _out_kernel,
        grid=(b, s // tm),
        in_specs=[
            pl.BlockSpec((None, tm, d), lambda bi, ti: (bi, ti, 0)),
            pl.BlockSpec((None, tm, SSM_WIDTH), lambda bi, ti: (bi, ti, 0)),
            pl.BlockSpec((None, tm, ATTN_WIDTH), lambda bi, ti: (bi, ti, 0)),
            pl.BlockSpec((None, tm, PLE_DIM), lambda bi, ti: (bi, ti, 0)),
            pl.BlockSpec((SSM_WIDTH, d), const2),
            pl.BlockSpec((ATTN_WIDTH, d), const2),
            pl.BlockSpec((d, d), const2),
            pl.BlockSpec((PLE_DIM, d), const2),
        ],
        out_specs=pl.BlockSpec((None, tm, d), lambda bi, ti: (bi, ti, 0)),
        out_shape=jax.ShapeDtypeStruct((b, s, d), F32),
        compiler_params=_compiler_params(2),
        name="out_proj",
    )(x, ys, ya, p, w_out[:SSM_WIDTH].astype(BF16), w_out[SSM_WIDTH:].astype(BF16),
      w_gate.astype(BF16), w_proj.astype(BF16))


def kernel(x, p, positions, norm_w, w_in, ssm_lambda_re, ssm_lambda_im, ssm_log_dt, ssm_b_re, ssm_b_im, ssm_c_re, ssm_c_im, ssm_d, glu_w, glu_b, q_norm_w, k_norm_w, lambda_q1, lambda_k1, lambda_q2, lambda_k2, subln_w, w_out, ple_w_proj, ple_w_gate):
    depth = norm_w.shape[0]
    for i in range(depth):
        lambda_init = 0.8 - 0.6 * math.exp(-0.3 * i)
        ab, ca, cl, bbc, al = _ssm_params(ssm_lambda_re[i], ssm_lambda_im[i], ssm_log_dt[i],
                                          ssm_b_re[i], ssm_b_im[i], ssm_c_re[i], ssm_c_im[i])
        u, zs, qt, k, vt, zat = _in_proj(x, positions, norm_w[i], w_in[i], q_norm_w[i], k_norm_w[i])
        ys = _ssm_branch(u, zs, ab, ca, cl, bbc, al, ssm_d[i], glu_w[i], glu_b[i])
        ya = _attention(qt, k, vt, zat, lambda_q1[i], lambda_k1[i], lambda_q2[i], lambda_k2[i],
                        subln_w[i], lambda_init)
        x = _out_proj(x, ys, ya, p[i], w_out[i], ple_w_proj[i], ple_w_gate[i])
    return x
```

```python
import functools
import math

import jax
import jax.numpy as jnp
import numpy as np
from jax import lax
from jax.experimental import pallas as pl
from jax.experimental.pallas import tpu as pltpu

F32 = jnp.float32
BF16 = jnp.bfloat16

D_MODEL = 1024
PLE_DIM = 256
SSM_WIDTH = 512
SSM_GROUP = 16
SSM_GROUPS = 32
SSM_STATE = 64
ATTN_WIDTH = 512
N_HEADS = 4
HEAD_DIM = 64
V_HEAD_DIM = 128
ROT_DIM = 16
ROT_HALF = 8
ROPE_THETA = 500000.0
EPS = 1e-6
LOG2E = math.log2(math.e)

V7X_LANES = 128
V7X_SUBLANES = 8
V7X_VMEM_BYTES = 64 * 1024 * 1024
VMEM_LIMIT_BYTES = V7X_VMEM_BYTES * 7 // 8

TOKEN_TILE = 1024
ATTN_TILE = 512
OUT_TOKEN_TILE = 1024
SSM_FOLD = 4
SSM_STEPS = 128
SSM_PART_STEPS = 64
SSM_PERM_STEPS = 32
GROUPS_PER_BLOCK = 8
N_SSM_BLOCKS = SSM_GROUPS // GROUPS_PER_BLOCK
BLOCK_STATE = GROUPS_PER_BLOCK * SSM_STATE
MASK_VALUE = -1e30
SUM_ROWS = 16


def _gelu_tanh(x):
    c0 = -2.0 * math.sqrt(2.0 / math.pi)
    return x / (1.0 + jnp.exp(x * (c0 + (c0 * 0.044715) * (x * x))))


def _compiler_params(n_grid_axes):
    return pltpu.CompilerParams(
        dimension_semantics=("arbitrary",) * n_grid_axes,
        vmem_limit_bytes=VMEM_LIMIT_BYTES,
    )


def _ssm_params_kernel(lr_ref, li_ref, logdt_ref, br_ref, bi_ref, cr_ref, ci_ref,
                       ab_ref, ca_ref, cl_ref, bbc_ref, al_ref):
    fold = ab_ref.shape[0]
    n = lr_ref.shape[-1]
    lr = lr_ref[...]
    li = li_ref[...]
    dt = jnp.exp(logdt_ref[...])
    mag = jnp.exp(lr * dt)
    a_re = mag * jnp.cos(li * dt)
    a_im = mag * jnp.sin(li * dt)
    nr = a_re - 1.0
    ni = a_im
    den = lr * lr + li * li
    coef_re = (nr * lr + ni * li) / den
    coef_im = (ni * lr - nr * li) / den
    br = br_ref[...]
    bi = bi_ref[...]
    bb_re = coef_re * br - coef_im * bi
    bb_im = coef_re * bi + coef_im * br
    c_re = cr_ref[...]
    c_im = ci_ref[...]

    def cmul(xr, xi, yr, yi):
        return xr * yr - xi * yi, xr * yi + xi * yr

    powers = [(jnp.ones_like(a_re), jnp.zeros_like(a_im))]
    for _ in range(fold):
        powers.append(cmul(*powers[-1], a_re, a_im))
    for i in range(fold):
        ab_ref[i, 0], ab_ref[i, 1] = cmul(*powers[fold - 1 - i], bb_re, bb_im)
        ca_re, ca_im = cmul(c_re, c_im, *powers[i + 1])
        ca_ref[i, 0] = ca_re
        ca_ref[i, 1] = -ca_im
        cl_re, cl_im = cmul(c_re, c_im, *powers[i])
        cl_ref[i, :, :, :n] = cl_re
        cl_ref[i, :, :, n:] = -cl_im
    bbc_ref[:, :, :n] = bb_re
    bbc_ref[:, :, n:] = bb_im
    al_ref[0] = powers[fold][0]
    al_ref[1] = powers[fold][1]


def _ssm_params(lam_re, lam_im, log_dt, b_re, b_im, c_re, c_im):
    g, n, p, fold = SSM_GROUPS, SSM_STATE, SSM_GROUP, SSM_FOLD
    per_offset = jax.ShapeDtypeStruct((fold, 2, g, p, n), F32)
    return pl.pallas_call(
        _ssm_params_kernel,
        out_shape=(per_offset, per_offset,
                   jax.ShapeDtypeStruct((fold, g, p, 2 * n), F32),
                   jax.ShapeDtypeStruct((g, p, 2 * n), F32),
                   jax.ShapeDtypeStruct((2, g, 1, n), F32)),
        name="ssm_params",
    )(lam_re.reshape(g, 1, n), lam_im.reshape(g, 1, n), log_dt.reshape(g, 1, 1),
      jnp.swapaxes(b_re, 1, 2), jnp.swapaxes(b_im, 1, 2), c_re, c_im)


def _in_proj_kernel(x_ref, nw_ref, w_ref, pos_ref, invf_ref, qnw_ref, knw_ref,
                    u_ref, zs_ref, qt_ref, k_ref, vt_ref, zat_ref, wnat_ref, wtr_ref):
    @pl.when((pl.program_id(0) == 0) & (pl.program_id(1) == 0))
    def _():
        n_nat = wnat_ref.shape[1]
        wnat_ref[...] = w_ref[:, :n_nat].astype(BF16)
        chunk = wtr_ref.shape[1] // 2
        for c0 in range(0, wtr_ref.shape[0], chunk):
            wtr_ref[c0:c0 + chunk, :] = w_ref[:, n_nat + c0:n_nat + c0 + chunk].T.astype(BF16)

    x = x_ref[...]
    ms = jnp.mean(x * x, axis=-1, keepdims=True)
    h = (x * lax.rsqrt(ms + EPS) * nw_ref[...]).astype(BF16)
    nat = jnp.dot(h, wnat_ref[...], preferred_element_type=F32)
    u_ref[...] = nat[:, :SSM_WIDTH].astype(BF16)
    zs_ref[...] = nat[:, SSM_WIDTH:].astype(BF16)
    tr = lax.dot_general(wtr_ref[...], h, (((1,), (1,)), ((), ())),
                         preferred_element_type=F32)
    tm = x.shape[0]
    ang = pos_ref[...].astype(F32) * invf_ref[...]
    cos = jnp.cos(ang)
    sin = jnp.sin(ang)

    def norm_rope(t, w_ref):
        t3 = t.reshape(2 * N_HEADS, HEAD_DIM, tm)
        ms3 = jnp.mean(t3 * t3, axis=1, keepdims=True)
        t3 = t3 * lax.rsqrt(ms3 + EPS) * w_ref[...]
        t1 = t3[:, :ROT_HALF, :]
        t2 = t3[:, ROT_HALF:ROT_DIM, :]
        r1 = t1 * cos - t2 * sin
        r2 = t2 * cos + t1 * sin
        t3 = jnp.concatenate([r1, r2, t3[:, ROT_DIM:, :]], axis=1)
        return t3.reshape(ATTN_WIDTH, tm)

    def store_head_major(ref, t):
        t = t.reshape(N_HEADS, V_HEAD_DIM, tm).astype(BF16)
        ta = ref.shape[-1]
        for j in range(tm // ta):
            ref[:, j] = t[:, :, j * ta:(j + 1) * ta]

    q = norm_rope(tr[0:ATTN_WIDTH], qnw_ref) * (HEAD_DIM ** -0.5 * LOG2E)
    store_head_major(qt_ref, q)
    k = norm_rope(tr[ATTN_WIDTH:2 * ATTN_WIDTH], knw_ref)
    k_ref[...] = k.T.astype(BF16)
    store_head_major(vt_ref, tr[2 * ATTN_WIDTH:3 * ATTN_WIDTH])
    store_head_major(zat_ref, tr[3 * ATTN_WIDTH:])


def _in_proj(x, positions, norm_w, w_in, q_norm_w, k_norm_w):
    b, s, d = x.shape
    tm, ta = TOKEN_TILE, ATTN_TILE
    nt, tiles_per_step = s // ta, tm // ta
    inv_freq = ROPE_THETA ** (-jnp.arange(0, ROT_DIM, 2, dtype=F32) / ROT_DIM)
    head_major = jax.ShapeDtypeStruct((b, N_HEADS, nt, V_HEAD_DIM, ta), BF16)
    head_spec = pl.BlockSpec((None, N_HEADS, tiles_per_step, V_HEAD_DIM, ta), lambda bi, ti: (bi, 0, ti, 0, 0))
    ssm_shape = jax.ShapeDtypeStruct((b, s, SSM_WIDTH), BF16)
    ssm_spec = pl.BlockSpec((None, tm, SSM_WIDTH), lambda bi, ti: (bi, ti, 0))
    const2 = lambda bi, ti: (0, 0)
    const3 = lambda bi, ti: (0, 0, 0)
    return pl.pallas_call(
        _in_proj_kernel,
        grid=(b, s // tm),
        in_specs=[
            pl.BlockSpec((None, tm, d), lambda bi, ti: (bi, ti, 0)),
            pl.BlockSpec((1, d), const2),
            pl.BlockSpec(w_in.shape, const2),
            pl.BlockSpec((None, 1, tm), lambda bi, ti: (bi, 0, ti)),
            pl.BlockSpec((ROT_HALF, 1), const2),
            pl.BlockSpec((1, HEAD_DIM, 1), const3),
            pl.BlockSpec((1, HEAD_DIM, 1), const3),
        ],
        out_specs=[
            ssm_spec, ssm_spec,
            head_spec,
            pl.BlockSpec((None, tm, ATTN_WIDTH), lambda bi, ti: (bi, ti, 0)),
            head_spec, head_spec,
        ],
        out_shape=(ssm_shape, ssm_shape, head_major,
                   jax.ShapeDtypeStruct((b, s, ATTN_WIDTH), BF16), head_major, head_major),
        scratch_shapes=[
            pltpu.VMEM((d, 2 * SSM_WIDTH), BF16),
            pltpu.VMEM((4 * ATTN_WIDTH, d), BF16),
        ],
        compiler_params=_compiler_params(2),
        name="in_proj",
    )(x, norm_w.reshape(1, d), w_in, positions.reshape(b, 1, s),
      inv_freq.reshape(ROT_HALF, 1), q_norm_w.reshape(1, HEAD_DIM, 1), k_norm_w.reshape(1, HEAD_DIM, 1))


def _ssm_kernel(u_ref, zs_ref, pin_ref, pout_ref, ab_ref, ca_ref, cd_ref, ar_ref, ai_ref,
                gw_ref, gb_ref, y_ref, xu_ref, bu_ref, y2_ref, yg_ref, state_ref,
                bw_ref, cw_ref, tmp_ref):
    n_batch, steps, width = u_ref.shape
    fold, perm_out_rows, perm_in_rows = pin_ref.shape
    perm_steps = perm_in_rows // n_batch
    part_steps = SSM_PART_STEPS
    perms_per_part = part_steps // perm_steps
    part_rows = perms_per_part * perm_out_rows
    n_parts = steps // part_steps
    blk = 2 * BLOCK_STATE
    fw = fold * V7X_LANES
    xu = blk + fw
    pack_rows = 2 * V7X_SUBLANES

    @pl.when(pl.program_id(0) == 0)
    def _():
        state_ref[...] = jnp.zeros_like(state_ref)
        bw_ref[...] = jnp.zeros_like(bw_ref)
        for m in range(N_SSM_BLOCKS):
            tmp_ref[...] = jnp.zeros_like(tmp_ref)
            for i in range(fold):
                for gl in range(GROUPS_PER_BLOCK):
                    g = m * GROUPS_PER_BLOCK + gl
                    in_rows = slice(i * V7X_LANES + gl * SSM_GROUP, i * V7X_LANES + (gl + 1) * SSM_GROUP)
                    for ri in range(2):
                        st_cols = slice(ri * BLOCK_STATE + gl * SSM_STATE, ri * BLOCK_STATE + (gl + 1) * SSM_STATE)
                        bw_ref[m, in_rows, st_cols] = ab_ref[i, ri, g].astype(BF16)
                        tmp_ref[in_rows, st_cols] = ca_ref[i, ri, g]
            cw_ref[m, :blk, :] = tmp_ref[...].T.astype(BF16)
            cw_ref[m, blk:, :] = cd_ref[m]

    def rows_of(part):
        return slice(part * part_rows, (part + 1) * part_rows)

    def fold_in(part):
        for h in range(perms_per_part):
            t0 = part * part_steps + h * perm_steps
            u_bt = u_ref[:, t0:t0 + perm_steps, :].reshape(perm_in_rows, width)
            r0 = part * part_rows + h * perm_out_rows
            for i in range(fold):
                sel = jnp.dot(pin_ref[i], u_bt, preferred_element_type=F32).astype(BF16)
                for m in range(N_SSM_BLOCKS):
                    c0 = m * xu + blk + i * V7X_LANES
                    xu_ref[r0:r0 + perm_out_rows, c0:c0 + V7X_LANES] = sel[:, m * V7X_LANES:(m + 1) * V7X_LANES]

    def state_block(part, m):
        rows = rows_of(part)
        re = slice(m * blk, m * blk + BLOCK_STATE)
        im = slice(m * blk + BLOCK_STATE, (m + 1) * blk)
        xre = slice(m * xu, m * xu + BLOCK_STATE)
        xim = slice(m * xu + BLOCK_STATE, m * xu + blk)
        bu_ref[rows, m * blk:(m + 1) * blk] = jnp.dot(
            xu_ref[rows, m * xu + blk:(m + 1) * xu], bw_ref[m], preferred_element_type=F32)
        a_re = jnp.broadcast_to(ar_ref[:, m * BLOCK_STATE:(m + 1) * BLOCK_STATE], (V7X_SUBLANES, BLOCK_STATE))
        a_im = jnp.broadcast_to(ai_ref[:, m * BLOCK_STATE:(m + 1) * BLOCK_STATE], (V7X_SUBLANES, BLOCK_STATE))
        x_re = state_ref[:, re]
        x_im = state_ref[:, im]
        for r0 in range(part * part_rows, (part + 1) * part_rows, pack_rows):
            ins_re, ins_im = [], []
            for half in range(2):
                rr = slice(r0 + half * V7X_SUBLANES, r0 + (half + 1) * V7X_SUBLANES)
                ins_re.append(x_re)
                ins_im.append(x_im)
                n_re = a_re * x_re - a_im * x_im + bu_ref[rr, re]
                n_im = a_re * x_im + a_im * x_re + bu_ref[rr, im]
                x_re, x_im = n_re, n_im
            xu_ref[r0:r0 + pack_rows, xre] = jnp.concatenate(ins_re, axis=0).astype(BF16)
            xu_ref[r0:r0 + pack_rows, xim] = jnp.concatenate(ins_im, axis=0).astype(BF16)
        state_ref[:, re] = x_re
        state_ref[:, im] = x_im
        half = fw // 2
        y2_ref[rows, m * fw:m * fw + half] = jnp.dot(
            xu_ref[rows, m * xu:m * xu + blk + half], cw_ref[m, :blk + half, :half], preferred_element_type=F32)
        y2_ref[rows, m * fw + half:(m + 1) * fw] = jnp.dot(
            xu_ref[rows, m * xu:(m + 1) * xu], cw_ref[m, :, half:], preferred_element_type=F32)

    def gate_offset(part, i):
        rows = rows_of(part)
        cols = [slice(m * fw + i * V7X_LANES, m * fw + (i + 1) * V7X_LANES) for m in range(N_SSM_BLOCKS)]
        y = _gelu_tanh(jnp.concatenate([y2_ref[rows, c] for c in cols], axis=-1))
        gate = jnp.dot(y.astype(BF16), gw_ref[...], preferred_element_type=F32) + gb_ref[...]
        y = (y * jax.nn.sigmoid(gate)).astype(BF16)
        for h in range(perms_per_part):
            yg_ref[part * perms_per_part + h, i * perm_out_rows:(i + 1) * perm_out_rows, :] = (
                y[h * perm_out_rows:(h + 1) * perm_out_rows])

    def unfold_out(part):
        for h in range(perms_per_part):
            t0 = part * part_steps + h * perm_steps
            y_bt = jnp.dot(pout_ref[...], yg_ref[part * perms_per_part + h], preferred_element_type=F32)
            z = zs_ref[:, t0:t0 + perm_steps, :].reshape(perm_in_rows, width).astype(F32)
            y_ref[:, t0:t0 + perm_steps, :] = (
                (y_bt * jax.nn.silu(z)).astype(BF16).reshape(n_batch, perm_steps, width))

    assert fold == N_SSM_BLOCKS
    for part in range(n_parts + 1):
        if part < n_parts:
            fold_in(part)
        for j in range(N_SSM_BLOCKS):
            if part < n_parts:
                state_block(part, j)
            if part >= 1:
                gate_offset(part - 1, j)
        if part >= 1:
            unfold_out(part - 1)


def _ssm_direct_weights(cl, bbc, d_skip):
    fold, gpb, nb, p = SSM_FOLD, GROUPS_PER_BLOCK, N_SSM_BLOCKS, SSM_GROUP
    in_lanes = fold * gpb * p
    lags = jnp.einsum('lgpn,gqn->lgqp', cl, bbc, precision=lax.Precision.HIGHEST)
    skip = d_skip.reshape(SSM_GROUPS, 1, p) * np.eye(p, dtype=np.float32)
    lags = [lags[0] + skip] + [lags[l] for l in range(1, fold)]
    zero = jnp.zeros_like(lags[0])
    toep = jnp.stack([jnp.stack([lags[i - k] if i >= k else zero for i in range(fold)]) for k in range(fold)])
    toep = toep.reshape(fold, fold, nb, gpb, p, p).transpose(2, 0, 3, 4, 1, 5).reshape(nb * in_lanes, fold * p)
    col = np.arange(in_lanes)
    spread = (np.arange(fold * p)[:, None] == ((col // (gpb * p)) * p + col % p)[None, :]).astype(np.float32)
    in_group = (col // p) % gpb
    same_group = (in_group[:, None] == in_group[None, :]).astype(np.float32)
    cd = jnp.dot(toep, spread, precision=lax.Precision.HIGHEST).reshape(nb, in_lanes, in_lanes)
    return (cd * same_group).astype(BF16)


def _ssm_branch(u, zs, ab, ca, cl, bbc, al, d_skip, glu_w, glu_b):
    n_batch, s, _ = u.shape
    assert n_batch == V7X_SUBLANES, "one time group of all batches must fill one sublane tile"
    fold, steps, perm_steps = SSM_FOLD, SSM_STEPS, SSM_PERM_STEPS
    g_rows = (perm_steps // fold) * n_batch
    n_rows = perm_steps * n_batch
    r = np.arange(g_rows)
    src = (r % n_batch) * perm_steps + fold * (r // n_batch)
    pin = np.arange(n_rows)[None, None, :] == (src[None, :, None] + np.arange(fold)[:, None, None])
    pout = jnp.asarray(pin.reshape(fold * g_rows, n_rows).T, dtype=BF16)
    pin = jnp.asarray(pin, dtype=BF16)
    cd = _ssm_direct_weights(cl, bbc, d_skip)
    row_spec = pl.BlockSpec((n_batch, steps, SSM_WIDTH), lambda i: (0, i, 0))
    const2 = lambda i: (0, 0)
    const3 = lambda i: (0, 0, 0)
    n_state = SSM_GROUPS * SSM_STATE
    f_rows = (steps // fold) * n_batch
    fw_all = fold * SSM_WIDTH
    return pl.pallas_call(
        _ssm_kernel,
        grid=(s // steps,),
        in_specs=[
            row_spec, row_spec,
            pl.BlockSpec(pin.shape, const3),
            pl.BlockSpec(pout.shape, const2),
            pl.BlockSpec(ab.shape, lambda i: (0, 0, 0, 0, 0)),
            pl.BlockSpec(ca.shape, lambda i: (0, 0, 0, 0, 0)),
            pl.BlockSpec(cd.shape, const3),
            pl.BlockSpec((1, n_state), const2),
            pl.BlockSpec((1, n_state), const2),
            pl.BlockSpec((SSM_WIDTH, SSM_WIDTH), const2),
            pl.BlockSpec((1, SSM_WIDTH), const2),
        ],
        out_specs=row_spec,
        out_shape=jax.ShapeDtypeStruct((n_batch, s, SSM_WIDTH), BF16),
        scratch_shapes=[
            pltpu.VMEM((f_rows, 2 * n_state + fw_all), BF16),
            pltpu.VMEM((f_rows, 2 * n_state), F32),
            pltpu.VMEM((f_rows, fw_all), F32),
            pltpu.VMEM((steps // perm_steps, fold * g_rows, SSM_WIDTH), BF16),
            pltpu.VMEM((V7X_SUBLANES, 2 * n_state), F32),
            pltpu.VMEM((N_SSM_BLOCKS, fold * V7X_LANES, 2 * BLOCK_STATE), BF16),
            pltpu.VMEM((N_SSM_BLOCKS, 2 * BLOCK_STATE + fold * V7X_LANES, fold * V7X_LANES), BF16),
            pltpu.VMEM((fold * V7X_LANES, 2 * BLOCK_STATE), F32),
        ],
        compiler_params=_compiler_params(1),
        name="ssm_branch",
    )(u, zs, pin, pout, ab, ca, cd, al[0].reshape(1, n_state), al[1].reshape(1, n_state),
      glu_w.astype(BF16), glu_b.reshape(1, SSM_WIDTH))


def _attn_kernel(lambda_init, qt_ref, k_ref, vt_ref, zat_ref, lq1_ref, lk1_ref, lq2_ref, lk2_ref,
                 subw_ref, y_ref, wq_ref, s_ref, mx_ref, acc_ref, m_ref):
    n_tiles, _, tq = qt_ref.shape
    tk = tq
    diag_slot = 2

    def load_queries(q):
        wq_ref[0, :HEAD_DIM] = qt_ref[q, :HEAD_DIM]
        wq_ref[1, HEAD_DIM:] = qt_ref[q, HEAD_DIM:]

    def reset_stats():
        m_ref[...] = jnp.full_like(m_ref, MASK_VALUE)
        acc_ref[...] = jnp.zeros_like(acc_ref)

    def scores(j, slot):
        kb = k_ref[pl.ds(pl.multiple_of(j * tk, tk), tk), :]
        for c in range(2):
            s = jnp.dot(kb, wq_ref[c], preferred_element_type=F32)
            s_ref[slot, c] = s
            mx_ref[slot, c] = jnp.max(s, axis=0, keepdims=True)

    def diagonal_scores(q, slot):
        h = tk // 2
        row0 = pl.multiple_of(q * tk, tk)
        k_lo = k_ref[pl.ds(row0, h), :]
        k_hi = k_ref[pl.ds(row0 + h, h), :]
        causal = (lax.broadcasted_iota(jnp.int32, (h, h), 0) <= lax.broadcasted_iota(jnp.int32, (h, h), 1))
        for c in range(2):
            s_lo = jnp.dot(k_lo, wq_ref[c], preferred_element_type=F32)
            s_hi = jnp.dot(k_hi, wq_ref[c, :, h:], preferred_element_type=F32)
            s_ll = jnp.where(causal, s_lo[:, :h], MASK_VALUE)
            s_hh = jnp.where(causal, s_hi, MASK_VALUE)
            s_ref[slot, c, :h, :h] = s_ll
            s_ref[slot, c, :h, h:] = s_lo[:, h:]
            s_ref[slot, c, h:, :h] = jnp.full((h, h), MASK_VALUE, F32)
            s_ref[slot, c, h:, h:] = s_hh
            mx_ref[slot, c, :, :h] = jnp.max(s_ll, axis=0, keepdims=True)
            mx_ref[slot, c, :, h:] = jnp.maximum(jnp.max(s_lo[:, h:], axis=0, keepdims=True),
                                                 jnp.max(s_hh, axis=0, keepdims=True))

    def accumulate(j, slot):
        vb = jnp.concatenate([vt_ref[j], jnp.ones((SUM_ROWS, tk), BF16)], axis=0)
        for c in range(2):
            m_old = m_ref[c]
            m_new = jnp.maximum(m_old, mx_ref[slot, c])
            alpha = jnp.exp2(m_old - m_new)
            p = jnp.exp2(s_ref[slot, c] - m_new)
            acc_ref[c] = alpha * acc_ref[c] + jnp.dot(vb, p.astype(BF16), preferred_element_type=F32)
            m_ref[c] = m_new

    def finalize(q):
        lam = (jnp.exp(jnp.sum(lq1_ref[...] * lk1_ref[...], axis=-1, keepdims=True))
               - jnp.exp(jnp.sum(lq2_ref[...] * lk2_ref[...], axis=-1, keepdims=True)) + lambda_init)
        l0 = acc_ref[0, V_HEAD_DIM:V_HEAD_DIM + 1]
        l1 = acc_ref[1, V_HEAD_DIM:V_HEAD_DIM + 1]
        out = acc_ref[0, :V_HEAD_DIM] * (1.0 / l0) - acc_ref[1, :V_HEAD_DIM] * (lam / l1)
        ms = jnp.mean(out * out, axis=0, keepdims=True)
        out = out * (lax.rsqrt(ms + EPS) * (1.0 - lambda_init)) * subw_ref[...]
        out = out * jax.nn.silu(zat_ref[q].astype(F32))
        y_ref[pl.ds(pl.multiple_of(q * tq, tq), tq), :] = out.T.astype(BF16)

    zeros = jnp.zeros((HEAD_DIM, tq), BF16)
    wq_ref[0, HEAD_DIM:] = zeros
    wq_ref[1, :HEAD_DIM] = zeros
    load_queries(0)
    reset_stats()
    diagonal_scores(0, 1)

    def enter_tile(q, pending_slot):
        load_queries(q)
        diagonal_scores(q, diag_slot)
        accumulate(jnp.maximum(q - 2, 0), pending_slot)
        finalize(q - 1)
        reset_stats()
        scores(0, 0)
        accumulate(q, diag_slot)

    def pair(i, c):
        j = 2 * i
        scores(j + 1, 1)
        accumulate(j, 0)
        scores(j + 2, 0)
        accumulate(j + 1, 1)
        return c

    def two_tiles(k, carry):
        q_odd = 2 * k + 1
        enter_tile(q_odd, 1)
        lax.fori_loop(0, k, pair, 0)
        q_even = q_odd + 1
        enter_tile(q_even, 0)
        lax.fori_loop(0, k, pair, 0)
        scores(q_even - 1, 1)
        accumulate(q_even - 2, 0)
        return carry

    assert n_tiles % 2 == 0
    lax.fori_loop(0, (n_tiles - 2) // 2, two_tiles, 0)
    q_last = n_tiles - 1
    enter_tile(q_last, 1)
    lax.fori_loop(0, (q_last - 1) // 2, pair, 0)
    accumulate(q_last - 1, 0)
    finalize(q_last)


def _attention(qt, k, vt, zat, lq1, lk1, lq2, lk2, subln_w, lambda_init):
    b, _, nt, _, tq = qt.shape
    s = nt * tq
    assert nt >= 2
    row64 = pl.BlockSpec((1, HEAD_DIM), lambda bi, hi: (0, 0))
    head_spec = pl.BlockSpec((None, None, nt, V_HEAD_DIM, tq), lambda bi, hi: (bi, hi, 0, 0, 0))
    token_spec = pl.BlockSpec((None, s, V_HEAD_DIM), lambda bi, hi: (bi, 0, hi))
    return pl.pallas_call(
        functools.partial(_attn_kernel, lambda_init),
        grid=(b, N_HEADS),
        in_specs=[
            head_spec, token_spec, head_spec, head_spec,
            row64, row64, row64, row64,
            pl.BlockSpec((V_HEAD_DIM, 1), lambda bi, hi: (0, 0)),
        ],
        out_specs=token_spec,
        out_shape=jax.ShapeDtypeStruct((b, s, ATTN_WIDTH), BF16),
        scratch_shapes=[
            pltpu.VMEM((2, 2 * HEAD_DIM, tq), BF16),
            pltpu.VMEM((3, 2, tq, tq), F32),
            pltpu.VMEM((3, 2, 1, tq), F32),
            pltpu.VMEM((2, V_HEAD_DIM + SUM_ROWS, tq), F32),
            pltpu.VMEM((2, 1, tq), F32),
        ],
        compiler_params=_compiler_params(2),
        name="diff_attention",
    )(qt, k, vt, zat, lq1.reshape(1, HEAD_DIM), lk1.reshape(1, HEAD_DIM),
      lq2.reshape(1, HEAD_DIM), lk2.reshape(1, HEAD_DIM), subln_w.reshape(V_HEAD_DIM, 1))


def _out_kernel(x_ref, ys_ref, ya_ref, p_ref, wos_ref, woa_ref, wg_ref, wp_ref, o_ref):
    x2 = (x_ref[...]
          + jnp.dot(ys_ref[...], wos_ref[...], preferred_element_type=F32)
          + jnp.dot(ya_ref[...], woa_ref[...], preferred_element_type=F32))
    gate = jax.nn.sigmoid(jnp.dot(x2.astype(BF16), wg_ref[...], preferred_element_type=F32))
    ple = jnp.dot(p_ref[...].astype(BF16), wp_ref[...], preferred_element_type=F32)
    o_ref[...] = x2 + gate * ple


def _out_proj(x, ys, ya, p, w_out, w_proj, w_gate):
    b, s, d = x.shape
    tm = OUT_TOKEN_TILE
    const2 = lambda bi, ti: (0, 0)
    return pl.pallas_call(
        _out_kernel,
        grid=(b, s // tm),
        in_specs=[
            pl.BlockSpec((None, tm, d), lambda bi, ti: (bi, ti, 0)),
            pl.BlockSpec((None, tm, SSM_WIDTH), lambda bi, ti: (bi, ti, 0)),
            pl.BlockSpec((None, tm, ATTN_WIDTH), lambda bi, ti: (bi, ti, 0)),
            pl.BlockSpec((None, tm, PLE_DIM), lambda bi, ti: (bi, ti, 0)),
            pl.BlockSpec((SSM_WIDTH, d), const2),
            pl.BlockSpec((ATTN_WIDTH, d), const2),
            pl.BlockSpec((d, d), const2),
            pl.BlockSpec((PLE_DIM, d), const2),
        ],
        out_specs=pl.BlockSpec((None, tm, d), lambda bi, ti: (bi, ti, 0)),
        out_shape=jax.ShapeDtypeStruct((b, s, d), F32),
        compiler_params=_compiler_params(2),
        name="out_proj",
    )(x, ys, ya, p, w_out[:SSM_WIDTH].astype(BF16), w_out[SSM_WIDTH:].astype(BF16),
      w_gate.astype(BF16), w_proj.astype(BF16))


def kernel(x, p, positions, norm_w, w_in, ssm_lambda_re, ssm_lambda_im, ssm_log_dt, ssm_b_re, ssm_b_im, ssm_c_re, ssm_c_im, ssm_d, glu_w, glu_b, q_norm_w, k_norm_w, lambda_q1, lambda_k1, lambda_q2, lambda_k2, subln_w, w_out, ple_w_proj, ple_w_gate):
    depth = norm_w.shape[0]
    for i in range(depth):
        lambda_init = 0.8 - 0.6 * math.exp(-0.3 * i)
        ab, ca, cl, bbc, al = _ssm_params(ssm_lambda_re[i], ssm_lambda_im[i], ssm_log_dt[i],
                                          ssm_b_re[i], ssm_b_im[i], ssm_c_re[i], ssm_c_im[i])
        u, zs, qt, k, vt, zat = _in_proj(x, positions, norm_w[i], w_in[i], q_norm_w[i], k_norm_w[i])
        ys = _ssm_branch(u, zs, ab, ca, cl, bbc, al, ssm_d[i], glu_w[i], glu_b[i])
        ya = _attention(qt, k, vt, zat, lambda_q1[i], lambda_k1[i], lambda_q2[i], lambda_k2[i],
                        subln_w[i], lambda_init)
        x = _out_proj(x, ys, ya, p[i], w_out[i], ple_w_proj[i], ple_w_gate[i])
    return x
```

```python
import functools
import math

import jax
import jax.numpy as jnp
import numpy as np
from jax import lax
from jax.experimental import pallas as pl
from jax.experimental.pallas import tpu as pltpu

F32 = jnp.float32
BF16 = jnp.bfloat16

D_MODEL = 1024
PLE_DIM = 256
SSM_WIDTH = 512
SSM_GROUP = 16
SSM_GROUPS = 32
SSM_STATE = 64
ATTN_WIDTH = 512
N_HEADS = 4
HEAD_DIM = 64
V_HEAD_DIM = 128
ROT_DIM = 16
ROT_HALF = 8
ROPE_THETA = 500000.0
EPS = 1e-6
LOG2E = math.log2(math.e)

V7X_LANES = 128
V7X_SUBLANES = 8
V7X_VMEM_BYTES = 64 * 1024 * 1024
VMEM_LIMIT_BYTES = V7X_VMEM_BYTES * 7 // 8

TOKEN_TILE = 1024
ATTN_TILE = 512
OUT_TOKEN_TILE = 1024
SSM_FOLD = 4
SSM_STEPS = 128
SSM_PART_STEPS = 64
SSM_PERM_STEPS = 32
GROUPS_PER_BLOCK = 8
N_SSM_BLOCKS = SSM_GROUPS // GROUPS_PER_BLOCK
BLOCK_STATE = GROUPS_PER_BLOCK * SSM_STATE
MASK_VALUE = -1e30
SUM_ROWS = 16


def _gelu_tanh(x):
    c0 = math.sqrt(2.0 / math.pi)
    inner = x * (c0 + (c0 * 0.044715) * (x * x))
    half = 0.5 * x
    return half + half * jnp.tanh(inner)


def _compiler_params(n_grid_axes):
    return pltpu.CompilerParams(
        dimension_semantics=("arbitrary",) * n_grid_axes,
        vmem_limit_bytes=VMEM_LIMIT_BYTES,
    )


def _ssm_params_kernel(lr_ref, li_ref, logdt_ref, br_ref, bi_ref, cr_ref, ci_ref,
                       ab_ref, ca_ref, cl_ref, bbc_ref, al_ref):
    fold = ab_ref.shape[0]
    n = lr_ref.shape[-1]
    lr = lr_ref[...]
    li = li_ref[...]
    dt = jnp.exp(logdt_ref[...])
    mag = jnp.exp(lr * dt)
    a_re = mag * jnp.cos(li * dt)
    a_im = mag * jnp.sin(li * dt)
    nr = a_re - 1.0
    ni = a_im
    den = lr * lr + li * li
    coef_re = (nr * lr + ni * li) / den
    coef_im = (ni * lr - nr * li) / den
    br = br_ref[...]
    bi = bi_ref[...]
    bb_re = coef_re * br - coef_im * bi
    bb_im = coef_re * bi + coef_im * br
    c_re = cr_ref[...]
    c_im = ci_ref[...]

    def cmul(xr, xi, yr, yi):
        return xr * yr - xi * yi, xr * yi + xi * yr

    powers = [(jnp.ones_like(a_re), jnp.zeros_like(a_im))]
    for _ in range(fold):
        powers.append(cmul(*powers[-1], a_re, a_im))
    for i in range(fold):
        ab_ref[i, 0], ab_ref[i, 1] = cmul(*powers[fold - 1 - i], bb_re, bb_im)
        ca_re, ca_im = cmul(c_re, c_im, *powers[i + 1])
        ca_ref[i, 0] = ca_re
        ca_ref[i, 1] = -ca_im
        cl_re, cl_im = cmul(c_re, c_im, *powers[i])
        cl_ref[i, :, :, :n] = cl_re
        cl_ref[i, :, :, n:] = -cl_im
    bbc_ref[:, :, :n] = bb_re
    bbc_ref[:, :, n:] = bb_im
    al_ref[0] = powers[fold][0]
    al_ref[1] = powers[fold][1]


def _ssm_params(lam_re, lam_im, log_dt, b_re, b_im, c_re, c_im):
    g, n, p, fold = SSM_GROUPS, SSM_STATE, SSM_GROUP, SSM_FOLD
    per_offset = jax.ShapeDtypeStruct((fold, 2, g, p, n), F32)
    return pl.pallas_call(
        _ssm_params_kernel,
        out_shape=(per_offset, per_offset,
                   jax.ShapeDtypeStruct((fold, g, p, 2 * n), F32),
                   jax.ShapeDtypeStruct((g, p, 2 * n), F32),
                   jax.ShapeDtypeStruct((2, g, 1, n), F32)),
        name="ssm_params",
    )(lam_re.reshape(g, 1, n), lam_im.reshape(g, 1, n), log_dt.reshape(g, 1, 1),
      jnp.swapaxes(b_re, 1, 2), jnp.swapaxes(b_im, 1, 2), c_re, c_im)


def _in_proj_kernel(x_ref, nw_ref, w_ref, pos_ref, invf_ref, qnw_ref, knw_ref,
                    u_ref, zs_ref, qt_ref, k_ref, vt_ref, zat_ref, wnat_ref, wtr_ref):
    @pl.when((pl.program_id(0) == 0) & (pl.program_id(1) == 0))
    def _():
        n_nat = wnat_ref.shape[1]
        wnat_ref[...] = w_ref[:, :n_nat].astype(BF16)
        chunk = wtr_ref.shape[1] // 2
        for c0 in range(0, wtr_ref.shape[0], chunk):
            wtr_ref[c0:c0 + chunk, :] = w_ref[:, n_nat + c0:n_nat + c0 + chunk].T.astype(BF16)

    x = x_ref[...]
    ms = jnp.mean(x * x, axis=-1, keepdims=True)
    h = (x * lax.rsqrt(ms + EPS) * nw_ref[...]).astype(BF16)
    nat = jnp.dot(h, wnat_ref[...], preferred_element_type=F32)
    u_ref[...] = nat[:, :SSM_WIDTH].astype(BF16)
    zs_ref[...] = nat[:, SSM_WIDTH:].astype(BF16)
    tr = lax.dot_general(wtr_ref[...], h, (((1,), (1,)), ((), ())),
                         preferred_element_type=F32)
    tm = x.shape[0]
    ang = pos_ref[...].astype(F32) * invf_ref[...]
    cos = jnp.cos(ang)
    sin = jnp.sin(ang)

    def norm_rope(t, w_ref):
        t3 = t.reshape(2 * N_HEADS, HEAD_DIM, tm)
        ms3 = jnp.mean(t3 * t3, axis=1, keepdims=True)
        t3 = t3 * lax.rsqrt(ms3 + EPS) * w_ref[...]
        t1 = t3[:, :ROT_HALF, :]
        t2 = t3[:, ROT_HALF:ROT_DIM, :]
        r1 = t1 * cos - t2 * sin
        r2 = t2 * cos + t1 * sin
        t3 = jnp.concatenate([r1, r2, t3[:, ROT_DIM:, :]], axis=1)
        return t3.reshape(ATTN_WIDTH, tm)

    def store_head_major(ref, t):
        t = t.reshape(N_HEADS, V_HEAD_DIM, tm).astype(BF16)
        ta = ref.shape[-1]
        for j in range(tm // ta):
            ref[:, j] = t[:, :, j * ta:(j + 1) * ta]

    q = norm_rope(tr[0:ATTN_WIDTH], qnw_ref) * (HEAD_DIM ** -0.5 * LOG2E)
    store_head_major(qt_ref, q)
    k = norm_rope(tr[ATTN_WIDTH:2 * ATTN_WIDTH], knw_ref)
    k_ref[...] = k.T.astype(BF16)
    store_head_major(vt_ref, tr[2 * ATTN_WIDTH:3 * ATTN_WIDTH])
    store_head_major(zat_ref, tr[3 * ATTN_WIDTH:])


def _in_proj(x, positions, norm_w, w_in, q_norm_w, k_norm_w):
    b, s, d = x.shape
    tm, ta = TOKEN_TILE, ATTN_TILE
    nt, tiles_per_step = s // ta, tm // ta
    inv_freq = ROPE_THETA ** (-jnp.arange(0, ROT_DIM, 2, dtype=F32) / ROT_DIM)
    head_major = jax.ShapeDtypeStruct((b, N_HEADS, nt, V_HEAD_DIM, ta), BF16)
    head_spec = pl.BlockSpec((None, N_HEADS, tiles_per_step, V_HEAD_DIM, ta), lambda bi, ti: (bi, 0, ti, 0, 0))
    ssm_shape = jax.ShapeDtypeStruct((b, s, SSM_WIDTH), BF16)
    ssm_spec = pl.BlockSpec((None, tm, SSM_WIDTH), lambda bi, ti: (bi, ti, 0))
    const2 = lambda bi, ti: (0, 0)
    const3 = lambda bi, ti: (0, 0, 0)
    return pl.pallas_call(
        _in_proj_kernel,
        grid=(b, s // tm),
        in_specs=[
            pl.BlockSpec((None, tm, d), lambda bi, ti: (bi, ti, 0)),
            pl.BlockSpec((1, d), const2),
            pl.BlockSpec(w_in.shape, const2),
            pl.BlockSpec((None, 1, tm), lambda bi, ti: (bi, 0, ti)),
            pl.BlockSpec((ROT_HALF, 1), const2),
            pl.BlockSpec((1, HEAD_DIM, 1), const3),
            pl.BlockSpec((1, HEAD_DIM, 1), const3),
        ],
        out_specs=[
            ssm_spec, ssm_spec,
            head_spec,
            pl.BlockSpec((None, tm, ATTN_WIDTH), lambda bi, ti: (bi, ti, 0)),
            head_spec, head_spec,
        ],
        out_shape=(ssm_shape, ssm_shape, head_major,
                   jax.ShapeDtypeStruct((b, s, ATTN_WIDTH), BF16), head_major, head_major),
        scratch_shapes=[
            pltpu.VMEM((d, 2 * SSM_WIDTH), BF16),
            pltpu.VMEM((4 * ATTN_WIDTH, d), BF16),
        ],
        compiler_params=_compiler_params(2),
        name="in_proj",
    )(x, norm_w.reshape(1, d), w_in, positions.reshape(b, 1, s),
      inv_freq.reshape(ROT_HALF, 1), q_norm_w.reshape(1, HEAD_DIM, 1), k_norm_w.reshape(1, HEAD_DIM, 1))


def _ssm_kernel(u_ref, zs_ref, pin_ref, pout_ref, ab_ref, ca_ref, cd_ref, ar_ref, ai_ref, d_ref,
                gw_ref, gb_ref, y_ref, uf_ref, bu_ref, xs_ref, y2_ref, yg_ref, state_ref,
                bw_ref, cs_ref, tmp_ref):
    n_batch, steps, width = u_ref.shape
    fold, perm_out_rows, perm_in_rows = pin_ref.shape
    perm_steps = perm_in_rows // n_batch
    part_steps = SSM_PART_STEPS
    perms_per_part = part_steps // perm_steps
    part_rows = perms_per_part * perm_out_rows
    n_parts = steps // part_steps
    blk = 2 * BLOCK_STATE
    fw = fold * V7X_LANES
    pack_rows = 2 * V7X_SUBLANES

    @pl.when(pl.program_id(0) == 0)
    def _():
        state_ref[...] = jnp.zeros_like(state_ref)
        bw_ref[...] = jnp.zeros_like(bw_ref)
        for m in range(N_SSM_BLOCKS):
            tmp_ref[...] = jnp.zeros_like(tmp_ref)
            for i in range(fold):
                for gl in range(GROUPS_PER_BLOCK):
                    g = m * GROUPS_PER_BLOCK + gl
                    in_rows = slice(i * V7X_LANES + gl * SSM_GROUP, i * V7X_LANES + (gl + 1) * SSM_GROUP)
                    for ri in range(2):
                        st_cols = slice(ri * BLOCK_STATE + gl * SSM_STATE, ri * BLOCK_STATE + (gl + 1) * SSM_STATE)
                        bw_ref[m, in_rows, st_cols] = ab_ref[i, ri, g].astype(BF16)
                        tmp_ref[in_rows, st_cols] = ca_ref[i, ri, g]
            cs_ref[m] = tmp_ref[...].T.astype(BF16)

    def rows_of(part):
        return slice(part * part_rows, (part + 1) * part_rows)

    def fold_in(part):
        for h in range(perms_per_part):
            t0 = part * part_steps + h * perm_steps
            u_bt = u_ref[:, t0:t0 + perm_steps, :].reshape(perm_in_rows, width)
            r0 = part * part_rows + h * perm_out_rows
            for i in range(fold):
                sel = jnp.dot(pin_ref[i], u_bt, preferred_element_type=F32).astype(BF16)
                for m in range(N_SSM_BLOCKS):
                    uf_ref[r0:r0 + perm_out_rows, m * fw + i * V7X_LANES:m * fw + (i + 1) * V7X_LANES] = (
                        sel[:, m * V7X_LANES:(m + 1) * V7X_LANES])

    def state_block(part, m):
        rows = rows_of(part)
        re = slice(m * blk, m * blk + BLOCK_STATE)
        im = slice(m * blk + BLOCK_STATE, (m + 1) * blk)
        bu_ref[rows, m * blk:(m + 1) * blk] = jnp.dot(
            uf_ref[rows, m * fw:(m + 1) * fw], bw_ref[m], preferred_element_type=F32)
        a_re = jnp.broadcast_to(ar_ref[:, m * BLOCK_STATE:(m + 1) * BLOCK_STATE], (V7X_SUBLANES, BLOCK_STATE))
        a_im = jnp.broadcast_to(ai_ref[:, m * BLOCK_STATE:(m + 1) * BLOCK_STATE], (V7X_SUBLANES, BLOCK_STATE))
        x_re = state_ref[:, re]
        x_im = state_ref[:, im]
        for r0 in range(part * part_rows, (part + 1) * part_rows, pack_rows):
            ins_re, ins_im = [], []
            for half in range(2):
                rr = slice(r0 + half * V7X_SUBLANES, r0 + (half + 1) * V7X_SUBLANES)
                ins_re.append(x_re)
                ins_im.append(x_im)
                n_re = a_re * x_re - a_im * x_im + bu_ref[rr, re]
                n_im = a_re * x_im + a_im * x_re + bu_ref[rr, im]
                x_re, x_im = n_re, n_im
            xs_ref[r0:r0 + pack_rows, re] = jnp.concatenate(ins_re, axis=0).astype(BF16)
            xs_ref[r0:r0 + pack_rows, im] = jnp.concatenate(ins_im, axis=0).astype(BF16)
        state_ref[:, re] = x_re
        state_ref[:, im] = x_im
        y2_ref[rows, m * fw:(m + 1) * fw] = (
            jnp.dot(xs_ref[rows, m * blk:(m + 1) * blk], cs_ref[m], preferred_element_type=F32)
            + jnp.dot(uf_ref[rows, m * fw:(m + 1) * fw], cd_ref[m], preferred_element_type=F32))

    def gate_offset(part, i):
        rows = rows_of(part)
        cols = [slice(m * fw + i * V7X_LANES, m * fw + (i + 1) * V7X_LANES) for m in range(N_SSM_BLOCKS)]
        y = jnp.concatenate([y2_ref[rows, c] for c in cols], axis=-1)
        u_i = jnp.concatenate([uf_ref[rows, c] for c in cols], axis=-1).astype(F32)
        y = _gelu_tanh(y + d_ref[...] * u_i)
        gate = jnp.dot(y.astype(BF16), gw_ref[...], preferred_element_type=F32) + gb_ref[...]
        y = (y * jax.nn.sigmoid(gate)).astype(BF16)
        for h in range(perms_per_part):
            yg_ref[part * perms_per_part + h, i * perm_out_rows:(i + 1) * perm_out_rows, :] = (
                y[h * perm_out_rows:(h + 1) * perm_out_rows])

    def unfold_out(part):
        for h in range(perms_per_part):
            t0 = part * part_steps + h * perm_steps
            y_bt = jnp.dot(pout_ref[...], yg_ref[part * perms_per_part + h], preferred_element_type=F32)
            z = zs_ref[:, t0:t0 + perm_steps, :].reshape(perm_in_rows, width).astype(F32)
            y_ref[:, t0:t0 + perm_steps, :] = (
                (y_bt * jax.nn.silu(z)).astype(BF16).reshape(n_batch, perm_steps, width))

    assert fold == N_SSM_BLOCKS
    for part in range(n_parts + 1):
        if part < n_parts:
            fold_in(part)
        for j in range(N_SSM_BLOCKS):
            if part < n_parts:
                state_block(part, j)
            if part >= 1:
                gate_offset(part - 1, j)
        if part >= 1:
            unfold_out(part - 1)


def _ssm_direct_weights(cl, bbc):
    fold, gpb, nb, p = SSM_FOLD, GROUPS_PER_BLOCK, N_SSM_BLOCKS, SSM_GROUP
    in_lanes = fold * gpb * p
    lags = jnp.einsum('lgpn,gqn->lgqp', cl, bbc, precision=lax.Precision.HIGHEST)
    zero = jnp.zeros_like(lags[0])
    toep = jnp.stack([jnp.stack([lags[i - k] if i >= k else zero for i in range(fold)]) for k in range(fold)])
    toep = toep.reshape(fold, fold, nb, gpb, p, p).transpose(2, 0, 3, 4, 1, 5).reshape(nb * in_lanes, fold * p)
    col = np.arange(in_lanes)
    spread = (np.arange(fold * p)[:, None] == ((col // (gpb * p)) * p + col % p)[None, :]).astype(np.float32)
    in_group = (col // p) % gpb
    same_group = (in_group[:, None] == in_group[None, :]).astype(np.float32)
    cd = jnp.dot(toep, spread, precision=lax.Precision.HIGHEST).reshape(nb, in_lanes, in_lanes)
    return (cd * same_group).astype(BF16)


def _ssm_branch(u, zs, ab, ca, cl, bbc, al, d_skip, glu_w, glu_b):
    n_batch, s, _ = u.shape
    assert n_batch == V7X_SUBLANES, "one time group of all batches must fill one sublane tile"
    fold, steps, perm_steps = SSM_FOLD, SSM_STEPS, SSM_PERM_STEPS
    g_rows = (perm_steps // fold) * n_batch
    n_rows = perm_steps * n_batch
    r = np.arange(g_rows)
    src = (r % n_batch) * perm_steps + fold * (r // n_batch)
    pin = np.arange(n_rows)[None, None, :] == (src[None, :, None] + np.arange(fold)[:, None, None])
    pout = jnp.asarray(pin.reshape(fold * g_rows, n_rows).T, dtype=BF16)
    pin = jnp.asarray(pin, dtype=BF16)
    cd = _ssm_direct_weights(cl, bbc)
    row_spec = pl.BlockSpec((n_batch, steps, SSM_WIDTH), lambda i: (0, i, 0))
    const2 = lambda i: (0, 0)
    const3 = lambda i: (0, 0, 0)
    n_state = SSM_GROUPS * SSM_STATE
    f_rows = (steps // fold) * n_batch
    fw_all = fold * SSM_WIDTH
    return pl.pallas_call(
        _ssm_kernel,
        grid=(s // steps,),
        in_specs=[
            row_spec, row_spec,
            pl.BlockSpec(pin.shape, const3),
            pl.BlockSpec(pout.shape, const2),
            pl.BlockSpec(ab.shape, lambda i: (0, 0, 0, 0, 0)),
            pl.BlockSpec(ca.shape, lambda i: (0, 0, 0, 0, 0)),
            pl.BlockSpec(cd.shape, const3),
            pl.BlockSpec((1, n_state), const2),
            pl.BlockSpec((1, n_state), const2),
            pl.BlockSpec((1, SSM_WIDTH), const2),
            pl.BlockSpec((SSM_WIDTH, SSM_WIDTH), const2),
            pl.BlockSpec((1, SSM_WIDTH), const2),
        ],
        out_specs=row_spec,
        out_shape=jax.ShapeDtypeStruct((n_batch, s, SSM_WIDTH), BF16),
        scratch_shapes=[
            pltpu.VMEM((f_rows, fw_all), BF16),
            pltpu.VMEM((f_rows, 2 * n_state), F32),
            pltpu.VMEM((f_rows, 2 * n_state), BF16),
            pltpu.VMEM((f_rows, fw_all), F32),
            pltpu.VMEM((steps // perm_steps, fold * g_rows, SSM_WIDTH), BF16),
            pltpu.VMEM((V7X_SUBLANES, 2 * n_state), F32),
            pltpu.VMEM((N_SSM_BLOCKS, fold * V7X_LANES, 2 * BLOCK_STATE), BF16),
            pltpu.VMEM((N_SSM_BLOCKS, 2 * BLOCK_STATE, fold * V7X_LANES), BF16),
            pltpu.VMEM((fold * V7X_LANES, 2 * BLOCK_STATE), F32),
        ],
        compiler_params=_compiler_params(1),
        name="ssm_branch",
    )(u, zs, pin, pout, ab, ca, cd, al[0].reshape(1, n_state), al[1].reshape(1, n_state),
      d_skip.reshape(1, SSM_WIDTH), glu_w.astype(BF16), glu_b.reshape(1, SSM_WIDTH))


def _attn_kernel(lambda_init, qt_ref, k_ref, vt_ref, zat_ref, lq1_ref, lk1_ref, lq2_ref, lk2_ref,
                 subw_ref, y_ref, wq_ref, s_ref, mx_ref, acc_ref, m_ref):
    n_tiles, _, tq = qt_ref.shape
    tk = tq
    diag_slot = 2

    def load_queries(q):
        wq_ref[0, :HEAD_DIM] = qt_ref[q, :HEAD_DIM]
        wq_ref[1, HEAD_DIM:] = qt_ref[q, HEAD_DIM:]

    def reset_stats():
        m_ref[...] = jnp.full_like(m_ref, MASK_VALUE)
        acc_ref[...] = jnp.zeros_like(acc_ref)

    def scores(j, slot):
        kb = k_ref[pl.ds(pl.multiple_of(j * tk, tk), tk), :]
        for c in range(2):
            s = jnp.dot(kb, wq_ref[c], preferred_element_type=F32)
            s_ref[slot, c] = s
            mx_ref[slot, c] = jnp.max(s, axis=0, keepdims=True)

    def diagonal_scores(q, slot, fill_masked=False):
        h = tk // 2
        row0 = pl.multiple_of(q * tk, tk)
        k_lo = k_ref[pl.ds(row0, h), :]
        k_hi = k_ref[pl.ds(row0 + h, h), :]
        causal = (lax.broadcasted_iota(jnp.int32, (h, h), 0) <= lax.broadcasted_iota(jnp.int32, (h, h), 1))
        for c in range(2):
            s_lo = jnp.dot(k_lo, wq_ref[c], preferred_element_type=F32)
            s_hi = jnp.dot(k_hi, wq_ref[c, :, h:], preferred_element_type=F32)
            s_ll = jnp.where(causal, s_lo[:, :h], MASK_VALUE)
            s_hh = jnp.where(causal, s_hi, MASK_VALUE)
            s_ref[slot, c, :h, :h] = s_ll
            s_ref[slot, c, :h, h:] = s_lo[:, h:]
            if fill_masked:
                s_ref[slot, c, h:, :h] = jnp.full((h, h), MASK_VALUE, F32)
            s_ref[slot, c, h:, h:] = s_hh
            mx_ref[slot, c, :, :h] = jnp.max(s_ll, axis=0, keepdims=True)
            mx_ref[slot, c, :, h:] = jnp.maximum(jnp.max(s_lo[:, h:], axis=0, keepdims=True),
                                                 jnp.max(s_hh, axis=0, keepdims=True))

    def accumulate(j, slot):
        vb = jnp.concatenate([vt_ref[j], jnp.ones((SUM_ROWS, tk), BF16)], axis=0)
        for c in range(2):
            m_old = m_ref[c]
            m_new = jnp.maximum(m_old, mx_ref[slot, c])
            alpha = jnp.exp2(m_old - m_new)
            p = jnp.exp2(s_ref[slot, c] - m_new)
            acc_ref[c] = alpha * acc_ref[c] + jnp.dot(vb, p.astype(BF16), preferred_element_type=F32)
            m_ref[c] = m_new

    def accumulate_diagonal_first(q, slot):
        h = tk // 2
        vb = jnp.concatenate([vt_ref[q], jnp.ones((SUM_ROWS, tk), BF16)], axis=0)
        for c in range(2):
            m_new = mx_ref[slot, c]
            p_early = jnp.exp2(s_ref[slot, c, :h, :h] - m_new[:, :h])
            p_late = jnp.exp2(s_ref[slot, c, :, h:] - m_new[:, h:])
            acc_ref[c, :, :h] = jnp.dot(vb[:, :h], p_early.astype(BF16), preferred_element_type=F32)
            acc_ref[c, :, h:] = jnp.dot(vb, p_late.astype(BF16), preferred_element_type=F32)
            m_ref[c] = m_new

    def finalize(q):
        lam = (jnp.exp(jnp.sum(lq1_ref[...] * lk1_ref[...], axis=-1, keepdims=True))
               - jnp.exp(jnp.sum(lq2_ref[...] * lk2_ref[...], axis=-1, keepdims=True)) + lambda_init)
        l0 = acc_ref[0, V_HEAD_DIM:V_HEAD_DIM + 1]
        l1 = acc_ref[1, V_HEAD_DIM:V_HEAD_DIM + 1]
        out = acc_ref[0, :V_HEAD_DIM] * (1.0 / l0) - acc_ref[1, :V_HEAD_DIM] * (lam / l1)
        ms = jnp.mean(out * out, axis=0, keepdims=True)
        out = out * (lax.rsqrt(ms + EPS) * (1.0 - lambda_init)) * subw_ref[...]
        out = out * jax.nn.silu(zat_ref[q].astype(F32))
        y_ref[pl.ds(pl.multiple_of(q * tq, tq), tq), :] = out.T.astype(BF16)

    zeros = jnp.zeros((HEAD_DIM, tq), BF16)
    wq_ref[0, HEAD_DIM:] = zeros
    wq_ref[1, :HEAD_DIM] = zeros
    load_queries(0)
    reset_stats()
    diagonal_scores(0, 1, fill_masked=True)

    def enter_tile(q, pending_slot):
        load_queries(q)
        diagonal_scores(q, diag_slot)
        accumulate(jnp.maximum(q - 2, 0), pending_slot)
        finalize(q - 1)
        scores(0, 0)
        accumulate_diagonal_first(q, diag_slot)

    def pair(i, c):
        j = 2 * i
        scores(j + 1, 1)
        accumulate(j, 0)
        scores(j + 2, 0)
        accumulate(j + 1, 1)
        return c

    def two_tiles(k, carry):
        q_odd = 2 * k + 1
        enter_tile(q_odd, 1)
        lax.fori_loop(0, k, pair, 0)
        q_even = q_odd + 1
        enter_tile(q_even, 0)
        lax.fori_loop(0, k, pair, 0)
        scores(q_even - 1, 1)
        accumulate(q_even - 2, 0)
        return carry

    assert n_tiles % 2 == 0
    lax.fori_loop(0, (n_tiles - 2) // 2, two_tiles, 0)
    q_last = n_tiles - 1
    enter_tile(q_last, 1)
    lax.fori_loop(0, (q_last - 1) // 2, pair, 0)
    accumulate(q_last - 1, 0)
    finalize(q_last)


def _attention(qt, k, vt, zat, lq1, lk1, lq2, lk2, subln_w, lambda_init):
    b, _, nt, _, tq = qt.shape
    s = nt * tq
    assert nt >= 2
    row64 = pl.BlockSpec((1, HEAD_DIM), lambda bi, hi: (0, 0))
    head_spec = pl.BlockSpec((None, None, nt, V_HEAD_DIM, tq), lambda bi, hi: (bi, hi, 0, 0, 0))
    token_spec = pl.BlockSpec((None, s, V_HEAD_DIM), lambda bi, hi: (bi, 0, hi))
    return pl.pallas_call(
        functools.partial(_attn_kernel, lambda_init),
        grid=(b, N_HEADS),
        in_specs=[
            head_spec, token_spec, head_spec, head_spec,
            row64, row64, row64, row64,
            pl.BlockSpec((V_HEAD_DIM, 1), lambda bi, hi: (0, 0)),
        ],
        out_specs=token_spec,
        out_shape=jax.ShapeDtypeStruct((b, s, ATTN_WIDTH), BF16),
        scratch_shapes=[
            pltpu.VMEM((2, 2 * HEAD_DIM, tq), BF16),
            pltpu.VMEM((3, 2, tq, tq), F32),
            pltpu.VMEM((3, 2, 1, tq), F32),
            pltpu.VMEM((2, V_HEAD_DIM + SUM_ROWS, tq), F32),
            pltpu.VMEM((2, 1, tq), F32),
        ],
        compiler_params=_compiler_params(2),
        name="diff_attention",
    )(qt, k, vt, zat, lq1.reshape(1, HEAD_DIM), lk1.reshape(1, HEAD_DIM),
      lq2.reshape(1, HEAD_DIM), lk2.reshape(1, HEAD_DIM), subln_w.reshape(V_HEAD_DIM, 1))


def _out_kernel(x_ref, ys_ref, ya_ref, p_ref, wos_ref, woa_ref, wg_ref, wp_ref, o_ref):
    x2 = (x_ref[...]
          + jnp.dot(ys_ref[...], wos_ref[...], preferred_element_type=F32)
          + jnp.dot(ya_ref[...], woa_ref[...], preferred_element_type=F32))
    gate = jax.nn.sigmoid(jnp.dot(x2.astype(BF16), wg_ref[...], preferred_element_type=F32))
    ple = jnp.dot(p_ref[...].astype(BF16), wp_ref[...], preferred_element_type=F32)
    o_ref[...] = x2 + gate * ple


def _out_proj(x, ys, ya, p, w_out, w_proj, w_gate):
    b, s, d = x.shape
    tm = OUT_TOKEN_TILE
    const2 = lambda bi, ti: (0, 0)
    return pl.pallas_call(
        _out_kernel,
        grid=(b, s // tm),
        in_specs=[
            pl.BlockSpec((None, tm, d), lambda bi, ti: (bi, ti, 0)),
            pl.BlockSpec((None, tm, SSM_WIDTH), lambda bi, ti: (bi, ti, 0)),
            pl.BlockSpec((None, tm, ATTN_WIDTH), lambda bi, ti: (bi, ti, 0)),
            pl.BlockSpec((None, tm, PLE_DIM), lambda bi, ti: (bi, ti, 0)),
            pl.BlockSpec((SSM_WIDTH, d), const2),
            pl.BlockSpec((ATTN_WIDTH, d), const2),
            pl.BlockSpec((d, d), const2),
            pl.BlockSpec((PLE_DIM, d), const2),
        ],
        out_specs=pl.BlockSpec((None, tm, d), lambda bi, ti: (bi, ti, 0)),
        out_shape=jax.ShapeDtypeStruct((b, s, d), F32),
        compiler_params=_compiler_params(2),
        name="out_proj",
    )(x, ys, ya, p, w_out[:SSM_WIDTH].astype(BF16), w_out[SSM_WIDTH:].astype(BF16),
      w_gate.astype(BF16), w_proj.astype(BF16))


def kernel(x, p, positions, norm_w, w_in, ssm_lambda_re, ssm_lambda_im, ssm_log_dt, ssm_b_re, ssm_b_im, ssm_c_re, ssm_c_im, ssm_d, glu_w, glu_b, q_norm_w, k_norm_w, lambda_q1, lambda_k1, lambda_q2, lambda_k2, subln_w, w_out, ple_w_proj, ple_w_gate):
    depth = norm_w.shape[0]
    for i in range(depth):
        lambda_init = 0.8 - 0.6 * math.exp(-0.3 * i)
        ab, ca, cl, bbc, al = _ssm_params(ssm_lambda_re[i], ssm_lambda_im[i], ssm_log_dt[i],
                                          ssm_b_re[i], ssm_b_im[i], ssm_c_re[i], ssm_c_im[i])
        u, zs, qt, k, vt, zat = _in_proj(x, positions, norm_w[i], w_in[i], q_norm_w[i], k_norm_w[i])
        ys = _ssm_branch(u, zs, ab, ca, cl, bbc, al, ssm_d[i], glu_w[i], glu_b[i])
        ya = _attention(qt, k, vt, zat, lambda_q1[i], lambda_k1[i], lambda_q2[i], lambda_k2[i],
                        subln_w[i], lambda_init)
        x = _out_proj(x, ys, ya, p[i], w_out[i], ple_w_proj[i], ple_w_gate[i])
    return x
```

```python
import functools
import math

import jax
import jax.numpy as jnp
import numpy as np
from jax import lax
from jax.experimental import pallas as pl
from jax.experimental.pallas import tpu as pltpu

F32 = jnp.float32
BF16 = jnp.bfloat16

D_MODEL = 1024
PLE_DIM = 256
SSM_WIDTH = 512
SSM_GROUP = 16
SSM_GROUPS = 32
SSM_STATE = 64
ATTN_WIDTH = 512
N_HEADS = 4
HEAD_DIM = 64
V_HEAD_DIM = 128
ROT_DIM = 16
ROT_HALF = 8
ROPE_THETA = 500000.0
EPS = 1e-6
LOG2E = math.log2(math.e)

V7X_LANES = 128
V7X_SUBLANES = 8
V7X_VMEM_BYTES = 64 * 1024 * 1024
VMEM_LIMIT_BYTES = V7X_VMEM_BYTES * 7 // 8

TOKEN_TILE = 1024
ATTN_TILE = 512
OUT_TOKEN_TILE = 1024
SSM_FOLD = 4
SSM_STEPS = 128
SSM_PART_STEPS = 64
SSM_PERM_STEPS = 32
GROUPS_PER_BLOCK = 8
N_SSM_BLOCKS = SSM_GROUPS // GROUPS_PER_BLOCK
BLOCK_STATE = GROUPS_PER_BLOCK * SSM_STATE
MASK_VALUE = -1e30
SUM_ROWS = 16


def _gelu_tanh(x):
    c0 = math.sqrt(2.0 / math.pi)
    inner = x * (c0 + (c0 * 0.044715) * (x * x))
    half = 0.5 * x
    return half + half * jnp.tanh(inner)


def _compiler_params(n_grid_axes):
    return pltpu.CompilerParams(
        dimension_semantics=("arbitrary",) * n_grid_axes,
        vmem_limit_bytes=VMEM_LIMIT_BYTES,
    )


def _ssm_params_kernel(lr_ref, li_ref, logdt_ref, br_ref, bi_ref, cr_ref, ci_ref,
                       ab_ref, ca_ref, cl_ref, bbc_ref, al_ref):
    fold = ab_ref.shape[0]
    n = lr_ref.shape[-1]
    lr = lr_ref[...]
    li = li_ref[...]
    dt = jnp.exp(logdt_ref[...])
    mag = jnp.exp(lr * dt)
    a_re = mag * jnp.cos(li * dt)
    a_im = mag * jnp.sin(li * dt)
    nr = a_re - 1.0
    ni = a_im
    den = lr * lr + li * li
    coef_re = (nr * lr + ni * li) / den
    coef_im = (ni * lr - nr * li) / den
    br = br_ref[...]
    bi = bi_ref[...]
    bb_re = coef_re * br - coef_im * bi
    bb_im = coef_re * bi + coef_im * br
    c_re = cr_ref[...]
    c_im = ci_ref[...]

    def cmul(xr, xi, yr, yi):
        return xr * yr - xi * yi, xr * yi + xi * yr

    powers = [(jnp.ones_like(a_re), jnp.zeros_like(a_im))]
    for _ in range(fold):
        powers.append(cmul(*powers[-1], a_re, a_im))
    for i in range(fold):
        ab_ref[i, 0], ab_ref[i, 1] = cmul(*powers[fold - 1 - i], bb_re, bb_im)
        ca_re, ca_im = cmul(c_re, c_im, *powers[i + 1])
        ca_ref[i, 0] = ca_re
        ca_ref[i, 1] = -ca_im
        cl_re, cl_im = cmul(c_re, c_im, *powers[i])
        cl_ref[i, :, :, :n] = cl_re
        cl_ref[i, :, :, n:] = -cl_im
    bbc_ref[:, :, :n] = bb_re
    bbc_ref[:, :, n:] = bb_im
    al_ref[0] = powers[fold][0]
    al_ref[1] = powers[fold][1]


def _ssm_params(lam_re, lam_im, log_dt, b_re, b_im, c_re, c_im):
    g, n, p, fold = SSM_GROUPS, SSM_STATE, SSM_GROUP, SSM_FOLD
    per_offset = jax.ShapeDtypeStruct((fold, 2, g, p, n), F32)
    return pl.pallas_call(
        _ssm_params_kernel,
        out_shape=(per_offset, per_offset,
                   jax.ShapeDtypeStruct((fold, g, p, 2 * n), F32),
                   jax.ShapeDtypeStruct((g, p, 2 * n), F32),
                   jax.ShapeDtypeStruct((2, g, 1, n), F32)),
        name="ssm_params",
    )(lam_re.reshape(g, 1, n), lam_im.reshape(g, 1, n), log_dt.reshape(g, 1, 1),
      jnp.swapaxes(b_re, 1, 2), jnp.swapaxes(b_im, 1, 2), c_re, c_im)


def _in_proj_kernel(x_ref, nw_ref, w_ref, pos_ref, invf_ref, qnw_ref, knw_ref,
                    u_ref, zs_ref, qt_ref, k_ref, vt_ref, zat_ref, wnat_ref, wtr_ref):
    @pl.when((pl.program_id(0) == 0) & (pl.program_id(1) == 0))
    def _():
        n_nat = wnat_ref.shape[1]
        wnat_ref[...] = w_ref[:, :n_nat].astype(BF16)
        chunk = wtr_ref.shape[1] // 2
        for c0 in range(0, wtr_ref.shape[0], chunk):
            wtr_ref[c0:c0 + chunk, :] = w_ref[:, n_nat + c0:n_nat + c0 + chunk].T.astype(BF16)

    x = x_ref[...]
    ms = jnp.mean(x * x, axis=-1, keepdims=True)
    h = (x * lax.rsqrt(ms + EPS) * nw_ref[...]).astype(BF16)
    tm = x.shape[0]

    def channel_major(r0, r1):
        return lax.dot_general(wtr_ref[r0:r1, :], h, (((1,), (1,)), ((), ())), preferred_element_type=F32)

    ang = pos_ref[...].astype(F32) * invf_ref[...]
    cos = jnp.cos(ang)
    sin = jnp.sin(ang)

    def norm_rope(t, w_ref):
        t3 = t.reshape(2 * N_HEADS, HEAD_DIM, tm)
        ms3 = jnp.mean(t3 * t3, axis=1, keepdims=True)
        t3 = t3 * lax.rsqrt(ms3 + EPS) * w_ref[...]
        t1 = t3[:, :ROT_HALF, :]
        t2 = t3[:, ROT_HALF:ROT_DIM, :]
        r1 = t1 * cos - t2 * sin
        r2 = t2 * cos + t1 * sin
        t3 = jnp.concatenate([r1, r2, t3[:, ROT_DIM:, :]], axis=1)
        return t3.reshape(ATTN_WIDTH, tm)

    def store_head_major(ref, t):
        t = t.reshape(N_HEADS, V_HEAD_DIM, tm).astype(BF16)
        ta = ref.shape[-1]
        for j in range(tm // ta):
            ref[:, j] = t[:, :, j * ta:(j + 1) * ta]

    q = norm_rope(channel_major(0, ATTN_WIDTH), qnw_ref) * (HEAD_DIM ** -0.5 * LOG2E)
    store_head_major(qt_ref, q)
    k = norm_rope(channel_major(ATTN_WIDTH, 2 * ATTN_WIDTH), knw_ref)
    k_ref[...] = k.T.astype(BF16)
    store_head_major(vt_ref, channel_major(2 * ATTN_WIDTH, 3 * ATTN_WIDTH))
    store_head_major(zat_ref, channel_major(3 * ATTN_WIDTH, 4 * ATTN_WIDTH))
    nat = jnp.dot(h, wnat_ref[...], preferred_element_type=F32)
    u_ref[...] = nat[:, :SSM_WIDTH].astype(BF16)
    zs_ref[...] = nat[:, SSM_WIDTH:].astype(BF16)


def _in_proj(x, positions, norm_w, w_in, q_norm_w, k_norm_w):
    b, s, d = x.shape
    tm, ta = TOKEN_TILE, ATTN_TILE
    nt, tiles_per_step = s // ta, tm // ta
    inv_freq = ROPE_THETA ** (-jnp.arange(0, ROT_DIM, 2, dtype=F32) / ROT_DIM)
    head_major = jax.ShapeDtypeStruct((b, N_HEADS, nt, V_HEAD_DIM, ta), BF16)
    head_spec = pl.BlockSpec((None, N_HEADS, tiles_per_step, V_HEAD_DIM, ta), lambda bi, ti: (bi, 0, ti, 0, 0))
    ssm_shape = jax.ShapeDtypeStruct((b, s, SSM_WIDTH), BF16)
    ssm_spec = pl.BlockSpec((None, tm, SSM_WIDTH), lambda bi, ti: (bi, ti, 0))
    const2 = lambda bi, ti: (0, 0)
    const3 = lambda bi, ti: (0, 0, 0)
    return pl.pallas_call(
        _in_proj_kernel,
        grid=(b, s // tm),
        in_specs=[
            pl.BlockSpec((None, tm, d), lambda bi, ti: (bi, ti, 0)),
            pl.BlockSpec((1, d), const2),
            pl.BlockSpec(w_in.shape, const2),
            pl.BlockSpec((None, 1, tm), lambda bi, ti: (bi, 0, ti)),
            pl.BlockSpec((ROT_HALF, 1), const2),
            pl.BlockSpec((1, HEAD_DIM, 1), const3),
            pl.BlockSpec((1, HEAD_DIM, 1), const3),
        ],
        out_specs=[
            ssm_spec, ssm_spec,
            head_spec,
            pl.BlockSpec((None, tm, ATTN_WIDTH), lambda bi, ti: (bi, ti, 0)),
            head_spec, head_spec,
        ],
        out_shape=(ssm_shape, ssm_shape, head_major,
                   jax.ShapeDtypeStruct((b, s, ATTN_WIDTH), BF16), head_major, head_major),
        scratch_shapes=[
            pltpu.VMEM((d, 2 * SSM_WIDTH), BF16),
            pltpu.VMEM((4 * ATTN_WIDTH, d), BF16),
        ],
        compiler_params=_compiler_params(2),
        name="in_proj",
    )(x, norm_w.reshape(1, d), w_in, positions.reshape(b, 1, s),
      inv_freq.reshape(ROT_HALF, 1), q_norm_w.reshape(1, HEAD_DIM, 1), k_norm_w.reshape(1, HEAD_DIM, 1))


def _ssm_kernel(u_ref, zs_ref, pin_ref, pout_ref, ab_ref, ca_ref, cd_ref, ar_ref, ai_ref, d_ref,
                gw_ref, gb_ref, y_ref, uf_ref, bu_ref, xs_ref, y2_ref, yg_ref, state_ref,
                bw_ref, cs_ref, tmp_ref):
    n_batch, steps, width = u_ref.shape
    fold, perm_out_rows, perm_in_rows = pin_ref.shape
    perm_steps = perm_in_rows // n_batch
    part_steps = SSM_PART_STEPS
    perms_per_part = part_steps // perm_steps
    part_rows = perms_per_part * perm_out_rows
    n_parts = steps // part_steps
    blk = 2 * BLOCK_STATE
    fw = fold * V7X_LANES
    pack_rows = 2 * V7X_SUBLANES

    @pl.when(pl.program_id(0) == 0)
    def _():
        state_ref[...] = jnp.zeros_like(state_ref)
        bw_ref[...] = jnp.zeros_like(bw_ref)
        for m in range(N_SSM_BLOCKS):
            tmp_ref[...] = jnp.zeros_like(tmp_ref)
            for i in range(fold):
                for gl in range(GROUPS_PER_BLOCK):
                    g = m * GROUPS_PER_BLOCK + gl
                    in_rows = slice(i * V7X_LANES + gl * SSM_GROUP, i * V7X_LANES + (gl + 1) * SSM_GROUP)
                    for ri in range(2):
                        st_cols = slice(ri * BLOCK_STATE + gl * SSM_STATE, ri * BLOCK_STATE + (gl + 1) * SSM_STATE)
                        bw_ref[m, in_rows, st_cols] = ab_ref[i, ri, g].astype(BF16)
                        tmp_ref[in_rows, st_cols] = ca_ref[i, ri, g]
            cs_ref[m] = tmp_ref[...].T.astype(BF16)

    def rows_of(part):
        return slice(part * part_rows, (part + 1) * part_rows)

    def fold_in(part):
        for h in range(perms_per_part):
            t0 = part * part_steps + h * perm_steps
            u_bt = u_ref[:, t0:t0 + perm_steps, :].reshape(perm_in_rows, width)
            r0 = part * part_rows + h * perm_out_rows
            for i in range(fold):
                sel = jnp.dot(pin_ref[i], u_bt, preferred_element_type=F32).astype(BF16)
                for m in range(N_SSM_BLOCKS):
                    uf_ref[r0:r0 + perm_out_rows, m * fw + i * V7X_LANES:m * fw + (i + 1) * V7X_LANES] = (
                        sel[:, m * V7X_LANES:(m + 1) * V7X_LANES])

    def state_block(part, m):
        rows = rows_of(part)
        re = slice(m * blk, m * blk + BLOCK_STATE)
        im = slice(m * blk + BLOCK_STATE, (m + 1) * blk)
        bu_ref[rows, m * blk:(m + 1) * blk] = jnp.dot(
            uf_ref[rows, m * fw:(m + 1) * fw], bw_ref[m], preferred_element_type=F32)
        a_re = jnp.broadcast_to(ar_ref[:, m * BLOCK_STATE:(m + 1) * BLOCK_STATE], (V7X_SUBLANES, BLOCK_STATE))
        a_im = jnp.broadcast_to(ai_ref[:, m * BLOCK_STATE:(m + 1) * BLOCK_STATE], (V7X_SUBLANES, BLOCK_STATE))
        x_re = state_ref[:, re]
        x_im = state_ref[:, im]
        for r0 in range(part * part_rows, (part + 1) * part_rows, pack_rows):
            ins_re, ins_im = [], []
            for half in range(2):
                rr = slice(r0 + half * V7X_SUBLANES, r0 + (half + 1) * V7X_SUBLANES)
                ins_re.append(x_re)
                ins_im.append(x_im)
                n_re = a_re * x_re - a_im * x_im + bu_ref[rr, re]
                n_im = a_re * x_im + a_im * x_re + bu_ref[rr, im]
                x_re, x_im = n_re, n_im
            xs_ref[r0:r0 + pack_rows, re] = jnp.concatenate(ins_re, axis=0).astype(BF16)
            xs_ref[r0:r0 + pack_rows, im] = jnp.concatenate(ins_im, axis=0).astype(BF16)
        state_ref[:, re] = x_re
        state_ref[:, im] = x_im
        y2_ref[rows, m * fw:(m + 1) * fw] = (
            jnp.dot(xs_ref[rows, m * blk:(m + 1) * blk], cs_ref[m], preferred_element_type=F32)
            + jnp.dot(uf_ref[rows, m * fw:(m + 1) * fw], cd_ref[m], preferred_element_type=F32))

    def gate_offset(part, i):
        rows = rows_of(part)
        cols = [slice(m * fw + i * V7X_LANES, m * fw + (i + 1) * V7X_LANES) for m in range(N_SSM_BLOCKS)]
        y = jnp.concatenate([y2_ref[rows, c] for c in cols], axis=-1)
        u_i = jnp.concatenate([uf_ref[rows, c] for c in cols], axis=-1).astype(F32)
        y = _gelu_tanh(y + d_ref[...] * u_i)
        gate = jnp.dot(y.astype(BF16), gw_ref[...], preferred_element_type=F32) + gb_ref[...]
        y = (y * jax.nn.sigmoid(gate)).astype(BF16)
        for h in range(perms_per_part):
            yg_ref[part * perms_per_part + h, i * perm_out_rows:(i + 1) * perm_out_rows, :] = (
                y[h * perm_out_rows:(h + 1) * perm_out_rows])

    def unfold_out(part):
        for h in range(perms_per_part):
            t0 = part * part_steps + h * perm_steps
            y_bt = jnp.dot(pout_ref[...], yg_ref[part * perms_per_part + h], preferred_element_type=F32)
            z = zs_ref[:, t0:t0 + perm_steps, :].reshape(perm_in_rows, width).astype(F32)
            y_ref[:, t0:t0 + perm_steps, :] = (
                (y_bt * jax.nn.silu(z)).astype(BF16).reshape(n_batch, perm_steps, width))

    assert fold == N_SSM_BLOCKS
    for part in range(n_parts + 1):
        if part < n_parts:
            fold_in(part)
        for j in range(N_SSM_BLOCKS):
            if part < n_parts:
                state_block(part, j)
            if part >= 1:
                gate_offset(part - 1, j)
        if part >= 1:
            unfold_out(part - 1)


def _ssm_direct_weights(cl, bbc):
    fold, gpb, nb, p = SSM_FOLD, GROUPS_PER_BLOCK, N_SSM_BLOCKS, SSM_GROUP
    in_lanes = fold * gpb * p
    lags = jnp.einsum('lgpn,gqn->lgqp', cl, bbc, precision=lax.Precision.HIGHEST)
    zero = jnp.zeros_like(lags[0])
    toep = jnp.stack([jnp.stack([lags[i - k] if i >= k else zero for i in range(fold)]) for k in range(fold)])
    toep = toep.reshape(fold, fold, nb, gpb, p, p).transpose(2, 0, 3, 4, 1, 5).reshape(nb * in_lanes, fold * p)
    col = np.arange(in_lanes)
    spread = (np.arange(fold * p)[:, None] == ((col // (gpb * p)) * p + col % p)[None, :]).astype(np.float32)
    in_group = (col // p) % gpb
    same_group = (in_group[:, None] == in_group[None, :]).astype(np.float32)
    cd = jnp.dot(toep, spread, precision=lax.Precision.HIGHEST).reshape(nb, in_lanes, in_lanes)
    return (cd * same_group).astype(BF16)


def _ssm_branch(u, zs, ab, ca, cl, bbc, al, d_skip, glu_w, glu_b):
    n_batch, s, _ = u.shape
    assert n_batch == V7X_SUBLANES, "one time group of all batches must fill one sublane tile"
    fold, steps, perm_steps = SSM_FOLD, SSM_STEPS, SSM_PERM_STEPS
    g_rows = (perm_steps // fold) * n_batch
    n_rows = perm_steps * n_batch
    r = np.arange(g_rows)
    src = (r % n_batch) * perm_steps + fold * (r // n_batch)
    pin = np.arange(n_rows)[None, None, :] == (src[None, :, None] + np.arange(fold)[:, None, None])
    pout = jnp.asarray(pin.reshape(fold * g_rows, n_rows).T, dtype=BF16)
    pin = jnp.asarray(pin, dtype=BF16)
    cd = _ssm_direct_weights(cl, bbc)
    row_spec = pl.BlockSpec((n_batch, steps, SSM_WIDTH), lambda i: (0, i, 0))
    const2 = lambda i: (0, 0)
    const3 = lambda i: (0, 0, 0)
    n_state = SSM_GROUPS * SSM_STATE
    f_rows = (steps // fold) * n_batch
    fw_all = fold * SSM_WIDTH
    return pl.pallas_call(
        _ssm_kernel,
        grid=(s // steps,),
        in_specs=[
            row_spec, row_spec,
            pl.BlockSpec(pin.shape, const3),
            pl.BlockSpec(pout.shape, const2),
            pl.BlockSpec(ab.shape, lambda i: (0, 0, 0, 0, 0)),
            pl.BlockSpec(ca.shape, lambda i: (0, 0, 0, 0, 0)),
            pl.BlockSpec(cd.shape, const3),
            pl.BlockSpec((1, n_state), const2),
            pl.BlockSpec((1, n_state), const2),
            pl.BlockSpec((1, SSM_WIDTH), const2),
            pl.BlockSpec((SSM_WIDTH, SSM_WIDTH), const2),
            pl.BlockSpec((1, SSM_WIDTH), const2),
        ],
        out_specs=row_spec,
        out_shape=jax.ShapeDtypeStruct((n_batch, s, SSM_WIDTH), BF16),
        scratch_shapes=[
            pltpu.VMEM((f_rows, fw_all), BF16),
            pltpu.VMEM((f_rows, 2 * n_state), F32),
            pltpu.VMEM((f_rows, 2 * n_state), BF16),
            pltpu.VMEM((f_rows, fw_all), F32),
            pltpu.VMEM((steps // perm_steps, fold * g_rows, SSM_WIDTH), BF16),
            pltpu.VMEM((V7X_SUBLANES, 2 * n_state), F32),
            pltpu.VMEM((N_SSM_BLOCKS, fold * V7X_LANES, 2 * BLOCK_STATE), BF16),
            pltpu.VMEM((N_SSM_BLOCKS, 2 * BLOCK_STATE, fold * V7X_LANES), BF16),
            pltpu.VMEM((fold * V7X_LANES, 2 * BLOCK_STATE), F32),
        ],
        compiler_params=_compiler_params(1),
        name="ssm_branch",
    )(u, zs, pin, pout, ab, ca, cd, al[0].reshape(1, n_state), al[1].reshape(1, n_state),
      d_skip.reshape(1, SSM_WIDTH), glu_w.astype(BF16), glu_b.reshape(1, SSM_WIDTH))


def _attn_kernel(lambda_init, qt_ref, k_ref, vt_ref, zat_ref, lq1_ref, lk1_ref, lq2_ref, lk2_ref,
                 subw_ref, y_ref, wq_ref, s_ref, mx_ref, acc_ref, m_ref):
    n_tiles, _, tq = qt_ref.shape
    tk = tq
    diag_slot = 2

    def load_queries(q):
        wq_ref[0, :HEAD_DIM] = qt_ref[q, :HEAD_DIM]
        wq_ref[1, HEAD_DIM:] = qt_ref[q, HEAD_DIM:]

    def reset_stats():
        m_ref[...] = jnp.full_like(m_ref, MASK_VALUE)
        acc_ref[...] = jnp.zeros_like(acc_ref)

    def scores(j, slot):
        kb = k_ref[pl.ds(pl.multiple_of(j * tk, tk), tk), :]
        for c in range(2):
            s = jnp.dot(kb, wq_ref[c], preferred_element_type=F32)
            s_ref[slot, c] = s
            mx_ref[slot, c] = jnp.max(s, axis=0, keepdims=True)

    def diagonal_scores(q, slot, fill_masked=False):
        h = tk // 2
        row0 = pl.multiple_of(q * tk, tk)
        k_lo = k_ref[pl.ds(row0, h), :]
        k_hi = k_ref[pl.ds(row0 + h, h), :]
        causal = (lax.broadcasted_iota(jnp.int32, (h, h), 0) <= lax.broadcasted_iota(jnp.int32, (h, h), 1))
        for c in range(2):
            s_lo = jnp.dot(k_lo, wq_ref[c], preferred_element_type=F32)
            s_hi = jnp.dot(k_hi, wq_ref[c, :, h:], preferred_element_type=F32)
            s_ll = jnp.where(causal, s_lo[:, :h], MASK_VALUE)
            s_hh = jnp.where(causal, s_hi, MASK_VALUE)
            s_ref[slot, c, :h, :h] = s_ll
            s_ref[slot, c, :h, h:] = s_lo[:, h:]
            if fill_masked:
                s_ref[slot, c, h:, :h] = jnp.full((h, h), MASK_VALUE, F32)
            s_ref[slot, c, h:, h:] = s_hh
            mx_ref[slot, c, :, :h] = jnp.max(s_ll, axis=0, keepdims=True)
            mx_ref[slot, c, :, h:] = jnp.maximum(jnp.max(s_lo[:, h:], axis=0, keepdims=True),
                                                 jnp.max(s_hh, axis=0, keepdims=True))

    def accumulate(j, slot):
        vb = jnp.concatenate([vt_ref[j], jnp.ones((SUM_ROWS, tk), BF16)], axis=0)
        for c in range(2):
            m_old = m_ref[c]
            m_new = jnp.maximum(m_old, mx_ref[slot, c])
            alpha = jnp.exp2(m_old - m_new)
            p = jnp.exp2(s_ref[slot, c] - m_new)
            acc_ref[c] = alpha * acc_ref[c] + jnp.dot(vb, p.astype(BF16), preferred_element_type=F32)
            m_ref[c] = m_new

    def accumulate_diagonal_first(q, slot):
        h = tk // 2
        vb = jnp.concatenate([vt_ref[q], jnp.ones((SUM_ROWS, tk), BF16)], axis=0)
        for c in range(2):
            m_new = mx_ref[slot, c]
            p_early = jnp.exp2(s_ref[slot, c, :h, :h] - m_new[:, :h])
            p_late = jnp.exp2(s_ref[slot, c, :, h:] - m_new[:, h:])
            acc_ref[c, :, :h] = jnp.dot(vb[:, :h], p_early.astype(BF16), preferred_element_type=F32)
            acc_ref[c, :, h:] = jnp.dot(vb, p_late.astype(BF16), preferred_element_type=F32)
            m_ref[c] = m_new

    def finalize(q):
        lam = (jnp.exp(jnp.sum(lq1_ref[...] * lk1_ref[...], axis=-1, keepdims=True))
               - jnp.exp(jnp.sum(lq2_ref[...] * lk2_ref[...], axis=-1, keepdims=True)) + lambda_init)
        l0 = acc_ref[0, V_HEAD_DIM:V_HEAD_DIM + 1]
        l1 = acc_ref[1, V_HEAD_DIM:V_HEAD_DIM + 1]
        out = acc_ref[0, :V_HEAD_DIM] * (1.0 / l0) - acc_ref[1, :V_HEAD_DIM] * (lam / l1)
        ms = jnp.mean(out * out, axis=0, keepdims=True)
        out = out * (lax.rsqrt(ms + EPS) * (1.0 - lambda_init)) * subw_ref[...]
        out = out * jax.nn.silu(zat_ref[q].astype(F32))
        y_ref[pl.ds(pl.multiple_of(q * tq, tq), tq), :] = out.T.astype(BF16)

    zeros = jnp.zeros((HEAD_DIM, tq), BF16)
    wq_ref[0, HEAD_DIM:] = zeros
    wq_ref[1, :HEAD_DIM] = zeros
    load_queries(0)
    reset_stats()
    diagonal_scores(0, 1, fill_masked=True)

    def enter_tile(q, pending_slot):
        load_queries(q)
        diagonal_scores(q, diag_slot)
        accumulate(jnp.maximum(q - 2, 0), pending_slot)
        finalize(q - 1)
        scores(0, 0)
        accumulate_diagonal_first(q, diag_slot)

    def pair(i, c):
        j = 2 * i
        scores(j + 1, 1)
        accumulate(j, 0)
        scores(j + 2, 0)
        accumulate(j + 1, 1)
        return c

    def two_tiles(k, carry):
        q_odd = 2 * k + 1
        enter_tile(q_odd, 1)
        lax.fori_loop(0, k, pair, 0)
        q_even = q_odd + 1
        enter_tile(q_even, 0)
        lax.fori_loop(0, k, pair, 0)
        scores(q_even - 1, 1)
        accumulate(q_even - 2, 0)
        return carry

    assert n_tiles % 2 == 0
    lax.fori_loop(0, (n_tiles - 2) // 2, two_tiles, 0)
    q_last = n_tiles - 1
    enter_tile(q_last, 1)
    lax.fori_loop(0, (q_last - 1) // 2, pair, 0)
    accumulate(q_last - 1, 0)
    finalize(q_last)


def _attention(qt, k, vt, zat, lq1, lk1, lq2, lk2, subln_w, lambda_init):
    b, _, nt, _, tq = qt.shape
    s = nt * tq
    assert nt >= 2
    row64 = pl.BlockSpec((1, HEAD_DIM), lambda bi, hi: (0, 0))
    head_spec = pl.BlockSpec((None, None, nt, V_HEAD_DIM, tq), lambda bi, hi: (bi, hi, 0, 0, 0))
    token_spec = pl.BlockSpec((None, s, V_HEAD_DIM), lambda bi, hi: (bi, 0, hi))
    return pl.pallas_call(
        functools.partial(_attn_kernel, lambda_init),
        grid=(b, N_HEADS),
        in_specs=[
            head_spec, token_spec, head_spec, head_spec,
            row64, row64, row64, row64,
            pl.BlockSpec((V_HEAD_DIM, 1), lambda bi, hi: (0, 0)),
        ],
        out_specs=token_spec,
        out_shape=jax.ShapeDtypeStruct((b, s, ATTN_WIDTH), BF16),
        scratch_shapes=[
            pltpu.VMEM((2, 2 * HEAD_DIM, tq), BF16),
            pltpu.VMEM((3, 2, tq, tq), F32),
            pltpu.VMEM((3, 2, 1, tq), F32),
            pltpu.VMEM((2, V_HEAD_DIM + SUM_ROWS, tq), F32),
            pltpu.VMEM((2, 1, tq), F32),
        ],
        compiler_params=_compiler_params(2),
        name="diff_attention",
    )(qt, k, vt, zat, lq1.reshape(1, HEAD_DIM), lk1.reshape(1, HEAD_DIM),
      lq2.reshape(1, HEAD_DIM), lk2.reshape(1, HEAD_DIM), subln_w.reshape(V_HEAD_DIM, 1))


def _out_kernel(x_ref, ys_ref, ya_ref, p_ref, wos_ref, woa_ref, wg_ref, wp_ref, o_ref):
    x2 = (x_ref[...]
          + jnp.dot(ys_ref[...], wos_ref[...], preferred_element_type=F32)
          + jnp.dot(ya_ref[...], woa_ref[...], preferred_element_type=F32))
    gate = jax.nn.sigmoid(jnp.dot(x2.astype(BF16), wg_ref[...], preferred_element_type=F32))
    ple = jnp.dot(p_ref[...].astype(BF16), wp_ref[...], preferred_element_type=F32)
    o_ref[...] = x2 + gate * ple


def _out_proj(x, ys, ya, p, w_out, w_proj, w_gate):
    b, s, d = x.shape
    tm = OUT_TOKEN_TILE
    const2 = lambda bi, ti: (0, 0)
    return pl.pallas_call(
        _out_kernel,
        grid=(b, s // tm),
        in_specs=[
            pl.BlockSpec((None, tm, d), lambda bi, ti: (bi, ti, 0)),
            pl.BlockSpec((None, tm, SSM_WIDTH), lambda bi, ti: (bi, ti, 0)),
            pl.BlockSpec((None, tm, ATTN_WIDTH), lambda bi, ti: (bi, ti, 0)),
            pl.BlockSpec((None, tm, PLE_DIM), lambda bi, ti: (bi, ti, 0)),
            pl.BlockSpec((SSM_WIDTH, d), const2),
            pl.BlockSpec((ATTN_WIDTH, d), const2),
            pl.BlockSpec((d, d), const2),
            pl.BlockSpec((PLE_DIM, d), const2),
        ],
        out_specs=pl.BlockSpec((None, tm, d), lambda bi, ti: (bi, ti, 0)),
        out_shape=jax.ShapeDtypeStruct((b, s, d), F32),
        compiler_params=_compiler_params(2),
        name="out_proj",
    )(x, ys, ya, p, w_out[:SSM_WIDTH].astype(BF16), w_out[SSM_WIDTH:].astype(BF16),
      w_gate.astype(BF16), w_proj.astype(BF16))


def kernel(x, p, positions, norm_w, w_in, ssm_lambda_re, ssm_lambda_im, ssm_log_dt, ssm_b_re, ssm_b_im, ssm_c_re, ssm_c_im, ssm_d, glu_w, glu_b, q_norm_w, k_norm_w, lambda_q1, lambda_k1, lambda_q2, lambda_k2, subln_w, w_out, ple_w_proj, ple_w_gate):
    depth = norm_w.shape[0]
    for i in range(depth):
        lambda_init = 0.8 - 0.6 * math.exp(-0.3 * i)
        ab, ca, cl, bbc, al = _ssm_params(ssm_lambda_re[i], ssm_lambda_im[i], ssm_log_dt[i],
                                          ssm_b_re[i], ssm_b_im[i], ssm_c_re[i], ssm_c_im[i])
        u, zs, qt, k, vt, zat = _in_proj(x, positions, norm_w[i], w_in[i], q_norm_w[i], k_norm_w[i])
        ys = _ssm_branch(u, zs, ab, ca, cl, bbc, al, ssm_d[i], glu_w[i], glu_b[i])
        ya = _attention(qt, k, vt, zat, lambda_q1[i], lambda_k1[i], lambda_q2[i], lambda_k2[i],
                        subln_w[i], lambda_init)
        x = _out_proj(x, ys, ya, p[i], w_out[i], ple_w_proj[i], ple_w_gate[i])
    return x
```

```python
import functools
import math

import jax
import jax.numpy as jnp
import numpy as np
from jax import lax
from jax.experimental import pallas as pl
from jax.experimental.pallas import tpu as pltpu

F32 = jnp.float32
BF16 = jnp.bfloat16

D_MODEL = 1024
PLE_DIM = 256
SSM_WIDTH = 512
SSM_GROUP = 16
SSM_GROUPS = 32
SSM_STATE = 64
ATTN_WIDTH = 512
N_HEADS = 4
HEAD_DIM = 64
V_HEAD_DIM = 128
ROT_DIM = 16
ROT_HALF = 8
ROPE_THETA = 500000.0
EPS = 1e-6
LOG2E = math.log2(math.e)

V7X_LANES = 128
V7X_SUBLANES = 8
V7X_VMEM_BYTES = 64 * 1024 * 1024
VMEM_LIMIT_BYTES = V7X_VMEM_BYTES * 7 // 8

TOKEN_TILE = 1024
ATTN_TILE = 512
OUT_TOKEN_TILE = 1024
SSM_FOLD = 4
SSM_STEPS = 256
SSM_PART_STEPS = 128
SSM_PERM_STEPS = 32
GROUPS_PER_BLOCK = 8
N_SSM_BLOCKS = SSM_GROUPS // GROUPS_PER_BLOCK
BLOCK_STATE = GROUPS_PER_BLOCK * SSM_STATE
MASK_VALUE = -1e30
SUM_ROWS = 16


def _gelu_tanh(x):
    c0 = math.sqrt(2.0 / math.pi)
    inner = x * (c0 + (c0 * 0.044715) * (x * x))
    half = 0.5 * x
    return half + half * jnp.tanh(inner)


def _compiler_params(n_grid_axes):
    return pltpu.CompilerParams(
        dimension_semantics=("arbitrary",) * n_grid_axes,
        vmem_limit_bytes=VMEM_LIMIT_BYTES,
    )


def _ssm_params_kernel(lr_ref, li_ref, logdt_ref, br_ref, bi_ref, cr_ref, ci_ref,
                       ab_ref, ca_ref, cl_ref, bbc_ref, al_ref):
    fold = ab_ref.shape[0]
    n = lr_ref.shape[-1]
    lr = lr_ref[...]
    li = li_ref[...]
    dt = jnp.exp(logdt_ref[...])
    mag = jnp.exp(lr * dt)
    a_re = mag * jnp.cos(li * dt)
    a_im = mag * jnp.sin(li * dt)
    nr = a_re - 1.0
    ni = a_im
    den = lr * lr + li * li
    coef_re = (nr * lr + ni * li) / den
    coef_im = (ni * lr - nr * li) / den
    br = br_ref[...]
    bi = bi_ref[...]
    bb_re = coef_re * br - coef_im * bi
    bb_im = coef_re * bi + coef_im * br
    c_re = cr_ref[...]
    c_im = ci_ref[...]

    def cmul(xr, xi, yr, yi):
        return xr * yr - xi * yi, xr * yi + xi * yr

    powers = [(jnp.ones_like(a_re), jnp.zeros_like(a_im))]
    for _ in range(fold):
        powers.append(cmul(*powers[-1], a_re, a_im))
    for i in range(fold):
        ab_ref[i, 0], ab_ref[i, 1] = cmul(*powers[fold - 1 - i], bb_re, bb_im)
        ca_re, ca_im = cmul(c_re, c_im, *powers[i + 1])
        ca_ref[i, 0] = ca_re
        ca_ref[i, 1] = -ca_im
        cl_re, cl_im = cmul(c_re, c_im, *powers[i])
        cl_ref[i, :, :, :n] = cl_re
        cl_ref[i, :, :, n:] = -cl_im
    bbc_ref[:, :, :n] = bb_re
    bbc_ref[:, :, n:] = bb_im
    al_ref[0] = powers[fold][0]
    al_ref[1] = powers[fold][1]


def _ssm_params(lam_re, lam_im, log_dt, b_re, b_im, c_re, c_im):
    g, n, p, fold = SSM_GROUPS, SSM_STATE, SSM_GROUP, SSM_FOLD
    per_offset = jax.ShapeDtypeStruct((fold, 2, g, p, n), F32)
    return pl.pallas_call(
        _ssm_params_kernel,
        out_shape=(per_offset, per_offset,
                   jax.ShapeDtypeStruct((fold, g, p, 2 * n), F32),
                   jax.ShapeDtypeStruct((g, p, 2 * n), F32),
                   jax.ShapeDtypeStruct((2, g, 1, n), F32)),
        name="ssm_params",
    )(lam_re.reshape(g, 1, n), lam_im.reshape(g, 1, n), log_dt.reshape(g, 1, 1),
      jnp.swapaxes(b_re, 1, 2), jnp.swapaxes(b_im, 1, 2), c_re, c_im)


def _in_proj_kernel(x_ref, nw_ref, w_ref, pos_ref, invf_ref, qnw_ref, knw_ref,
                    u_ref, zs_ref, qt_ref, k_ref, vt_ref, zat_ref, wnat_ref, wtr_ref):
    @pl.when((pl.program_id(0) == 0) & (pl.program_id(1) == 0))
    def _():
        n_nat = wnat_ref.shape[1]
        wnat_ref[...] = w_ref[:, :n_nat].astype(BF16)
        chunk = wtr_ref.shape[1] // 2
        for c0 in range(0, wtr_ref.shape[0], chunk):
            wtr_ref[c0:c0 + chunk, :] = w_ref[:, n_nat + c0:n_nat + c0 + chunk].T.astype(BF16)

    x = x_ref[...]
    ms = jnp.mean(x * x, axis=-1, keepdims=True)
    h = (x * lax.rsqrt(ms + EPS) * nw_ref[...]).astype(BF16)
    tm = x.shape[0]

    def channel_major(r0, r1):
        return lax.dot_general(wtr_ref[r0:r1, :], h, (((1,), (1,)), ((), ())), preferred_element_type=F32)

    ang = pos_ref[...].astype(F32) * invf_ref[...]
    cos = jnp.cos(ang)
    sin = jnp.sin(ang)

    def norm_rope(t, w_ref):
        t3 = t.reshape(2 * N_HEADS, HEAD_DIM, tm)
        ms3 = jnp.mean(t3 * t3, axis=1, keepdims=True)
        t3 = t3 * lax.rsqrt(ms3 + EPS) * w_ref[...]
        t1 = t3[:, :ROT_HALF, :]
        t2 = t3[:, ROT_HALF:ROT_DIM, :]
        r1 = t1 * cos - t2 * sin
        r2 = t2 * cos + t1 * sin
        t3 = jnp.concatenate([r1, r2, t3[:, ROT_DIM:, :]], axis=1)
        return t3.reshape(ATTN_WIDTH, tm)

    def store_head_major(ref, t):
        t = t.reshape(N_HEADS, V_HEAD_DIM, tm).astype(BF16)
        ta = ref.shape[-1]
        for j in range(tm // ta):
            ref[:, j] = t[:, :, j * ta:(j + 1) * ta]

    q = norm_rope(channel_major(0, ATTN_WIDTH), qnw_ref) * (HEAD_DIM ** -0.5 * LOG2E)
    store_head_major(qt_ref, q)
    k = norm_rope(channel_major(ATTN_WIDTH, 2 * ATTN_WIDTH), knw_ref)
    k_ref[...] = k.T.astype(BF16)
    store_head_major(vt_ref, channel_major(2 * ATTN_WIDTH, 3 * ATTN_WIDTH))
    store_head_major(zat_ref, channel_major(3 * ATTN_WIDTH, 4 * ATTN_WIDTH))
    nat = jnp.dot(h, wnat_ref[...], preferred_element_type=F32)
    u_ref[...] = nat[:, :SSM_WIDTH].astype(BF16)
    zs_ref[...] = nat[:, SSM_WIDTH:].astype(BF16)


def _in_proj(x, positions, norm_w, w_in, q_norm_w, k_norm_w):
    b, s, d = x.shape
    tm, ta = TOKEN_TILE, ATTN_TILE
    nt, tiles_per_step = s // ta, tm // ta
    inv_freq = ROPE_THETA ** (-jnp.arange(0, ROT_DIM, 2, dtype=F32) / ROT_DIM)
    head_major = jax.ShapeDtypeStruct((b, N_HEADS, nt, V_HEAD_DIM, ta), BF16)
    head_spec = pl.BlockSpec((None, N_HEADS, tiles_per_step, V_HEAD_DIM, ta), lambda bi, ti: (bi, 0, ti, 0, 0))
    ssm_shape = jax.ShapeDtypeStruct((b, s, SSM_WIDTH), BF16)
    ssm_spec = pl.BlockSpec((None, tm, SSM_WIDTH), lambda bi, ti: (bi, ti, 0))
    const2 = lambda bi, ti: (0, 0)
    const3 = lambda bi, ti: (0, 0, 0)
    return pl.pallas_call(
        _in_proj_kernel,
        grid=(b, s // tm),
        in_specs=[
            pl.BlockSpec((None, tm, d), lambda bi, ti: (bi, ti, 0)),
            pl.BlockSpec((1, d), const2),
            pl.BlockSpec(w_in.shape, const2),
            pl.BlockSpec((None, 1, tm), lambda bi, ti: (bi, 0, ti)),
            pl.BlockSpec((ROT_HALF, 1), const2),
            pl.BlockSpec((1, HEAD_DIM, 1), const3),
            pl.BlockSpec((1, HEAD_DIM, 1), const3),
        ],
        out_specs=[
            ssm_spec, ssm_spec,
            head_spec,
            pl.BlockSpec((None, tm, ATTN_WIDTH), lambda bi, ti: (bi, ti, 0)),
            head_spec, head_spec,
        ],
        out_shape=(ssm_shape, ssm_shape, head_major,
                   jax.ShapeDtypeStruct((b, s, ATTN_WIDTH), BF16), head_major, head_major),
        scratch_shapes=[
            pltpu.VMEM((d, 2 * SSM_WIDTH), BF16),
            pltpu.VMEM((4 * ATTN_WIDTH, d), BF16),
        ],
        compiler_params=_compiler_params(2),
        name="in_proj",
    )(x, norm_w.reshape(1, d), w_in, positions.reshape(b, 1, s),
      inv_freq.reshape(ROT_HALF, 1), q_norm_w.reshape(1, HEAD_DIM, 1), k_norm_w.reshape(1, HEAD_DIM, 1))


def _ssm_kernel(u_ref, zs_ref, pin_ref, pout_ref, ab_ref, ca_ref, cd_ref, ar_ref, ai_ref, d_ref,
                gw_ref, gb_ref, y_ref, uf_ref, bu_ref, xs_ref, y2_ref, yg_ref, state_ref,
                bw_ref, cs_ref, tmp_ref):
    n_batch, steps, width = u_ref.shape
    fold, perm_out_rows, perm_in_rows = pin_ref.shape
    perm_steps = perm_in_rows // n_batch
    part_steps = SSM_PART_STEPS
    perms_per_part = part_steps // perm_steps
    part_rows = perms_per_part * perm_out_rows
    n_parts = steps // part_steps
    blk = 2 * BLOCK_STATE
    fw = fold * V7X_LANES
    pack_rows = 2 * V7X_SUBLANES

    @pl.when(pl.program_id(0) == 0)
    def _():
        state_ref[...] = jnp.zeros_like(state_ref)
        bw_ref[...] = jnp.zeros_like(bw_ref)
        for m in range(N_SSM_BLOCKS):
            tmp_ref[...] = jnp.zeros_like(tmp_ref)
            for i in range(fold):
                for gl in range(GROUPS_PER_BLOCK):
                    g = m * GROUPS_PER_BLOCK + gl
                    in_rows = slice(i * V7X_LANES + gl * SSM_GROUP, i * V7X_LANES + (gl + 1) * SSM_GROUP)
                    for ri in range(2):
                        st_cols = slice(ri * BLOCK_STATE + gl * SSM_STATE, ri * BLOCK_STATE + (gl + 1) * SSM_STATE)
                        bw_ref[m, in_rows, st_cols] = ab_ref[i, ri, g].astype(BF16)
                        tmp_ref[in_rows, st_cols] = ca_ref[i, ri, g]
            cs_ref[m] = tmp_ref[...].T.astype(BF16)

    def rows_of(part):
        return slice(part * part_rows, (part + 1) * part_rows)

    def fold_in(part):
        for h in range(perms_per_part):
            t0 = part * part_steps + h * perm_steps
            u_bt = u_ref[:, t0:t0 + perm_steps, :].reshape(perm_in_rows, width)
            r0 = part * part_rows + h * perm_out_rows
            for i in range(fold):
                sel = jnp.dot(pin_ref[i], u_bt, preferred_element_type=F32).astype(BF16)
                for m in range(N_SSM_BLOCKS):
                    uf_ref[r0:r0 + perm_out_rows, m * fw + i * V7X_LANES:m * fw + (i + 1) * V7X_LANES] = (
                        sel[:, m * V7X_LANES:(m + 1) * V7X_LANES])

    def state_block(part, m):
        rows = rows_of(part)
        re = slice(m * blk, m * blk + BLOCK_STATE)
        im = slice(m * blk + BLOCK_STATE, (m + 1) * blk)
        bu_ref[rows, m * blk:(m + 1) * blk] = jnp.dot(
            uf_ref[rows, m * fw:(m + 1) * fw], bw_ref[m], preferred_element_type=F32)
        a_re = jnp.broadcast_to(ar_ref[:, m * BLOCK_STATE:(m + 1) * BLOCK_STATE], (V7X_SUBLANES, BLOCK_STATE))
        a_im = jnp.broadcast_to(ai_ref[:, m * BLOCK_STATE:(m + 1) * BLOCK_STATE], (V7X_SUBLANES, BLOCK_STATE))
        x_re = state_ref[:, re]
        x_im = state_ref[:, im]
        for r0 in range(part * part_rows, (part + 1) * part_rows, pack_rows):
            ins_re, ins_im = [], []
            for half in range(2):
                rr = slice(r0 + half * V7X_SUBLANES, r0 + (half + 1) * V7X_SUBLANES)
                ins_re.append(x_re)
                ins_im.append(x_im)
                n_re = a_re * x_re - a_im * x_im + bu_ref[rr, re]
                n_im = a_re * x_im + a_im * x_re + bu_ref[rr, im]
                x_re, x_im = n_re, n_im
            xs_ref[r0:r0 + pack_rows, re] = jnp.concatenate(ins_re, axis=0).astype(BF16)
            xs_ref[r0:r0 + pack_rows, im] = jnp.concatenate(ins_im, axis=0).astype(BF16)
        state_ref[:, re] = x_re
        state_ref[:, im] = x_im
        y2_ref[rows, m * fw:(m + 1) * fw] = (
            jnp.dot(xs_ref[rows, m * blk:(m + 1) * blk], cs_ref[m], preferred_element_type=F32)
            + jnp.dot(uf_ref[rows, m * fw:(m + 1) * fw], cd_ref[m], preferred_element_type=F32))

    def gate_offset(part, i):
        rows = rows_of(part)
        cols = [slice(m * fw + i * V7X_LANES, m * fw + (i + 1) * V7X_LANES) for m in range(N_SSM_BLOCKS)]
        y = jnp.concatenate([y2_ref[rows, c] for c in cols], axis=-1)
        u_i = jnp.concatenate([uf_ref[rows, c] for c in cols], axis=-1).astype(F32)
        y = _gelu_tanh(y + d_ref[...] * u_i)
        gate = jnp.dot(y.astype(BF16), gw_ref[...], preferred_element_type=F32) + gb_ref[...]
        y = (y * jax.nn.sigmoid(gate)).astype(BF16)
        for h in range(perms_per_part):
            yg_ref[part * perms_per_part + h, i * perm_out_rows:(i + 1) * perm_out_rows, :] = (
                y[h * perm_out_rows:(h + 1) * perm_out_rows])

    def unfold_out(part):
        for h in range(perms_per_part):
            t0 = part * part_steps + h * perm_steps
            y_bt = jnp.dot(pout_ref[...], yg_ref[part * perms_per_part + h], preferred_element_type=F32)
            z = zs_ref[:, t0:t0 + perm_steps, :].reshape(perm_in_rows, width).astype(F32)
            y_ref[:, t0:t0 + perm_steps, :] = (
                (y_bt * jax.nn.silu(z)).astype(BF16).reshape(n_batch, perm_steps, width))

    assert fold == N_SSM_BLOCKS
    for part in range(n_parts + 1):
        if part < n_parts:
            fold_in(part)
        for j in range(N_SSM_BLOCKS):
            if part < n_parts:
                state_block(part, j)
            if part >= 1:
                gate_offset(part - 1, j)
        if part >= 1:
            unfold_out(part - 1)


def _ssm_direct_weights(cl, bbc):
    fold, gpb, nb, p = SSM_FOLD, GROUPS_PER_BLOCK, N_SSM_BLOCKS, SSM_GROUP
    lanes = gpb * p
    hi = lax.Precision.HIGHEST
    lags = jnp.einsum('lgpn,gqn->lgqp', cl, bbc, precision=hi)
    lane = np.arange(lanes)
    spread = (np.arange(p)[:, None] == (lane % p)[None, :]).astype(np.float32)
    same_group = ((lane // p)[:, None] == (lane // p)[None, :]).astype(np.float32)
    bd = jnp.dot(lags.reshape(fold, nb, lanes, p), spread, precision=hi) * same_group
    zero = jnp.zeros_like(bd[0])
    rows = [jnp.concatenate([bd[i - k] if i >= k else zero for i in range(fold)], axis=-1) for k in range(fold)]
    return jnp.concatenate(rows, axis=-2).astype(BF16)


def _ssm_branch(u, zs, ab, ca, cl, bbc, al, d_skip, glu_w, glu_b):
    n_batch, s, _ = u.shape
    assert n_batch == V7X_SUBLANES, "one time group of all batches must fill one sublane tile"
    fold, steps, perm_steps = SSM_FOLD, SSM_STEPS, SSM_PERM_STEPS
    g_rows = (perm_steps // fold) * n_batch
    n_rows = perm_steps * n_batch
    r = np.arange(g_rows)
    src = (r % n_batch) * perm_steps + fold * (r // n_batch)
    pin = np.arange(n_rows)[None, None, :] == (src[None, :, None] + np.arange(fold)[:, None, None])
    pout = jnp.asarray(pin.reshape(fold * g_rows, n_rows).T, dtype=BF16)
    pin = jnp.asarray(pin, dtype=BF16)
    cd = _ssm_direct_weights(cl, bbc)
    row_spec = pl.BlockSpec((n_batch, steps, SSM_WIDTH), lambda i: (0, i, 0))
    const2 = lambda i: (0, 0)
    const3 = lambda i: (0, 0, 0)
    n_state = SSM_GROUPS * SSM_STATE
    f_rows = (steps // fold) * n_batch
    fw_all = fold * SSM_WIDTH
    return pl.pallas_call(
        _ssm_kernel,
        grid=(s // steps,),
        in_specs=[
            row_spec, row_spec,
            pl.BlockSpec(pin.shape, const3),
            pl.BlockSpec(pout.shape, const2),
            pl.BlockSpec(ab.shape, lambda i: (0, 0, 0, 0, 0)),
            pl.BlockSpec(ca.shape, lambda i: (0, 0, 0, 0, 0)),
            pl.BlockSpec(cd.shape, const3),
            pl.BlockSpec((1, n_state), const2),
            pl.BlockSpec((1, n_state), const2),
            pl.BlockSpec((1, SSM_WIDTH), const2),
            pl.BlockSpec((SSM_WIDTH, SSM_WIDTH), const2),
            pl.BlockSpec((1, SSM_WIDTH), const2),
        ],
        out_specs=row_spec,
        out_shape=jax.ShapeDtypeStruct((n_batch, s, SSM_WIDTH), BF16),
        scratch_shapes=[
            pltpu.VMEM((f_rows, fw_all), BF16),
            pltpu.VMEM((f_rows, 2 * n_state), F32),
            pltpu.VMEM((f_rows, 2 * n_state), BF16),
            pltpu.VMEM((f_rows, fw_all), F32),
            pltpu.VMEM((steps // perm_steps, fold * g_rows, SSM_WIDTH), BF16),
            pltpu.VMEM((V7X_SUBLANES, 2 * n_state), F32),
            pltpu.VMEM((N_SSM_BLOCKS, fold * V7X_LANES, 2 * BLOCK_STATE), BF16),
            pltpu.VMEM((N_SSM_BLOCKS, 2 * BLOCK_STATE, fold * V7X_LANES), BF16),
            pltpu.VMEM((fold * V7X_LANES, 2 * BLOCK_STATE), F32),
        ],
        compiler_params=_compiler_params(1),
        name="ssm_branch",
    )(u, zs, pin, pout, ab, ca, cd, al[0].reshape(1, n_state), al[1].reshape(1, n_state),
      d_skip.reshape(1, SSM_WIDTH), glu_w.astype(BF16), glu_b.reshape(1, SSM_WIDTH))


def _attn_kernel(lambda_init, qt_ref, k_ref, vt_ref, zat_ref, lq1_ref, lk1_ref, lq2_ref, lk2_ref,
                 subw_ref, y_ref, wq_ref, s_ref, mx_ref, acc_ref, m_ref):
    n_tiles, _, tq = qt_ref.shape
    tk = tq
    diag_slot = 2

    def load_queries(q):
        wq_ref[0, :HEAD_DIM] = qt_ref[q, :HEAD_DIM]
        wq_ref[1, HEAD_DIM:] = qt_ref[q, HEAD_DIM:]

    def reset_stats():
        m_ref[...] = jnp.full_like(m_ref, MASK_VALUE)
        acc_ref[...] = jnp.zeros_like(acc_ref)

    def scores(j, slot):
        kb = k_ref[pl.ds(pl.multiple_of(j * tk, tk), tk), :]
        for c in range(2):
            s = jnp.dot(kb, wq_ref[c], preferred_element_type=F32)
            s_ref[slot, c] = s
            mx_ref[slot, c] = jnp.max(s, axis=0, keepdims=True)

    def diagonal_scores(q, slot, fill_masked=False):
        h = tk // 2
        row0 = pl.multiple_of(q * tk, tk)
        k_lo = k_ref[pl.ds(row0, h), :]
        k_hi = k_ref[pl.ds(row0 + h, h), :]
        causal = (lax.broadcasted_iota(jnp.int32, (h, h), 0) <= lax.broadcasted_iota(jnp.int32, (h, h), 1))
        for c in range(2):
            s_lo = jnp.dot(k_lo, wq_ref[c], preferred_element_type=F32)
            s_hi = jnp.dot(k_hi, wq_ref[c, :, h:], preferred_element_type=F32)
            s_ll = jnp.where(causal, s_lo[:, :h], MASK_VALUE)
            s_hh = jnp.where(causal, s_hi, MASK_VALUE)
            s_ref[slot, c, :h, :h] = s_ll
            s_ref[slot, c, :h, h:] = s_lo[:, h:]
            if fill_masked:
                s_ref[slot, c, h:, :h] = jnp.full((h, h), MASK_VALUE, F32)
            s_ref[slot, c, h:, h:] = s_hh
            mx_ref[slot, c, :, :h] = jnp.max(s_ll, axis=0, keepdims=True)
            mx_ref[slot, c, :, h:] = jnp.maximum(jnp.max(s_lo[:, h:], axis=0, keepdims=True),
                                                 jnp.max(s_hh, axis=0, keepdims=True))

    def accumulate(j, slot):
        vb = jnp.concatenate([vt_ref[j], jnp.ones((SUM_ROWS, tk), BF16)], axis=0)
        for c in range(2):
            m_old = m_ref[c]
            m_new = jnp.maximum(m_old, mx_ref[slot, c])
            alpha = jnp.exp2(m_old - m_new)
            p = jnp.exp2(s_ref[slot, c] - m_new)
            acc_ref[c] = alpha * acc_ref[c] + jnp.dot(vb, p.astype(BF16), preferred_element_type=F32)
            m_ref[c] = m_new

    def accumulate_diagonal_first(q, slot):
        h = tk // 2
        vb = jnp.concatenate([vt_ref[q], jnp.ones((SUM_ROWS, tk), BF16)], axis=0)
        for c in range(2):
            m_new = mx_ref[slot, c]
            p_early = jnp.exp2(s_ref[slot, c, :h, :h] - m_new[:, :h])
            p_late = jnp.exp2(s_ref[slot, c, :, h:] - m_new[:, h:])
            acc_ref[c, :, :h] = jnp.dot(vb[:, :h], p_early.astype(BF16), preferred_element_type=F32)
            acc_ref[c, :, h:] = jnp.dot(vb, p_late.astype(BF16), preferred_element_type=F32)
            m_ref[c] = m_new

    def finalize(q):
        lam = (jnp.exp(jnp.sum(lq1_ref[...] * lk1_ref[...], axis=-1, keepdims=True))
               - jnp.exp(jnp.sum(lq2_ref[...] * lk2_ref[...], axis=-1, keepdims=True)) + lambda_init)
        l0 = acc_ref[0, V_HEAD_DIM:V_HEAD_DIM + 1]
        l1 = acc_ref[1, V_HEAD_DIM:V_HEAD_DIM + 1]
        out = acc_ref[0, :V_HEAD_DIM] * (1.0 / l0) - acc_ref[1, :V_HEAD_DIM] * (lam / l1)
        ms = jnp.mean(out * out, axis=0, keepdims=True)
        out = out * (lax.rsqrt(ms + EPS) * (1.0 - lambda_init)) * subw_ref[...]
        out = out * jax.nn.silu(zat_ref[q].astype(F32))
        y_ref[pl.ds(pl.multiple_of(q * tq, tq), tq), :] = out.T.astype(BF16)

    zeros = jnp.zeros((HEAD_DIM, tq), BF16)
    wq_ref[0, HEAD_DIM:] = zeros
    wq_ref[1, :HEAD_DIM] = zeros
    load_queries(0)
    reset_stats()
    diagonal_scores(0, 1, fill_masked=True)

    def enter_tile(q, pending_slot):
        load_queries(q)
        diagonal_scores(q, diag_slot)
        accumulate(jnp.maximum(q - 2, 0), pending_slot)
        finalize(q - 1)
        scores(0, 0)
        accumulate_diagonal_first(q, diag_slot)

    def pair(i, c):
        j = 2 * i
        scores(j + 1, 1)
        accumulate(j, 0)
        scores(j + 2, 0)
        accumulate(j + 1, 1)
        return c

    def two_tiles(k, carry):
        q_odd = 2 * k + 1
        enter_tile(q_odd, 1)
        lax.fori_loop(0, k, pair, 0)
        q_even = q_odd + 1
        enter_tile(q_even, 0)
        lax.fori_loop(0, k, pair, 0)
        scores(q_even - 1, 1)
        accumulate(q_even - 2, 0)
        return carry

    assert n_tiles % 2 == 0
    lax.fori_loop(0, (n_tiles - 2) // 2, two_tiles, 0)
    q_last = n_tiles - 1
    enter_tile(q_last, 1)
    lax.fori_loop(0, (q_last - 1) // 2, pair, 0)
    accumulate(q_last - 1, 0)
    finalize(q_last)


def _attention(qt, k, vt, zat, lq1, lk1, lq2, lk2, subln_w, lambda_init):
    b, _, nt, _, tq = qt.shape
    s = nt * tq
    assert nt >= 2
    row64 = pl.BlockSpec((1, HEAD_DIM), lambda bi, hi: (0, 0))
    head_spec = pl.BlockSpec((None, None, nt, V_HEAD_DIM, tq), lambda bi, hi: (bi, hi, 0, 0, 0))
    token_spec = pl.BlockSpec((None, s, V_HEAD_DIM), lambda bi, hi: (bi, 0, hi))
    return pl.pallas_call(
        functools.partial(_attn_kernel, lambda_init),
        grid=(b, N_HEADS),
        in_specs=[
            head_spec, token_spec, head_spec, head_spec,
            row64, row64, row64, row64,
            pl.BlockSpec((V_HEAD_DIM, 1), lambda bi, hi: (0, 0)),
        ],
        out_specs=token_spec,
        out_shape=jax.ShapeDtypeStruct((b, s, ATTN_WIDTH), BF16),
        scratch_shapes=[
            pltpu.VMEM((2, 2 * HEAD_DIM, tq), BF16),
            pltpu.VMEM((3, 2, tq, tq), F32),
            pltpu.VMEM((3, 2, 1, tq), F32),
            pltpu.VMEM((2, V_HEAD_DIM + SUM_ROWS, tq), F32),
            pltpu.VMEM((2, 1, tq), F32),
        ],
        compiler_params=_compiler_params(2),
        name="diff_attention",
    )(qt, k, vt, zat, lq1.reshape(1, HEAD_DIM), lk1.reshape(1, HEAD_DIM),
      lq2.reshape(1, HEAD_DIM), lk2.reshape(1, HEAD_DIM), subln_w.reshape(V_HEAD_DIM, 1))


def _out_kernel(x_ref, ys_ref, ya_ref, p_ref, wos_ref, woa_ref, wg_ref, wp_ref, o_ref):
    x2 = (x_ref[...]
          + jnp.dot(ys_ref[...], wos_ref[...], preferred_element_type=F32)
          + jnp.dot(ya_ref[...], woa_ref[...], preferred_element_type=F32))
    gate = jax.nn.sigmoid(jnp.dot(x2.astype(BF16), wg_ref[...], preferred_element_type=F32))
    ple = jnp.dot(p_ref[...].astype(BF16), wp_ref[...], preferred_element_type=F32)
    o_ref[...] = x2 + gate * ple


def _out_proj(x, ys, ya, p, w_out, w_proj, w_gate):
    b, s, d = x.shape
    tm = OUT_TOKEN_TILE
    const2 = lambda bi, ti: (0, 0)
    return pl.pallas_call(
        _out_kernel,
        grid=(b, s // tm),
        in_specs=[
            pl.BlockSpec((None, tm, d), lambda bi, ti: (bi, ti, 0)),
            pl.BlockSpec((None, tm, SSM_WIDTH), lambda bi, ti: (bi, ti, 0)),
            pl.BlockSpec((None, tm, ATTN_WIDTH), lambda bi, ti: (bi, ti, 0)),
            pl.BlockSpec((None, tm, PLE_DIM), lambda bi, ti: (bi, ti, 0)),
            pl.BlockSpec((SSM_WIDTH, d), const2),
            pl.BlockSpec((ATTN_WIDTH, d), const2),
            pl.BlockSpec((d, d), const2),
            pl.BlockSpec((PLE_DIM, d), const2),
        ],
        out_specs=pl.BlockSpec((None, tm, d), lambda bi, ti: (bi, ti, 0)),
        out_shape=jax.ShapeDtypeStruct((b, s, d), F32),
        compiler_params=_compiler_params(2),
        name="out_proj",
    )(x, ys, ya, p, w_out[:SSM_WIDTH].astype(BF16), w_out[SSM_WIDTH:].astype(BF16),
      w_gate.astype(BF16), w_proj.astype(BF16))


def kernel(x, p, positions, norm_w, w_in, ssm_lambda_re, ssm_lambda_im, ssm_log_dt, ssm_b_re, ssm_b_im, ssm_c_re, ssm_c_im, ssm_d, glu_w, glu_b, q_norm_w, k_norm_w, lambda_q1, lambda_k1, lambda_q2, lambda_k2, subln_w, w_out, ple_w_proj, ple_w_gate):
    depth = norm_w.shape[0]
    for i in range(depth):
        lambda_init = 0.8 - 0.6 * math.exp(-0.3 * i)
        ab, ca, cl, bbc, al = _ssm_params(ssm_lambda_re[i], ssm_lambda_im[i], ssm_log_dt[i],
                                          ssm_b_re[i], ssm_b_im[i], ssm_c_re[i], ssm_c_im[i])
        u, zs, qt, k, vt, zat = _in_proj(x, positions, norm_w[i], w_in[i], q_norm_w[i], k_norm_w[i])
        ys = _ssm_branch(u, zs, ab, ca, cl, bbc, al, ssm_d[i], glu_w[i], glu_b[i])
        ya = _attention(qt, k, vt, zat, lambda_q1[i], lambda_k1[i], lambda_q2[i], lambda_k2[i],
                        subln_w[i], lambda_init)
        x = _out_proj(x, ys, ya, p[i], w_out[i], ple_w_proj[i], ple_w_gate[i])
    return x
```

```python
import functools
import math

import jax
import jax.numpy as jnp
import numpy as np
from jax import lax
from jax.experimental import pallas as pl
from jax.experimental.pallas import tpu as pltpu

F32 = jnp.float32
BF16 = jnp.bfloat16

D_MODEL = 1024
PLE_DIM = 256
SSM_WIDTH = 512
SSM_GROUP = 16
SSM_GROUPS = 32
SSM_STATE = 64
ATTN_WIDTH = 512
N_HEADS = 4
HEAD_DIM = 64
V_HEAD_DIM = 128
ROT_DIM = 16
ROT_HALF = 8
ROPE_THETA = 500000.0
EPS = 1e-6
LOG2E = math.log2(math.e)

V7X_LANES = 128
V7X_SUBLANES = 8
V7X_VMEM_BYTES = 64 * 1024 * 1024
VMEM_LIMIT_BYTES = V7X_VMEM_BYTES * 7 // 8

TOKEN_TILE = 1024
ATTN_TILE = 512
OUT_TOKEN_TILE = 1024
SSM_FOLD = 4
SSM_STEPS = 256
SSM_PART_STEPS = 256
SSM_PERM_STEPS = 32
GROUPS_PER_BLOCK = 8
N_SSM_BLOCKS = SSM_GROUPS // GROUPS_PER_BLOCK
BLOCK_STATE = GROUPS_PER_BLOCK * SSM_STATE
MASK_VALUE = -1e30
SUM_ROWS = 16


def _gelu_tanh(x):
    c0 = math.sqrt(2.0 / math.pi)
    inner = x * (c0 + (c0 * 0.044715) * (x * x))
    half = 0.5 * x
    return half + half * jnp.tanh(inner)


def _compiler_params(n_grid_axes):
    return pltpu.CompilerParams(
        dimension_semantics=("arbitrary",) * n_grid_axes,
        vmem_limit_bytes=VMEM_LIMIT_BYTES,
    )


def _ssm_params_kernel(lr_ref, li_ref, logdt_ref, br_ref, bi_ref, cr_ref, ci_ref,
                       ab_ref, ca_ref, cl_ref, bbc_ref, al_ref):
    fold = ab_ref.shape[0]
    n = lr_ref.shape[-1]
    lr = lr_ref[...]
    li = li_ref[...]
    dt = jnp.exp(logdt_ref[...])
    mag = jnp.exp(lr * dt)
    a_re = mag * jnp.cos(li * dt)
    a_im = mag * jnp.sin(li * dt)
    nr = a_re - 1.0
    ni = a_im
    den = lr * lr + li * li
    coef_re = (nr * lr + ni * li) / den
    coef_im = (ni * lr - nr * li) / den
    br = br_ref[...]
    bi = bi_ref[...]
    bb_re = coef_re * br - coef_im * bi
    bb_im = coef_re * bi + coef_im * br
    c_re = cr_ref[...]
    c_im = ci_ref[...]

    def cmul(xr, xi, yr, yi):
        return xr * yr - xi * yi, xr * yi + xi * yr

    powers = [(jnp.ones_like(a_re), jnp.zeros_like(a_im))]
    for _ in range(fold):
        powers.append(cmul(*powers[-1], a_re, a_im))
    for i in range(fold):
        ab_ref[i, 0], ab_ref[i, 1] = cmul(*powers[fold - 1 - i], bb_re, bb_im)
        ca_re, ca_im = cmul(c_re, c_im, *powers[i + 1])
        ca_ref[i, 0] = ca_re
        ca_ref[i, 1] = -ca_im
        cl_re, cl_im = cmul(c_re, c_im, *powers[i])
        cl_ref[i, :, :, :n] = cl_re
        cl_ref[i, :, :, n:] = -cl_im
    bbc_ref[:, :, :n] = bb_re
    bbc_ref[:, :, n:] = bb_im
    al_ref[0] = powers[fold][0]
    al_ref[1] = powers[fold][1]


def _ssm_params(lam_re, lam_im, log_dt, b_re, b_im, c_re, c_im):
    g, n, p, fold = SSM_GROUPS, SSM_STATE, SSM_GROUP, SSM_FOLD
    per_offset = jax.ShapeDtypeStruct((fold, 2, g, p, n), F32)
    return pl.pallas_call(
        _ssm_params_kernel,
        out_shape=(per_offset, per_offset,
                   jax.ShapeDtypeStruct((fold, g, p, 2 * n), F32),
                   jax.ShapeDtypeStruct((g, p, 2 * n), F32),
                   jax.ShapeDtypeStruct((2, g, 1, n), F32)),
        name="ssm_params",
    )(lam_re.reshape(g, 1, n), lam_im.reshape(g, 1, n), log_dt.reshape(g, 1, 1),
      jnp.swapaxes(b_re, 1, 2), jnp.swapaxes(b_im, 1, 2), c_re, c_im)


def _in_proj_kernel(x_ref, nw_ref, w_ref, pos_ref, invf_ref, qnw_ref, knw_ref,
                    u_ref, zs_ref, qt_ref, k_ref, vt_ref, zat_ref, wnat_ref, wtr_ref):
    @pl.when((pl.program_id(0) == 0) & (pl.program_id(1) == 0))
    def _():
        n_nat = wnat_ref.shape[1]
        wnat_ref[...] = w_ref[:, :n_nat].astype(BF16)
        chunk = wtr_ref.shape[1] // 2
        for c0 in range(0, wtr_ref.shape[0], chunk):
            wtr_ref[c0:c0 + chunk, :] = w_ref[:, n_nat + c0:n_nat + c0 + chunk].T.astype(BF16)

    x = x_ref[...]
    ms = jnp.mean(x * x, axis=-1, keepdims=True)
    h = (x * lax.rsqrt(ms + EPS) * nw_ref[...]).astype(BF16)
    tm = x.shape[0]

    def channel_major(r0, r1):
        return lax.dot_general(wtr_ref[r0:r1, :], h, (((1,), (1,)), ((), ())), preferred_element_type=F32)

    ang = pos_ref[...].astype(F32) * invf_ref[...]
    cos = jnp.cos(ang)
    sin = jnp.sin(ang)

    def norm_rope(t, w_ref):
        t3 = t.reshape(2 * N_HEADS, HEAD_DIM, tm)
        ms3 = jnp.mean(t3 * t3, axis=1, keepdims=True)
        t3 = t3 * lax.rsqrt(ms3 + EPS) * w_ref[...]
        t1 = t3[:, :ROT_HALF, :]
        t2 = t3[:, ROT_HALF:ROT_DIM, :]
        r1 = t1 * cos - t2 * sin
        r2 = t2 * cos + t1 * sin
        t3 = jnp.concatenate([r1, r2, t3[:, ROT_DIM:, :]], axis=1)
        return t3.reshape(ATTN_WIDTH, tm)

    def store_head_major(ref, t):
        t = t.reshape(N_HEADS, V_HEAD_DIM, tm).astype(BF16)
        ta = ref.shape[-1]
        for j in range(tm // ta):
            ref[:, j] = t[:, :, j * ta:(j + 1) * ta]

    q = norm_rope(channel_major(0, ATTN_WIDTH), qnw_ref) * (HEAD_DIM ** -0.5 * LOG2E)
    store_head_major(qt_ref, q)
    k = norm_rope(channel_major(ATTN_WIDTH, 2 * ATTN_WIDTH), knw_ref)
    k_ref[...] = k.T.astype(BF16)
    store_head_major(vt_ref, channel_major(2 * ATTN_WIDTH, 3 * ATTN_WIDTH))
    store_head_major(zat_ref, channel_major(3 * ATTN_WIDTH, 4 * ATTN_WIDTH))
    nat = jnp.dot(h, wnat_ref[...], preferred_element_type=F32)
    u_ref[...] = nat[:, :SSM_WIDTH].astype(BF16)
    zs_ref[...] = nat[:, SSM_WIDTH:].astype(BF16)


def _in_proj(x, positions, norm_w, w_in, q_norm_w, k_norm_w):
    b, s, d = x.shape
    tm, ta = TOKEN_TILE, ATTN_TILE
    nt, tiles_per_step = s // ta, tm // ta
    inv_freq = ROPE_THETA ** (-jnp.arange(0, ROT_DIM, 2, dtype=F32) / ROT_DIM)
    head_major = jax.ShapeDtypeStruct((b, N_HEADS, nt, V_HEAD_DIM, ta), BF16)
    head_spec = pl.BlockSpec((None, N_HEADS, tiles_per_step, V_HEAD_DIM, ta), lambda bi, ti: (bi, 0, ti, 0, 0))
    ssm_shape = jax.ShapeDtypeStruct((b, s, SSM_WIDTH), BF16)
    ssm_spec = pl.BlockSpec((None, tm, SSM_WIDTH), lambda bi, ti: (bi, ti, 0))
    const2 = lambda bi, ti: (0, 0)
    const3 = lambda bi, ti: (0, 0, 0)
    return pl.pallas_call(
        _in_proj_kernel,
        grid=(b, s // tm),
        in_specs=[
            pl.BlockSpec((None, tm, d), lambda bi, ti: (bi, ti, 0)),
            pl.BlockSpec((1, d), const2),
            pl.BlockSpec(w_in.shape, const2),
            pl.BlockSpec((None, 1, tm), lambda bi, ti: (bi, 0, ti)),
            pl.BlockSpec((ROT_HALF, 1), const2),
            pl.BlockSpec((1, HEAD_DIM, 1), const3),
            pl.BlockSpec((1, HEAD_DIM, 1), const3),
        ],
        out_specs=[
            ssm_spec, ssm_spec,
            head_spec,
            pl.BlockSpec((None, tm, ATTN_WIDTH), lambda bi, ti: (bi, ti, 0)),
            head_spec, head_spec,
        ],
        out_shape=(ssm_shape, ssm_shape, head_major,
                   jax.ShapeDtypeStruct((b, s, ATTN_WIDTH), BF16), head_major, head_major),
        scratch_shapes=[
            pltpu.VMEM((d, 2 * SSM_WIDTH), BF16),
            pltpu.VMEM((4 * ATTN_WIDTH, d), BF16),
        ],
        compiler_params=_compiler_params(2),
        name="in_proj",
    )(x, norm_w.reshape(1, d), w_in, positions.reshape(b, 1, s),
      inv_freq.reshape(ROT_HALF, 1), q_norm_w.reshape(1, HEAD_DIM, 1), k_norm_w.reshape(1, HEAD_DIM, 1))


def _ssm_kernel(u_ref, zs_ref, pin_ref, pout_ref, ab_ref, ca_ref, cd_ref, ar_ref, ai_ref, d_ref,
                gw_ref, gb_ref, y_ref, uf_ref, bu_ref, xs_ref, y2_ref, yg_ref, state_ref,
                bw_ref, cs_ref, tmp_ref):
    n_batch, steps, width = u_ref.shape
    fold, perm_out_rows, perm_in_rows = pin_ref.shape
    perm_steps = perm_in_rows // n_batch
    part_steps = SSM_PART_STEPS
    perms_per_part = part_steps // perm_steps
    part_rows = perms_per_part * perm_out_rows
    n_parts = steps // part_steps
    blk = 2 * BLOCK_STATE
    fw = fold * V7X_LANES
    pack_rows = 2 * V7X_SUBLANES

    @pl.when(pl.program_id(0) == 0)
    def _():
        state_ref[...] = jnp.zeros_like(state_ref)
        bw_ref[...] = jnp.zeros_like(bw_ref)
        for m in range(N_SSM_BLOCKS):
            tmp_ref[...] = jnp.zeros_like(tmp_ref)
            for i in range(fold):
                for gl in range(GROUPS_PER_BLOCK):
                    g = m * GROUPS_PER_BLOCK + gl
                    in_rows = slice(i * V7X_LANES + gl * SSM_GROUP, i * V7X_LANES + (gl + 1) * SSM_GROUP)
                    for ri in range(2):
                        st_cols = slice(ri * BLOCK_STATE + gl * SSM_STATE, ri * BLOCK_STATE + (gl + 1) * SSM_STATE)
                        bw_ref[m, in_rows, st_cols] = ab_ref[i, ri, g].astype(BF16)
                        tmp_ref[in_rows, st_cols] = ca_ref[i, ri, g]
            cs_ref[m] = tmp_ref[...].T.astype(BF16)

    def rows_of(part):
        return slice(part * part_rows, (part + 1) * part_rows)

    def fold_in(part):
        for h in range(perms_per_part):
            t0 = part * part_steps + h * perm_steps
            u_bt = u_ref[:, t0:t0 + perm_steps, :].reshape(perm_in_rows, width)
            r0 = part * part_rows + h * perm_out_rows
            for i in range(fold):
                sel = jnp.dot(pin_ref[i], u_bt, preferred_element_type=F32).astype(BF16)
                for m in range(N_SSM_BLOCKS):
                    uf_ref[r0:r0 + perm_out_rows, m * fw + i * V7X_LANES:m * fw + (i + 1) * V7X_LANES] = (
                        sel[:, m * V7X_LANES:(m + 1) * V7X_LANES])

    def state_block(part, m):
        rows = rows_of(part)
        re = slice(m * blk, m * blk + BLOCK_STATE)
        im = slice(m * blk + BLOCK_STATE, (m + 1) * blk)
        bu_ref[rows, m * blk:(m + 1) * blk] = jnp.dot(
            uf_ref[rows, m * fw:(m + 1) * fw], bw_ref[m], preferred_element_type=F32)
        a_re = jnp.broadcast_to(ar_ref[:, m * BLOCK_STATE:(m + 1) * BLOCK_STATE], (V7X_SUBLANES, BLOCK_STATE))
        a_im = jnp.broadcast_to(ai_ref[:, m * BLOCK_STATE:(m + 1) * BLOCK_STATE], (V7X_SUBLANES, BLOCK_STATE))
        x_re = state_ref[:, re]
        x_im = state_ref[:, im]
        for r0 in range(part * part_rows, (part + 1) * part_rows, pack_rows):
            ins_re, ins_im = [], []
            for half in range(2):
                rr = slice(r0 + half * V7X_SUBLANES, r0 + (half + 1) * V7X_SUBLANES)
                ins_re.append(x_re)
                ins_im.append(x_im)
                n_re = a_re * x_re - a_im * x_im + bu_ref[rr, re]
                n_im = a_re * x_im + a_im * x_re + bu_ref[rr, im]
                x_re, x_im = n_re, n_im
            xs_ref[r0:r0 + pack_rows, re] = jnp.concatenate(ins_re, axis=0).astype(BF16)
            xs_ref[r0:r0 + pack_rows, im] = jnp.concatenate(ins_im, axis=0).astype(BF16)
        state_ref[:, re] = x_re
        state_ref[:, im] = x_im
        y2_ref[rows, m * fw:(m + 1) * fw] = (
            jnp.dot(xs_ref[rows, m * blk:(m + 1) * blk], cs_ref[m], preferred_element_type=F32)
            + jnp.dot(uf_ref[rows, m * fw:(m + 1) * fw], cd_ref[m], preferred_element_type=F32))

    def gate_offset(part, i):
        rows = rows_of(part)
        cols = [slice(m * fw + i * V7X_LANES, m * fw + (i + 1) * V7X_LANES) for m in range(N_SSM_BLOCKS)]
        y = jnp.concatenate([y2_ref[rows, c] for c in cols], axis=-1)
        u_i = jnp.concatenate([uf_ref[rows, c] for c in cols], axis=-1).astype(F32)
        y = _gelu_tanh(y + d_ref[...] * u_i)
        gate = jnp.dot(y.astype(BF16), gw_ref[...], preferred_element_type=F32) + gb_ref[...]
        y = (y * jax.nn.sigmoid(gate)).astype(BF16)
        for h in range(perms_per_part):
            yg_ref[part * perms_per_part + h, i * perm_out_rows:(i + 1) * perm_out_rows, :] = (
                y[h * perm_out_rows:(h + 1) * perm_out_rows])

    def unfold_out(part):
        for h in range(perms_per_part):
            t0 = part * part_steps + h * perm_steps
            y_bt = jnp.dot(pout_ref[...], yg_ref[part * perms_per_part + h], preferred_element_type=F32)
            z = zs_ref[:, t0:t0 + perm_steps, :].reshape(perm_in_rows, width).astype(F32)
            y_ref[:, t0:t0 + perm_steps, :] = (
                (y_bt * jax.nn.silu(z)).astype(BF16).reshape(n_batch, perm_steps, width))

    assert fold == N_SSM_BLOCKS
    for part in range(n_parts + 1):
        if part < n_parts:
            fold_in(part)
        for j in range(N_SSM_BLOCKS):
            if part < n_parts:
                state_block(part, j)
            if part >= 1:
                gate_offset(part - 1, j)
        if part >= 1:
            unfold_out(part - 1)


def _ssm_direct_weights(cl, bbc):
    fold, gpb, nb, p = SSM_FOLD, GROUPS_PER_BLOCK, N_SSM_BLOCKS, SSM_GROUP
    lanes = gpb * p
    hi = lax.Precision.HIGHEST
    lags = jnp.einsum('lgpn,gqn->lgqp', cl, bbc, precision=hi)
    lane = np.arange(lanes)
    spread = (np.arange(p)[:, None] == (lane % p)[None, :]).astype(np.float32)
    same_group = ((lane // p)[:, None] == (lane // p)[None, :]).astype(np.float32)
    bd = jnp.dot(lags.reshape(fold, nb, lanes, p), spread, precision=hi) * same_group
    zero = jnp.zeros_like(bd[0])
    rows = [jnp.concatenate([bd[i - k] if i >= k else zero for i in range(fold)], axis=-1) for k in range(fold)]
    return jnp.concatenate(rows, axis=-2).astype(BF16)


def _ssm_branch(u, zs, ab, ca, cl, bbc, al, d_skip, glu_w, glu_b):
    n_batch, s, _ = u.shape
    assert n_batch == V7X_SUBLANES, "one time group of all batches must fill one sublane tile"
    fold, steps, perm_steps = SSM_FOLD, SSM_STEPS, SSM_PERM_STEPS
    g_rows = (perm_steps // fold) * n_batch
    n_rows = perm_steps * n_batch
    r = np.arange(g_rows)
    src = (r % n_batch) * perm_steps + fold * (r // n_batch)
    pin = np.arange(n_rows)[None, None, :] == (src[None, :, None] + np.arange(fold)[:, None, None])
    pout = jnp.asarray(pin.reshape(fold * g_rows, n_rows).T, dtype=BF16)
    pin = jnp.asarray(pin, dtype=BF16)
    cd = _ssm_direct_weights(cl, bbc)
    row_spec = pl.BlockSpec((n_batch, steps, SSM_WIDTH), lambda i: (0, i, 0))
    const2 = lambda i: (0, 0)
    const3 = lambda i: (0, 0, 0)
    n_state = SSM_GROUPS * SSM_STATE
    f_rows = (steps // fold) * n_batch
    fw_all = fold * SSM_WIDTH
    return pl.pallas_call(
        _ssm_kernel,
        grid=(s // steps,),
        in_specs=[
            row_spec, row_spec,
            pl.BlockSpec(pin.shape, const3),
            pl.BlockSpec(pout.shape, const2),
            pl.BlockSpec(ab.shape, lambda i: (0, 0, 0, 0, 0)),
            pl.BlockSpec(ca.shape, lambda i: (0, 0, 0, 0, 0)),
            pl.BlockSpec(cd.shape, const3),
            pl.BlockSpec((1, n_state), const2),
            pl.BlockSpec((1, n_state), const2),
            pl.BlockSpec((1, SSM_WIDTH), const2),
            pl.BlockSpec((SSM_WIDTH, SSM_WIDTH), const2),
            pl.BlockSpec((1, SSM_WIDTH), const2),
        ],
        out_specs=row_spec,
        out_shape=jax.ShapeDtypeStruct((n_batch, s, SSM_WIDTH), BF16),
        scratch_shapes=[
            pltpu.VMEM((f_rows, fw_all), BF16),
            pltpu.VMEM((f_rows, 2 * n_state), F32),
            pltpu.VMEM((f_rows, 2 * n_state), BF16),
            pltpu.VMEM((f_rows, fw_all), F32),
            pltpu.VMEM((steps // perm_steps, fold * g_rows, SSM_WIDTH), BF16),
            pltpu.VMEM((V7X_SUBLANES, 2 * n_state), F32),
            pltpu.VMEM((N_SSM_BLOCKS, fold * V7X_LANES, 2 * BLOCK_STATE), BF16),
            pltpu.VMEM((N_SSM_BLOCKS, 2 * BLOCK_STATE, fold * V7X_LANES), BF16),
            pltpu.VMEM((fold * V7X_LANES, 2 * BLOCK_STATE), F32),
        ],
        compiler_params=_compiler_params(1),
        name="ssm_branch",
    )(u, zs, pin, pout, ab, ca, cd, al[0].reshape(1, n_state), al[1].reshape(1, n_state),
      d_skip.reshape(1, SSM_WIDTH), glu_w.astype(BF16), glu_b.reshape(1, SSM_WIDTH))


def _attn_kernel(lambda_init, qt_ref, k_ref, vt_ref, zat_ref, lq1_ref, lk1_ref, lq2_ref, lk2_ref,
                 subw_ref, y_ref, wq_ref, s_ref, mx_ref, acc_ref, m_ref):
    n_tiles, _, tq = qt_ref.shape
    tk = tq
    diag_slot = 2

    def load_queries(q):
        wq_ref[0, :HEAD_DIM] = qt_ref[q, :HEAD_DIM]
        wq_ref[1, HEAD_DIM:] = qt_ref[q, HEAD_DIM:]

    def reset_stats():
        m_ref[...] = jnp.full_like(m_ref, MASK_VALUE)
        acc_ref[...] = jnp.zeros_like(acc_ref)

    def scores(j, slot):
        kb = k_ref[pl.ds(pl.multiple_of(j * tk, tk), tk), :]
        for c in range(2):
            s = jnp.dot(kb, wq_ref[c], preferred_element_type=F32)
            s_ref[slot, c] = s
            mx_ref[slot, c] = jnp.max(s, axis=0, keepdims=True)

    def diagonal_scores(q, slot, fill_masked=False):
        h = tk // 2
        row0 = pl.multiple_of(q * tk, tk)
        k_lo = k_ref[pl.ds(row0, h), :]
        k_hi = k_ref[pl.ds(row0 + h, h), :]
        causal = (lax.broadcasted_iota(jnp.int32, (h, h), 0) <= lax.broadcasted_iota(jnp.int32, (h, h), 1))
        for c in range(2):
            s_lo = jnp.dot(k_lo, wq_ref[c], preferred_element_type=F32)
            s_hi = jnp.dot(k_hi, wq_ref[c, :, h:], preferred_element_type=F32)
            s_ll = jnp.where(causal, s_lo[:, :h], MASK_VALUE)
            s_hh = jnp.where(causal, s_hi, MASK_VALUE)
            s_ref[slot, c, :h, :h] = s_ll
            s_ref[slot, c, :h, h:] = s_lo[:, h:]
            if fill_masked:
                s_ref[slot, c, h:, :h] = jnp.full((h, h), MASK_VALUE, F32)
            s_ref[slot, c, h:, h:] = s_hh
            mx_ref[slot, c, :, :h] = jnp.max(s_ll, axis=0, keepdims=True)
            mx_ref[slot, c, :, h:] = jnp.maximum(jnp.max(s_lo[:, h:], axis=0, keepdims=True),
                                                 jnp.max(s_hh, axis=0, keepdims=True))

    def accumulate(j, slot):
        vb = jnp.concatenate([vt_ref[j], jnp.ones((SUM_ROWS, tk), BF16)], axis=0)
        for c in range(2):
            m_old = m_ref[c]
            m_new = jnp.maximum(m_old, mx_ref[slot, c])
            alpha = jnp.exp2(m_old - m_new)
            p = jnp.exp2(s_ref[slot, c] - m_new)
            acc_ref[c] = alpha * acc_ref[c] + jnp.dot(vb, p.astype(BF16), preferred_element_type=F32)
            m_ref[c] = m_new

    def accumulate_diagonal_first(q, slot):
        h = tk // 2
        vb = jnp.concatenate([vt_ref[q], jnp.ones((SUM_ROWS, tk), BF16)], axis=0)
        for c in range(2):
            m_new = mx_ref[slot, c]
            p_early = jnp.exp2(s_ref[slot, c, :h, :h] - m_new[:, :h])
            p_late = jnp.exp2(s_ref[slot, c, :, h:] - m_new[:, h:])
            acc_ref[c, :, :h] = jnp.dot(vb[:, :h], p_early.astype(BF16), preferred_element_type=F32)
            acc_ref[c, :, h:] = jnp.dot(vb, p_late.astype(BF16), preferred_element_type=F32)
            m_ref[c] = m_new

    def finalize(q):
        lam = (jnp.exp(jnp.sum(lq1_ref[...] * lk1_ref[...], axis=-1, keepdims=True))
               - jnp.exp(jnp.sum(lq2_ref[...] * lk2_ref[...], axis=-1, keepdims=True)) + lambda_init)
        l0 = acc_ref[0, V_HEAD_DIM:V_HEAD_DIM + 1]
        l1 = acc_ref[1, V_HEAD_DIM:V_HEAD_DIM + 1]
        out = acc_ref[0, :V_HEAD_DIM] * (1.0 / l0) - acc_ref[1, :V_HEAD_DIM] * (lam / l1)
        ms = jnp.mean(out * out, axis=0, keepdims=True)
        out = out * (lax.rsqrt(ms + EPS) * (1.0 - lambda_init)) * subw_ref[...]
        out = out * jax.nn.silu(zat_ref[q].astype(F32))
        y_ref[pl.ds(pl.multiple_of(q * tq, tq), tq), :] = out.T.astype(BF16)

    zeros = jnp.zeros((HEAD_DIM, tq), BF16)
    wq_ref[0, HEAD_DIM:] = zeros
    wq_ref[1, :HEAD_DIM] = zeros
    load_queries(0)
    reset_stats()
    diagonal_scores(0, 1, fill_masked=True)

    def enter_tile(q, pending_slot):
        load_queries(q)
        diagonal_scores(q, diag_slot)
        accumulate(jnp.maximum(q - 2, 0), pending_slot)
        finalize(q - 1)
        scores(0, 0)
        accumulate_diagonal_first(q, diag_slot)

    def pair(i, c):
        j = 2 * i
        scores(j + 1, 1)
        accumulate(j, 0)
        scores(j + 2, 0)
        accumulate(j + 1, 1)
        return c

    def two_tiles(k, carry):
        q_odd = 2 * k + 1
        enter_tile(q_odd, 1)
        lax.fori_loop(0, k, pair, 0)
        q_even = q_odd + 1
        enter_tile(q_even, 0)
        lax.fori_loop(0, k, pair, 0)
        scores(q_even - 1, 1)
        accumulate(q_even - 2, 0)
        return carry

    assert n_tiles % 2 == 0
    lax.fori_loop(0, (n_tiles - 2) // 2, two_tiles, 0)
    q_last = n_tiles - 1
    enter_tile(q_last, 1)
    lax.fori_loop(0, (q_last - 1) // 2, pair, 0)
    accumulate(q_last - 1, 0)
    finalize(q_last)


def _attention(qt, k, vt, zat, lq1, lk1, lq2, lk2, subln_w, lambda_init):
    b, _, nt, _, tq = qt.shape
    s = nt * tq
    assert nt >= 2
    row64 = pl.BlockSpec((1, HEAD_DIM), lambda bi, hi: (0, 0))
    head_spec = pl.BlockSpec((None, None, nt, V_HEAD_DIM, tq), lambda bi, hi: (bi, hi, 0, 0, 0))
    token_spec = pl.BlockSpec((None, s, V_HEAD_DIM), lambda bi, hi: (bi, 0, hi))
    return pl.pallas_call(
        functools.partial(_attn_kernel, lambda_init),
        grid=(b, N_HEADS),
        in_specs=[
            head_spec, token_spec, head_spec, head_spec,
            row64, row64, row64, row64,
            pl.BlockSpec((V_HEAD_DIM, 1), lambda bi, hi: (0, 0)),
        ],
        out_specs=token_spec,
        out_shape=jax.ShapeDtypeStruct((b, s, ATTN_WIDTH), BF16),
        scratch_shapes=[
            pltpu.VMEM((2, 2 * HEAD_DIM, tq), BF16),
            pltpu.VMEM((3, 2, tq, tq), F32),
            pltpu.VMEM((3, 2, 1, tq), F32),
            pltpu.VMEM((2, V_HEAD_DIM + SUM_ROWS, tq), F32),
            pltpu.VMEM((2, 1, tq), F32),
        ],
        compiler_params=_compiler_params(2),
        name="diff_attention",
    )(qt, k, vt, zat, lq1.reshape(1, HEAD_DIM), lk1.reshape(1, HEAD_DIM),
      lq2.reshape(1, HEAD_DIM), lk2.reshape(1, HEAD_DIM), subln_w.reshape(V_HEAD_DIM, 1))


def _out_kernel(x_ref, ys_ref, ya_ref, p_ref, wos_ref, woa_ref, wg_ref, wp_ref, o_ref):
    x2 = (x_ref[...]
          + jnp.dot(ys_ref[...], wos_ref[...], preferred_element_type=F32)
          + jnp.dot(ya_ref[...], woa_ref[...], preferred_element_type=F32))
    gate = jax.nn.sigmoid(jnp.dot(x2.astype(BF16), wg_ref[...], preferred_element_type=F32))
    ple = jnp.dot(p_ref[...].astype(BF16), wp_ref[...], preferred_element_type=F32)
    o_ref[...] = x2 + gate * ple


def _out_proj(x, ys, ya, p, w_out, w_proj, w_gate):
    b, s, d = x.shape
    tm = OUT_TOKEN_TILE
    const2 = lambda bi, ti: (0, 0)
    return pl.pallas_call(
        _out_kernel,
        grid=(b, s // tm),
        in_specs=[
            pl.BlockSpec((None, tm, d), lambda bi, ti: (bi, ti, 0)),
            pl.BlockSpec((None, tm, SSM_WIDTH), lambda bi, ti: (bi, ti, 0)),
            pl.BlockSpec((None, tm, ATTN_WIDTH), lambda bi, ti: (bi, ti, 0)),
            pl.BlockSpec((None, tm, PLE_DIM), lambda bi, ti: (bi, ti, 0)),
            pl.BlockSpec((SSM_WIDTH, d), const2),
            pl.BlockSpec((ATTN_WIDTH, d), const2),
            pl.BlockSpec((d, d), const2),
            pl.BlockSpec((PLE_DIM, d), const2),
        ],
        out_specs=pl.BlockSpec((None, tm, d), lambda bi, ti: (bi, ti, 0)),
        out_shape=jax.ShapeDtypeStruct((b, s, d), F32),
        compiler_params=_compiler_params(2),
        name="out_proj",
    )(x, ys, ya, p, w_out[:SSM_WIDTH].astype(BF16), w_out[SSM_WIDTH:].astype(BF16),
      w_gate.astype(BF16), w_proj.astype(BF16))


def kernel(x, p, positions, norm_w, w_in, ssm_lambda_re, ssm_lambda_im, ssm_log_dt, ssm_b_re, ssm_b_im, ssm_c_re, ssm_c_im, ssm_d, glu_w, glu_b, q_norm_w, k_norm_w, lambda_q1, lambda_k1, lambda_q2, lambda_k2, subln_w, w_out, ple_w_proj, ple_w_gate):
    depth = norm_w.shape[0]
    for i in range(depth):
        lambda_init = 0.8 - 0.6 * math.exp(-0.3 * i)
        ab, ca, cl, bbc, al = _ssm_params(ssm_lambda_re[i], ssm_lambda_im[i], ssm_log_dt[i],
                                          ssm_b_re[i], ssm_b_im[i], ssm_c_re[i], ssm_c_im[i])
        u, zs, qt, k, vt, zat = _in_proj(x, positions, norm_w[i], w_in[i], q_norm_w[i], k_norm_w[i])
        ys = _ssm_branch(u, zs, ab, ca, cl, bbc, al, ssm_d[i], glu_w[i], glu_b[i])
        ya = _attention(qt, k, vt, zat, lambda_q1[i], lambda_k1[i], lambda_q2[i], lambda_k2[i],
                        subln_w[i], lambda_init)
        x = _out_proj(x, ys, ya, p[i], w_out[i], ple_w_proj[i], ple_w_gate[i])
    return x
```

```python
import functools
import math

import jax
import jax.numpy as jnp
import numpy as np
from jax import lax
from jax.experimental import pallas as pl
from jax.experimental.pallas import tpu as pltpu

F32 = jnp.float32
BF16 = jnp.bfloat16

D_MODEL = 1024
PLE_DIM = 256
SSM_WIDTH = 512
SSM_GROUP = 16
SSM_GROUPS = 32
SSM_STATE = 64
ATTN_WIDTH = 512
N_HEADS = 4
HEAD_DIM = 64
V_HEAD_DIM = 128
ROT_DIM = 16
ROT_HALF = 8
ROPE_THETA = 500000.0
EPS = 1e-6
LOG2E = math.log2(math.e)

V7X_LANES = 128
V7X_SUBLANES = 8
V7X_VMEM_BYTES = 64 * 1024 * 1024
VMEM_LIMIT_BYTES = V7X_VMEM_BYTES * 7 // 8

TOKEN_TILE = 1024
ATTN_TILE = 512
OUT_TOKEN_TILE = 1024
SSM_FOLD = 4
SSM_STEPS = 256
SSM_PART_STEPS = 256
SSM_PERM_STEPS = 32
GROUPS_PER_BLOCK = 8
N_SSM_BLOCKS = SSM_GROUPS // GROUPS_PER_BLOCK
BLOCK_STATE = GROUPS_PER_BLOCK * SSM_STATE
MASK_VALUE = -1e30
SUM_ROWS = 16


def _gelu_tanh(x):
    c0 = math.sqrt(2.0 / math.pi)
    inner = x * (c0 + (c0 * 0.044715) * (x * x))
    half = 0.5 * x
    return half + half * jnp.tanh(inner)


def _compiler_params(n_grid_axes):
    return pltpu.CompilerParams(
        dimension_semantics=("arbitrary",) * n_grid_axes,
        vmem_limit_bytes=VMEM_LIMIT_BYTES,
    )


def _ssm_params_kernel(lr_ref, li_ref, logdt_ref, br_ref, bi_ref, cr_ref, ci_ref,
                       ab_ref, ca_ref, cl_ref, bbc_ref, al_ref):
    fold = ab_ref.shape[0]
    n = lr_ref.shape[-1]
    lr = lr_ref[...]
    li = li_ref[...]
    dt = jnp.exp(logdt_ref[...])
    mag = jnp.exp(lr * dt)
    a_re = mag * jnp.cos(li * dt)
    a_im = mag * jnp.sin(li * dt)
    nr = a_re - 1.0
    ni = a_im
    den = lr * lr + li * li
    coef_re = (nr * lr + ni * li) / den
    coef_im = (ni * lr - nr * li) / den
    br = br_ref[...]
    bi = bi_ref[...]
    bb_re = coef_re * br - coef_im * bi
    bb_im = coef_re * bi + coef_im * br
    c_re = cr_ref[...]
    c_im = ci_ref[...]

    def cmul(xr, xi, yr, yi):
        return xr * yr - xi * yi, xr * yi + xi * yr

    powers = [(jnp.ones_like(a_re), jnp.zeros_like(a_im))]
    for _ in range(fold):
        powers.append(cmul(*powers[-1], a_re, a_im))
    for i in range(fold):
        ab_ref[i, 0], ab_ref[i, 1] = cmul(*powers[fold - 1 - i], bb_re, bb_im)
        ca_re, ca_im = cmul(c_re, c_im, *powers[i + 1])
        ca_ref[i, 0] = ca_re
        ca_ref[i, 1] = -ca_im
        cl_re, cl_im = cmul(c_re, c_im, *powers[i])
        cl_ref[i, :, :, :n] = cl_re
        cl_ref[i, :, :, n:] = -cl_im
    bbc_ref[:, :, :n] = bb_re
    bbc_ref[:, :, n:] = bb_im
    al_ref[0] = powers[fold][0]
    al_ref[1] = powers[fold][1]


def _ssm_params(lam_re, lam_im, log_dt, b_re, b_im, c_re, c_im):
    g, n, p, fold = SSM_GROUPS, SSM_STATE, SSM_GROUP, SSM_FOLD
    per_offset = jax.ShapeDtypeStruct((fold, 2, g, p, n), F32)
    return pl.pallas_call(
        _ssm_params_kernel,
        out_shape=(per_offset, per_offset,
                   jax.ShapeDtypeStruct((fold, g, p, 2 * n), F32),
                   jax.ShapeDtypeStruct((g, p, 2 * n), F32),
                   jax.ShapeDtypeStruct((2, g, 1, n), F32)),
        name="ssm_params",
    )(lam_re.reshape(g, 1, n), lam_im.reshape(g, 1, n), log_dt.reshape(g, 1, 1),
      jnp.swapaxes(b_re, 1, 2), jnp.swapaxes(b_im, 1, 2), c_re, c_im)


def _in_proj_kernel(x_ref, nw_ref, w_ref, pos_ref, invf_ref, qnw_ref, knw_ref,
                    u_ref, zs_ref, qt_ref, k_ref, vt_ref, zat_ref, wnat_ref, wtr_ref):
    @pl.when((pl.program_id(0) == 0) & (pl.program_id(1) == 0))
    def _():
        n_nat = wnat_ref.shape[1]
        wnat_ref[...] = w_ref[:, :n_nat].astype(BF16)
        chunk = wtr_ref.shape[1] // 2
        for c0 in range(0, wtr_ref.shape[0], chunk):
            wtr_ref[c0:c0 + chunk, :] = w_ref[:, n_nat + c0:n_nat + c0 + chunk].T.astype(BF16)

    x = x_ref[...]
    ms = jnp.mean(x * x, axis=-1, keepdims=True)
    h = (x * lax.rsqrt(ms + EPS) * nw_ref[...]).astype(BF16)
    tm = x.shape[0]

    def channel_major(r0, r1):
        return lax.dot_general(wtr_ref[r0:r1, :], h, (((1,), (1,)), ((), ())), preferred_element_type=F32)

    ang = pos_ref[...].astype(F32) * invf_ref[...]
    cos = jnp.cos(ang)
    sin = jnp.sin(ang)

    def norm_rope(t, w_ref):
        t3 = t.reshape(2 * N_HEADS, HEAD_DIM, tm)
        ms3 = jnp.mean(t3 * t3, axis=1, keepdims=True)
        t3 = t3 * lax.rsqrt(ms3 + EPS) * w_ref[...]
        t1 = t3[:, :ROT_HALF, :]
        t2 = t3[:, ROT_HALF:ROT_DIM, :]
        r1 = t1 * cos - t2 * sin
        r2 = t2 * cos + t1 * sin
        t3 = jnp.concatenate([r1, r2, t3[:, ROT_DIM:, :]], axis=1)
        return t3.reshape(ATTN_WIDTH, tm)

    def store_head_major(ref, t):
        t = t.reshape(N_HEADS, V_HEAD_DIM, tm).astype(BF16)
        ta = ref.shape[-1]
        for j in range(tm // ta):
            ref[:, j] = t[:, :, j * ta:(j + 1) * ta]

    q = norm_rope(channel_major(0, ATTN_WIDTH), qnw_ref) * (HEAD_DIM ** -0.5 * LOG2E)
    store_head_major(qt_ref, q)
    k = norm_rope(channel_major(ATTN_WIDTH, 2 * ATTN_WIDTH), knw_ref)
    k_ref[...] = k.T.astype(BF16)
    store_head_major(vt_ref, channel_major(2 * ATTN_WIDTH, 3 * ATTN_WIDTH))
    store_head_major(zat_ref, channel_major(3 * ATTN_WIDTH, 4 * ATTN_WIDTH))
    nat = jnp.dot(h, wnat_ref[...], preferred_element_type=F32)
    u_ref[...] = nat[:, :SSM_WIDTH].astype(BF16)
    zs_ref[...] = nat[:, SSM_WIDTH:].astype(BF16)


def _in_proj(x, positions, norm_w, w_in, q_norm_w, k_norm_w):
    b, s, d = x.shape
    tm, ta = TOKEN_TILE, ATTN_TILE
    nt, tiles_per_step = s // ta, tm // ta
    inv_freq = ROPE_THETA ** (-jnp.arange(0, ROT_DIM, 2, dtype=F32) / ROT_DIM)
    head_major = jax.ShapeDtypeStruct((b, N_HEADS, nt, V_HEAD_DIM, ta), BF16)
    head_spec = pl.BlockSpec((None, N_HEADS, tiles_per_step, V_HEAD_DIM, ta), lambda bi, ti: (bi, 0, ti, 0, 0))
    ssm_shape = jax.ShapeDtypeStruct((b, s, SSM_WIDTH), BF16)
    ssm_spec = pl.BlockSpec((None, tm, SSM_WIDTH), lambda bi, ti: (bi, ti, 0))
    const2 = lambda bi, ti: (0, 0)
    const3 = lambda bi, ti: (0, 0, 0)
    return pl.pallas_call(
        _in_proj_kernel,
        grid=(b, s // tm),
        in_specs=[
            pl.BlockSpec((None, tm, d), lambda bi, ti: (bi, ti, 0)),
            pl.BlockSpec((1, d), const2),
            pl.BlockSpec(w_in.shape, const2),
            pl.BlockSpec((None, 1, tm), lambda bi, ti: (bi, 0, ti)),
            pl.BlockSpec((ROT_HALF, 1), const2),
            pl.BlockSpec((1, HEAD_DIM, 1), const3),
            pl.BlockSpec((1, HEAD_DIM, 1), const3),
        ],
        out_specs=[
            ssm_spec, ssm_spec,
            head_spec,
            pl.BlockSpec((None, tm, ATTN_WIDTH), lambda bi, ti: (bi, ti, 0)),
            head_spec, head_spec,
        ],
        out_shape=(ssm_shape, ssm_shape, head_major,
                   jax.ShapeDtypeStruct((b, s, ATTN_WIDTH), BF16), head_major, head_major),
        scratch_shapes=[
            pltpu.VMEM((d, 2 * SSM_WIDTH), BF16),
            pltpu.VMEM((4 * ATTN_WIDTH, d), BF16),
        ],
        compiler_params=_compiler_params(2),
        name="in_proj",
    )(x, norm_w.reshape(1, d), w_in, positions.reshape(b, 1, s),
      inv_freq.reshape(ROT_HALF, 1), q_norm_w.reshape(1, HEAD_DIM, 1), k_norm_w.reshape(1, HEAD_DIM, 1))


def _ssm_kernel(u_ref, zs_ref, pin_ref, pout_ref, ab_ref, ca_ref, cd_ref, ar_ref, ai_ref, d_ref,
                gw_ref, gb_ref, y_ref, uf_ref, bu_ref, xs_ref, y2_ref, yg_ref, state_ref,
                bw_ref, cs_ref, tmp_ref):
    n_batch, steps, width = u_ref.shape
    fold, perm_out_rows, perm_in_rows = pin_ref.shape
    perm_steps = perm_in_rows // n_batch
    part_steps = SSM_PART_STEPS
    perms_per_part = part_steps // perm_steps
    part_rows = perms_per_part * perm_out_rows
    n_parts = steps // part_steps
    blk = 2 * BLOCK_STATE
    fw = fold * V7X_LANES
    pack_rows = 2 * V7X_SUBLANES

    @pl.when(pl.program_id(0) == 0)
    def _():
        state_ref[...] = jnp.zeros_like(state_ref)
        bw_ref[...] = jnp.zeros_like(bw_ref)
        for m in range(N_SSM_BLOCKS):
            tmp_ref[...] = jnp.zeros_like(tmp_ref)
            for i in range(fold):
                for gl in range(GROUPS_PER_BLOCK):
                    g = m * GROUPS_PER_BLOCK + gl
                    in_rows = slice(i * V7X_LANES + gl * SSM_GROUP, i * V7X_LANES + (gl + 1) * SSM_GROUP)
                    for ri in range(2):
                        st_cols = slice(ri * BLOCK_STATE + gl * SSM_STATE, ri * BLOCK_STATE + (gl + 1) * SSM_STATE)
                        bw_ref[m, in_rows, st_cols] = ab_ref[i, ri, g].astype(BF16)
                        tmp_ref[in_rows, st_cols] = ca_ref[i, ri, g]
            cs_ref[m] = tmp_ref[...].T.astype(BF16)

    def rows_of(part):
        return slice(part * part_rows, (part + 1) * part_rows)

    def fold_in(part):
        for h in range(perms_per_part):
            t0 = part * part_steps + h * perm_steps
            u_bt = u_ref[:, t0:t0 + perm_steps, :].reshape(perm_in_rows, width)
            r0 = part * part_rows + h * perm_out_rows
            sel_all = jnp.dot(pin_ref[...].reshape(fold * perm_out_rows, perm_in_rows), u_bt,
                              preferred_element_type=F32).astype(BF16)
            for i in range(fold):
                sel = sel_all[i * perm_out_rows:(i + 1) * perm_out_rows]
                for m in range(N_SSM_BLOCKS):
                    uf_ref[r0:r0 + perm_out_rows, m * fw + i * V7X_LANES:m * fw + (i + 1) * V7X_LANES] = (
                        sel[:, m * V7X_LANES:(m + 1) * V7X_LANES])

    def state_block(part, m):
        rows = rows_of(part)
        re = slice(m * blk, m * blk + BLOCK_STATE)
        im = slice(m * blk + BLOCK_STATE, (m + 1) * blk)
        bu_ref[rows, m * blk:(m + 1) * blk] = jnp.dot(
            uf_ref[rows, m * fw:(m + 1) * fw], bw_ref[m], preferred_element_type=F32)
        a_re = jnp.broadcast_to(ar_ref[:, m * BLOCK_STATE:(m + 1) * BLOCK_STATE], (V7X_SUBLANES, BLOCK_STATE))
        a_im = jnp.broadcast_to(ai_ref[:, m * BLOCK_STATE:(m + 1) * BLOCK_STATE], (V7X_SUBLANES, BLOCK_STATE))
        x_re = state_ref[:, re]
        x_im = state_ref[:, im]
        for r0 in range(part * part_rows, (part + 1) * part_rows, pack_rows):
            ins_re, ins_im = [], []
            for half in range(2):
                rr = slice(r0 + half * V7X_SUBLANES, r0 + (half + 1) * V7X_SUBLANES)
                ins_re.append(x_re)
                ins_im.append(x_im)
                n_re = a_re * x_re - a_im * x_im + bu_ref[rr, re]
                n_im = a_re * x_im + a_im * x_re + bu_ref[rr, im]
                x_re, x_im = n_re, n_im
            xs_ref[r0:r0 + pack_rows, re] = jnp.concatenate(ins_re, axis=0).astype(BF16)
            xs_ref[r0:r0 + pack_rows, im] = jnp.concatenate(ins_im, axis=0).astype(BF16)
        state_ref[:, re] = x_re
        state_ref[:, im] = x_im
        y2_ref[rows, m * fw:(m + 1) * fw] = (
            jnp.dot(xs_ref[rows, m * blk:(m + 1) * blk], cs_ref[m], preferred_element_type=F32)
            + jnp.dot(uf_ref[rows, m * fw:(m + 1) * fw], cd_ref[m], preferred_element_type=F32))

    def gate_offset(part, i):
        rows = rows_of(part)
        cols = [slice(m * fw + i * V7X_LANES, m * fw + (i + 1) * V7X_LANES) for m in range(N_SSM_BLOCKS)]
        y = jnp.concatenate([y2_ref[rows, c] for c in cols], axis=-1)
        u_i = jnp.concatenate([uf_ref[rows, c] for c in cols], axis=-1).astype(F32)
        y = _gelu_tanh(y + d_ref[...] * u_i)
        gate = jnp.dot(y.astype(BF16), gw_ref[...], preferred_element_type=F32) + gb_ref[...]
        y = (y * jax.nn.sigmoid(gate)).astype(BF16)
        for h in range(perms_per_part):
            yg_ref[part * perms_per_part + h, i * perm_out_rows:(i + 1) * perm_out_rows, :] = (
                y[h * perm_out_rows:(h + 1) * perm_out_rows])

    def unfold_out(part):
        for h in range(perms_per_part):
            t0 = part * part_steps + h * perm_steps
            y_bt = jnp.dot(pout_ref[...], yg_ref[part * perms_per_part + h], preferred_element_type=F32)
            z = zs_ref[:, t0:t0 + perm_steps, :].reshape(perm_in_rows, width).astype(F32)
            y_ref[:, t0:t0 + perm_steps, :] = (
                (y_bt * jax.nn.silu(z)).astype(BF16).reshape(n_batch, perm_steps, width))

    assert fold == N_SSM_BLOCKS
    for part in range(n_parts + 1):
        if part < n_parts:
            fold_in(part)
        for j in range(N_SSM_BLOCKS):
            if part < n_parts:
                state_block(part, j)
            if part >= 1:
                gate_offset(part - 1, j)
        if part >= 1:
            unfold_out(part - 1)


def _ssm_direct_weights(cl, bbc):
    fold, gpb, nb, p = SSM_FOLD, GROUPS_PER_BLOCK, N_SSM_BLOCKS, SSM_GROUP
    lanes = gpb * p
    hi = lax.Precision.HIGHEST
    lags = jnp.einsum('lgpn,gqn->lgqp', cl, bbc, precision=hi)
    lane = np.arange(lanes)
    spread = (np.arange(p)[:, None] == (lane % p)[None, :]).astype(np.float32)
    same_group = ((lane // p)[:, None] == (lane // p)[None, :]).astype(np.float32)
    bd = jnp.dot(lags.reshape(fold, nb, lanes, p), spread, precision=hi) * same_group
    zero = jnp.zeros_like(bd[0])
    rows = [jnp.concatenate([bd[i - k] if i >= k else zero for i in range(fold)], axis=-1) for k in range(fold)]
    return jnp.concatenate(rows, axis=-2).astype(BF16)


def _ssm_branch(u, zs, ab, ca, cl, bbc, al, d_skip, glu_w, glu_b):
    n_batch, s, _ = u.shape
    assert n_batch == V7X_SUBLANES, "one time group of all batches must fill one sublane tile"
    fold, steps, perm_steps = SSM_FOLD, SSM_STEPS, SSM_PERM_STEPS
    g_rows = (perm_steps // fold) * n_batch
    n_rows = perm_steps * n_batch
    r = np.arange(g_rows)
    src = (r % n_batch) * perm_steps + fold * (r // n_batch)
    pin = np.arange(n_rows)[None, None, :] == (src[None, :, None] + np.arange(fold)[:, None, None])
    pout = jnp.asarray(pin.reshape(fold * g_rows, n_rows).T, dtype=BF16)
    pin = jnp.asarray(pin, dtype=BF16)
    cd = _ssm_direct_weights(cl, bbc)
    row_spec = pl.BlockSpec((n_batch, steps, SSM_WIDTH), lambda i: (0, i, 0))
    const2 = lambda i: (0, 0)
    const3 = lambda i: (0, 0, 0)
    n_state = SSM_GROUPS * SSM_STATE
    f_rows = (steps // fold) * n_batch
    fw_all = fold * SSM_WIDTH
    return pl.pallas_call(
        _ssm_kernel,
        grid=(s // steps,),
        in_specs=[
            row_spec, row_spec,
            pl.BlockSpec(pin.shape, const3),
            pl.BlockSpec(pout.shape, const2),
            pl.BlockSpec(ab.shape, lambda i: (0, 0, 0, 0, 0)),
            pl.BlockSpec(ca.shape, lambda i: (0, 0, 0, 0, 0)),
            pl.BlockSpec(cd.shape, const3),
            pl.BlockSpec((1, n_state), const2),
            pl.BlockSpec((1, n_state), const2),
            pl.BlockSpec((1, SSM_WIDTH), const2),
            pl.BlockSpec((SSM_WIDTH, SSM_WIDTH), const2),
            pl.BlockSpec((1, SSM_WIDTH), const2),
        ],
        out_specs=row_spec,
        out_shape=jax.ShapeDtypeStruct((n_batch, s, SSM_WIDTH), BF16),
        scratch_shapes=[
            pltpu.VMEM((f_rows, fw_all), BF16),
            pltpu.VMEM((f_rows, 2 * n_state), F32),
            pltpu.VMEM((f_rows, 2 * n_state), BF16),
            pltpu.VMEM((f_rows, fw_all), F32),
            pltpu.VMEM((steps // perm_steps, fold * g_rows, SSM_WIDTH), BF16),
            pltpu.VMEM((V7X_SUBLANES, 2 * n_state), F32),
            pltpu.VMEM((N_SSM_BLOCKS, fold * V7X_LANES, 2 * BLOCK_STATE), BF16),
            pltpu.VMEM((N_SSM_BLOCKS, 2 * BLOCK_STATE, fold * V7X_LANES), BF16),
            pltpu.VMEM((fold * V7X_LANES, 2 * BLOCK_STATE), F32),
        ],
        compiler_params=_compiler_params(1),
        name="ssm_branch",
    )(u, zs, pin, pout, ab, ca, cd, al[0].reshape(1, n_state), al[1].reshape(1, n_state),
      d_skip.reshape(1, SSM_WIDTH), glu_w.astype(BF16), glu_b.reshape(1, SSM_WIDTH))


def _attn_kernel(lambda_init, qt_ref, k_ref, vt_ref, zat_ref, lq1_ref, lk1_ref, lq2_ref, lk2_ref,
                 subw_ref, y_ref, wq_ref, s_ref, mx_ref, acc_ref, m_ref):
    n_tiles, _, tq = qt_ref.shape
    tk = tq
    diag_slot = 2

    def load_queries(q):
        wq_ref[0, :HEAD_DIM] = qt_ref[q, :HEAD_DIM]
        wq_ref[1, HEAD_DIM:] = qt_ref[q, HEAD_DIM:]

    def reset_stats():
        m_ref[...] = jnp.full_like(m_ref, MASK_VALUE)
        acc_ref[...] = jnp.zeros_like(acc_ref)

    def scores(j, slot):
        kb = k_ref[pl.ds(pl.multiple_of(j * tk, tk), tk), :]
        for c in range(2):
            s = jnp.dot(kb, wq_ref[c], preferred_element_type=F32)
            s_ref[slot, c] = s
            mx_ref[slot, c] = jnp.max(s, axis=0, keepdims=True)

    def diagonal_scores(q, slot, fill_masked=False):
        h = tk // 2
        row0 = pl.multiple_of(q * tk, tk)
        k_lo = k_ref[pl.ds(row0, h), :]
        k_hi = k_ref[pl.ds(row0 + h, h), :]
        causal = (lax.broadcasted_iota(jnp.int32, (h, h), 0) <= lax.broadcasted_iota(jnp.int32, (h, h), 1))
        for c in range(2):
            s_lo = jnp.dot(k_lo, wq_ref[c], preferred_element_type=F32)
            s_hi = jnp.dot(k_hi, wq_ref[c, :, h:], preferred_element_type=F32)
            s_ll = jnp.where(causal, s_lo[:, :h], MASK_VALUE)
            s_hh = jnp.where(causal, s_hi, MASK_VALUE)
            s_ref[slot, c, :h, :h] = s_ll
            s_ref[slot, c, :h, h:] = s_lo[:, h:]
            if fill_masked:
                s_ref[slot, c, h:, :h] = jnp.full((h, h), MASK_VALUE, F32)
            s_ref[slot, c, h:, h:] = s_hh
            mx_ref[slot, c, :, :h] = jnp.max(s_ll, axis=0, keepdims=True)
            mx_ref[slot, c, :, h:] = jnp.maximum(jnp.max(s_lo[:, h:], axis=0, keepdims=True),
                                                 jnp.max(s_hh, axis=0, keepdims=True))

    def accumulate(j, slot):
        vb = jnp.concatenate([vt_ref[j], jnp.ones((SUM_ROWS, tk), BF16)], axis=0)
        for c in range(2):
            m_old = m_ref[c]
            m_new = jnp.maximum(m_old, mx_ref[slot, c])
            alpha = jnp.exp2(m_old - m_new)
            p = jnp.exp2(s_ref[slot, c] - m_new)
            acc_ref[c] = alpha * acc_ref[c] + jnp.dot(vb, p.astype(BF16), preferred_element_type=F32)
            m_ref[c] = m_new

    def accumulate_diagonal_first(q, slot):
        h = tk // 2
        vb = jnp.concatenate([vt_ref[q], jnp.ones((SUM_ROWS, tk), BF16)], axis=0)
        for c in range(2):
            m_new = mx_ref[slot, c]
            p_early = jnp.exp2(s_ref[slot, c, :h, :h] - m_new[:, :h])
            p_late = jnp.exp2(s_ref[slot, c, :, h:] - m_new[:, h:])
            acc_ref[c, :, :h] = jnp.dot(vb[:, :h], p_early.astype(BF16), preferred_element_type=F32)
            acc_ref[c, :, h:] = jnp.dot(vb, p_late.astype(BF16), preferred_element_type=F32)
            m_ref[c] = m_new

    def finalize(q):
        lam = (jnp.exp(jnp.sum(lq1_ref[...] * lk1_ref[...], axis=-1, keepdims=True))
               - jnp.exp(jnp.sum(lq2_ref[...] * lk2_ref[...], axis=-1, keepdims=True)) + lambda_init)
        l0 = acc_ref[0, V_HEAD_DIM:V_HEAD_DIM + 1]
        l1 = acc_ref[1, V_HEAD_DIM:V_HEAD_DIM + 1]
        out = acc_ref[0, :V_HEAD_DIM] * (1.0 / l0) - acc_ref[1, :V_HEAD_DIM] * (lam / l1)
        ms = jnp.mean(out * out, axis=0, keepdims=True)
        out = out * (lax.rsqrt(ms + EPS) * (1.0 - lambda_init)) * subw_ref[...]
        out = out * jax.nn.silu(zat_ref[q].astype(F32))
        y_ref[pl.ds(pl.multiple_of(q * tq, tq), tq), :] = out.T.astype(BF16)

    zeros = jnp.zeros((HEAD_DIM, tq), BF16)
    wq_ref[0, HEAD_DIM:] = zeros
    wq_ref[1, :HEAD_DIM] = zeros
    load_queries(0)
    reset_stats()
    diagonal_scores(0, 1, fill_masked=True)

    def enter_tile(q, pending_slot):
        load_queries(q)
        diagonal_scores(q, diag_slot)
        accumulate(jnp.maximum(q - 2, 0), pending_slot)
        finalize(q - 1)
        scores(0, 0)
        accumulate_diagonal_first(q, diag_slot)

    def pair(i, c):
        j = 2 * i
        scores(j + 1, 1)
        accumulate(j, 0)
        scores(j + 2, 0)
        accumulate(j + 1, 1)
        return c

    def two_tiles(k, carry):
        q_odd = 2 * k + 1
        enter_tile(q_odd, 1)
        lax.fori_loop(0, k, pair, 0)
        q_even = q_odd + 1
        enter_tile(q_even, 0)
        lax.fori_loop(0, k, pair, 0)
        scores(q_even - 1, 1)
        accumulate(q_even - 2, 0)
        return carry

    assert n_tiles % 2 == 0
    lax.fori_loop(0, (n_tiles - 2) // 2, two_tiles, 0)
    q_last = n_tiles - 1
    enter_tile(q_last, 1)
    lax.fori_loop(0, (q_last - 1) // 2, pair, 0)
    accumulate(q_last - 1, 0)
    finalize(q_last)


def _attention(qt, k, vt, zat, lq1, lk1, lq2, lk2, subln_w, lambda_init):
    b, _, nt, _, tq = qt.shape
    s = nt * tq
    assert nt >= 2
    row64 = pl.BlockSpec((1, HEAD_DIM), lambda bi, hi: (0, 0))
    head_spec = pl.BlockSpec((None, None, nt, V_HEAD_DIM, tq), lambda bi, hi: (bi, hi, 0, 0, 0))
    token_spec = pl.BlockSpec((None, s, V_HEAD_DIM), lambda bi, hi: (bi, 0, hi))
    return pl.pallas_call(
        functools.partial(_attn_kernel, lambda_init),
        grid=(b, N_HEADS),
        in_specs=[
            head_spec, token_spec, head_spec, head_spec,
            row64, row64, row64, row64,
            pl.BlockSpec((V_HEAD_DIM, 1), lambda bi, hi: (0, 0)),
        ],
        out_specs=token_spec,
        out_shape=jax.ShapeDtypeStruct((b, s, ATTN_WIDTH), BF16),
        scratch_shapes=[
            pltpu.VMEM((2, 2 * HEAD_DIM, tq), BF16),
            pltpu.VMEM((3, 2, tq, tq), F32),
            pltpu.VMEM((3, 2, 1, tq), F32),
            pltpu.VMEM((2, V_HEAD_DIM + SUM_ROWS, tq), F32),
            pltpu.VMEM((2, 1, tq), F32),
        ],
        compiler_params=_compiler_params(2),
        name="diff_attention",
    )(qt, k, vt, zat, lq1.reshape(1, HEAD_DIM), lk1.reshape(1, HEAD_DIM),
      lq2.reshape(1, HEAD_DIM), lk2.reshape(1, HEAD_DIM), subln_w.reshape(V_HEAD_DIM, 1))


def _out_kernel(x_ref, ys_ref, ya_ref, p_ref, wos_ref, woa_ref, wg_ref, wp_ref, o_ref):
    x2 = (x_ref[...]
          + jnp.dot(ys_ref[...], wos_ref[...], preferred_element_type=F32)
          + jnp.dot(ya_ref[...], woa_ref[...], preferred_element_type=F32))
    gate = jax.nn.sigmoid(jnp.dot(x2.astype(BF16), wg_ref[...], preferred_element_type=F32))
    ple = jnp.dot(p_ref[...].astype(BF16), wp_ref[...], preferred_element_type=F32)
    o_ref[...] = x2 + gate * ple


def _out_proj(x, ys, ya, p, w_out, w_proj, w_gate):
    b, s, d = x.shape
    tm = OUT_TOKEN_TILE
    const2 = lambda bi, ti: (0, 0)
    return pl.pallas_call(
        _out_kernel,
        grid=(b, s // tm),
        in_specs=[
            pl.BlockSpec((None, tm, d), lambda bi, ti: (bi, ti, 0)),
            pl.BlockSpec((None, tm, SSM_WIDTH), lambda bi, ti: (bi, ti, 0)),
            pl.BlockSpec((None, tm, ATTN_WIDTH), lambda bi, ti: (bi, ti, 0)),
            pl.BlockSpec((None, tm, PLE_DIM), lambda bi, ti: (bi, ti, 0)),
            pl.BlockSpec((SSM_WIDTH, d), const2),
            pl.BlockSpec((ATTN_WIDTH, d), const2),
            pl.BlockSpec((d, d), const2),
            pl.BlockSpec((PLE_DIM, d), const2),
        ],
        out_specs=pl.BlockSpec((None, tm, d), lambda bi, ti: (bi, ti, 0)),
        out_shape=jax.ShapeDtypeStruct((b, s, d), F32),
        compiler_params=_compiler_params(2),
        name="out_proj",
    )(x, ys, ya, p, w_out[:SSM_WIDTH].astype(BF16), w_out[SSM_WIDTH:].astype(BF16),
      w_gate.astype(BF16), w_proj.astype(BF16))


def kernel(x, p, positions, norm_w, w_in, ssm_lambda_re, ssm_lambda_im, ssm_log_dt, ssm_b_re, ssm_b_im, ssm_c_re, ssm_c_im, ssm_d, glu_w, glu_b, q_norm_w, k_norm_w, lambda_q1, lambda_k1, lambda_q2, lambda_k2, subln_w, w_out, ple_w_proj, ple_w_gate):
    depth = norm_w.shape[0]
    for i in range(depth):
        lambda_init = 0.8 - 0.6 * math.exp(-0.3 * i)
        ab, ca, cl, bbc, al = _ssm_params(ssm_lambda_re[i], ssm_lambda_im[i], ssm_log_dt[i],
                                          ssm_b_re[i], ssm_b_im[i], ssm_c_re[i], ssm_c_im[i])
        u, zs, qt, k, vt, zat = _in_proj(x, positions, norm_w[i], w_in[i], q_norm_w[i], k_norm_w[i])
        ys = _ssm_branch(u, zs, ab, ca, cl, bbc, al, ssm_d[i], glu_w[i], glu_b[i])
        ya = _attention(qt, k, vt, zat, lambda_q1[i], lambda_k1[i], lambda_q2[i], lambda_k2[i],
                        subln_w[i], lambda_init)
        x = _out_proj(x, ys, ya, p[i], w_out[i], ple_w_proj[i], ple_w_gate[i])
    return x
```

```python
import functools
import math

import jax
import jax.numpy as jnp
import numpy as np
from jax import lax
from jax.experimental import pallas as pl
from jax.experimental.pallas import tpu as pltpu

F32 = jnp.float32
BF16 = jnp.bfloat16

D_MODEL = 1024
PLE_DIM = 256
SSM_WIDTH = 512
SSM_GROUP = 16
SSM_GROUPS = 32
SSM_STATE = 64
ATTN_WIDTH = 512
N_HEADS = 4
HEAD_DIM = 64
V_HEAD_DIM = 128
ROT_DIM = 16
ROT_HALF = 8
ROPE_THETA = 500000.0
EPS = 1e-6
LOG2E = math.log2(math.e)

V7X_LANES = 128
V7X_SUBLANES = 8
V7X_VMEM_BYTES = 64 * 1024 * 1024
VMEM_LIMIT_BYTES = V7X_VMEM_BYTES * 7 // 8

TOKEN_TILE = 1024
ATTN_TILE = 512
OUT_TOKEN_TILE = 1024
SSM_FOLD = 4
SSM_STEPS = 256
SSM_PART_STEPS = 256
SSM_PERM_STEPS = 32
GROUPS_PER_BLOCK = 8
N_SSM_BLOCKS = SSM_GROUPS // GROUPS_PER_BLOCK
BLOCK_STATE = GROUPS_PER_BLOCK * SSM_STATE
MASK_VALUE = -1e30
SUM_ROWS = 16


def _gelu_tanh(x):
    c0 = math.sqrt(2.0 / math.pi)
    inner = x * (c0 + (c0 * 0.044715) * (x * x))
    half = 0.5 * x
    return half + half * jnp.tanh(inner)


def _compiler_params(n_grid_axes):
    return pltpu.CompilerParams(
        dimension_semantics=("arbitrary",) * n_grid_axes,
        vmem_limit_bytes=VMEM_LIMIT_BYTES,
    )


def _ssm_params_kernel(lr_ref, li_ref, logdt_ref, br_ref, bi_ref, cr_ref, ci_ref,
                       ab_ref, ca_ref, cl_ref, bbc_ref, al_ref):
    fold = ab_ref.shape[0]
    n = lr_ref.shape[-1]
    lr = lr_ref[...]
    li = li_ref[...]
    dt = jnp.exp(logdt_ref[...])
    mag = jnp.exp(lr * dt)
    a_re = mag * jnp.cos(li * dt)
    a_im = mag * jnp.sin(li * dt)
    nr = a_re - 1.0
    ni = a_im
    den = lr * lr + li * li
    coef_re = (nr * lr + ni * li) / den
    coef_im = (ni * lr - nr * li) / den
    br = br_ref[...]
    bi = bi_ref[...]
    bb_re = coef_re * br - coef_im * bi
    bb_im = coef_re * bi + coef_im * br
    c_re = cr_ref[...]
    c_im = ci_ref[...]

    def cmul(xr, xi, yr, yi):
        return xr * yr - xi * yi, xr * yi + xi * yr

    powers = [(jnp.ones_like(a_re), jnp.zeros_like(a_im))]
    for _ in range(fold):
        powers.append(cmul(*powers[-1], a_re, a_im))
    for i in range(fold):
        ab_ref[i, 0], ab_ref[i, 1] = cmul(*powers[fold - 1 - i], bb_re, bb_im)
        ca_re, ca_im = cmul(c_re, c_im, *powers[i + 1])
        ca_ref[i, 0] = ca_re
        ca_ref[i, 1] = -ca_im
        cl_re, cl_im = cmul(c_re, c_im, *powers[i])
        cl_ref[i, :, :, :n] = cl_re
        cl_ref[i, :, :, n:] = -cl_im
    bbc_ref[:, :, :n] = bb_re
    bbc_ref[:, :, n:] = bb_im
    al_ref[0] = powers[fold][0]
    al_ref[1] = powers[fold][1]


def _ssm_params(lam_re, lam_im, log_dt, b_re, b_im, c_re, c_im):
    g, n, p, fold = SSM_GROUPS, SSM_STATE, SSM_GROUP, SSM_FOLD
    per_offset = jax.ShapeDtypeStruct((fold, 2, g, p, n), F32)
    return pl.pallas_call(
        _ssm_params_kernel,
        out_shape=(per_offset, per_offset,
                   jax.ShapeDtypeStruct((fold, g, p, 2 * n), F32),
                   jax.ShapeDtypeStruct((g, p, 2 * n), F32),
                   jax.ShapeDtypeStruct((2, g, 1, n), F32)),
        name="ssm_params",
    )(lam_re.reshape(g, 1, n), lam_im.reshape(g, 1, n), log_dt.reshape(g, 1, 1),
      jnp.swapaxes(b_re, 1, 2), jnp.swapaxes(b_im, 1, 2), c_re, c_im)


def _in_proj_kernel(x_ref, nw_ref, w_ref, pos_ref, invf_ref, qnw_ref, knw_ref,
                    u_ref, zs_ref, qt_ref, k_ref, vt_ref, zat_ref, wnat_ref, wtr_ref):
    @pl.when((pl.program_id(0) == 0) & (pl.program_id(1) == 0))
    def _():
        n_nat = wnat_ref.shape[1]
        wnat_ref[...] = w_ref[:, :n_nat].astype(BF16)
        chunk = wtr_ref.shape[1] // 2
        for c0 in range(0, wtr_ref.shape[0], chunk):
            wtr_ref[c0:c0 + chunk, :] = w_ref[:, n_nat + c0:n_nat + c0 + chunk].T.astype(BF16)

    x = x_ref[...]
    ms = jnp.mean(x * x, axis=-1, keepdims=True)
    h = (x * lax.rsqrt(ms + EPS) * nw_ref[...]).astype(BF16)
    tm = x.shape[0]

    def channel_major(r0, r1):
        return lax.dot_general(wtr_ref[r0:r1, :], h, (((1,), (1,)), ((), ())), preferred_element_type=F32)

    ang = pos_ref[...].astype(F32) * invf_ref[...]
    cos = jnp.cos(ang)
    sin = jnp.sin(ang)

    def norm_rope(t, w_ref):
        t3 = t.reshape(2 * N_HEADS, HEAD_DIM, tm)
        ms3 = jnp.mean(t3 * t3, axis=1, keepdims=True)
        t3 = t3 * lax.rsqrt(ms3 + EPS) * w_ref[...]
        t1 = t3[:, :ROT_HALF, :]
        t2 = t3[:, ROT_HALF:ROT_DIM, :]
        r1 = t1 * cos - t2 * sin
        r2 = t2 * cos + t1 * sin
        t3 = jnp.concatenate([r1, r2, t3[:, ROT_DIM:, :]], axis=1)
        return t3.reshape(ATTN_WIDTH, tm)

    def store_head_major(ref, t):
        t = t.reshape(N_HEADS, V_HEAD_DIM, tm).astype(BF16)
        ta = ref.shape[-1]
        for j in range(tm // ta):
            ref[:, j] = t[:, :, j * ta:(j + 1) * ta]

    q = norm_rope(channel_major(0, ATTN_WIDTH), qnw_ref) * (HEAD_DIM ** -0.5 * LOG2E)
    store_head_major(qt_ref, q)
    k = norm_rope(channel_major(ATTN_WIDTH, 2 * ATTN_WIDTH), knw_ref)
    k_ref[...] = k.T.astype(BF16)
    vz = channel_major(2 * ATTN_WIDTH, 4 * ATTN_WIDTH)
    store_head_major(vt_ref, vz[:ATTN_WIDTH])
    store_head_major(zat_ref, vz[ATTN_WIDTH:])
    nat = jnp.dot(h, wnat_ref[...], preferred_element_type=F32)
    u_ref[...] = nat[:, :SSM_WIDTH].astype(BF16)
    zs_ref[...] = nat[:, SSM_WIDTH:].astype(BF16)


def _in_proj(x, positions, norm_w, w_in, q_norm_w, k_norm_w):
    b, s, d = x.shape
    tm, ta = TOKEN_TILE, ATTN_TILE
    nt, tiles_per_step = s // ta, tm // ta
    inv_freq = ROPE_THETA ** (-jnp.arange(0, ROT_DIM, 2, dtype=F32) / ROT_DIM)
    head_major = jax.ShapeDtypeStruct((b, N_HEADS, nt, V_HEAD_DIM, ta), BF16)
    head_spec = pl.BlockSpec((None, N_HEADS, tiles_per_step, V_HEAD_DIM, ta), lambda bi, ti: (bi, 0, ti, 0, 0))
    ssm_shape = jax.ShapeDtypeStruct((b, s, SSM_WIDTH), BF16)
    ssm_spec = pl.BlockSpec((None, tm, SSM_WIDTH), lambda bi, ti: (bi, ti, 0))
    const2 = lambda bi, ti: (0, 0)
    const3 = lambda bi, ti: (0, 0, 0)
    return pl.pallas_call(
        _in_proj_kernel,
        grid=(b, s // tm),
        in_specs=[
            pl.BlockSpec((None, tm, d), lambda bi, ti: (bi, ti, 0)),
            pl.BlockSpec((1, d), const2),
            pl.BlockSpec(w_in.shape, const2),
            pl.BlockSpec((None, 1, tm), lambda bi, ti: (bi, 0, ti)),
            pl.BlockSpec((ROT_HALF, 1), const2),
            pl.BlockSpec((1, HEAD_DIM, 1), const3),
            pl.BlockSpec((1, HEAD_DIM, 1), const3),
        ],
        out_specs=[
            ssm_spec, ssm_spec,
            head_spec,
            pl.BlockSpec((None, tm, ATTN_WIDTH), lambda bi, ti: (bi, ti, 0)),
            head_spec, head_spec,
        ],
        out_shape=(ssm_shape, ssm_shape, head_major,
                   jax.ShapeDtypeStruct((b, s, ATTN_WIDTH), BF16), head_major, head_major),
        scratch_shapes=[
            pltpu.VMEM((d, 2 * SSM_WIDTH), BF16),
            pltpu.VMEM((4 * ATTN_WIDTH, d), BF16),
        ],
        compiler_params=_compiler_params(2),
        name="in_proj",
    )(x, norm_w.reshape(1, d), w_in, positions.reshape(b, 1, s),
      inv_freq.reshape(ROT_HALF, 1), q_norm_w.reshape(1, HEAD_DIM, 1), k_norm_w.reshape(1, HEAD_DIM, 1))


def _ssm_kernel(u_ref, zs_ref, pin_ref, pout_ref, ab_ref, ca_ref, cd_ref, ar_ref, ai_ref, d_ref,
                gw_ref, gb_ref, y_ref, uf_ref, bu_ref, xs_ref, y2_ref, yg_ref, state_ref,
                bw_ref, cs_ref, tmp_ref):
    n_batch, steps, width = u_ref.shape
    fold, perm_out_rows, perm_in_rows = pin_ref.shape
    perm_steps = perm_in_rows // n_batch
    part_steps = SSM_PART_STEPS
    perms_per_part = part_steps // perm_steps
    part_rows = perms_per_part * perm_out_rows
    n_parts = steps // part_steps
    blk = 2 * BLOCK_STATE
    fw = fold * V7X_LANES
    pack_rows = 2 * V7X_SUBLANES

    @pl.when(pl.program_id(0) == 0)
    def _():
        state_ref[...] = jnp.zeros_like(state_ref)
        bw_ref[...] = jnp.zeros_like(bw_ref)
        for m in range(N_SSM_BLOCKS):
            tmp_ref[...] = jnp.zeros_like(tmp_ref)
            for i in range(fold):
                for gl in range(GROUPS_PER_BLOCK):
                    g = m * GROUPS_PER_BLOCK + gl
                    in_rows = slice(i * V7X_LANES + gl * SSM_GROUP, i * V7X_LANES + (gl + 1) * SSM_GROUP)
                    for ri in range(2):
                        st_cols = slice(ri * BLOCK_STATE + gl * SSM_STATE, ri * BLOCK_STATE + (gl + 1) * SSM_STATE)
                        bw_ref[m, in_rows, st_cols] = ab_ref[i, ri, g].astype(BF16)
                        tmp_ref[in_rows, st_cols] = ca_ref[i, ri, g]
            cs_ref[m] = tmp_ref[...].T.astype(BF16)

    def rows_of(part):
        return slice(part * part_rows, (part + 1) * part_rows)

    def fold_in(part):
        for h in range(perms_per_part):
            t0 = part * part_steps + h * perm_steps
            u_bt = u_ref[:, t0:t0 + perm_steps, :].reshape(perm_in_rows, width)
            r0 = part * part_rows + h * perm_out_rows
            sel_all = jnp.dot(pin_ref[...].reshape(fold * perm_out_rows, perm_in_rows), u_bt,
                              preferred_element_type=F32).astype(BF16)
            for i in range(fold):
                sel = sel_all[i * perm_out_rows:(i + 1) * perm_out_rows]
                for m in range(N_SSM_BLOCKS):
                    uf_ref[r0:r0 + perm_out_rows, m * fw + i * V7X_LANES:m * fw + (i + 1) * V7X_LANES] = (
                        sel[:, m * V7X_LANES:(m + 1) * V7X_LANES])

    def state_block(part, m):
        rows = rows_of(part)
        re = slice(m * blk, m * blk + BLOCK_STATE)
        im = slice(m * blk + BLOCK_STATE, (m + 1) * blk)
        bu_ref[rows, m * blk:(m + 1) * blk] = jnp.dot(
            uf_ref[rows, m * fw:(m + 1) * fw], bw_ref[m], preferred_element_type=F32)
        a_re = jnp.broadcast_to(ar_ref[:, m * BLOCK_STATE:(m + 1) * BLOCK_STATE], (V7X_SUBLANES, BLOCK_STATE))
        a_im = jnp.broadcast_to(ai_ref[:, m * BLOCK_STATE:(m + 1) * BLOCK_STATE], (V7X_SUBLANES, BLOCK_STATE))
        x_re = state_ref[:, re]
        x_im = state_ref[:, im]
        for r0 in range(part * part_rows, (part + 1) * part_rows, pack_rows):
            ins_re, ins_im = [], []
            for half in range(2):
                rr = slice(r0 + half * V7X_SUBLANES, r0 + (half + 1) * V7X_SUBLANES)
                ins_re.append(x_re)
                ins_im.append(x_im)
                n_re = a_re * x_re - a_im * x_im + bu_ref[rr, re]
                n_im = a_re * x_im + a_im * x_re + bu_ref[rr, im]
                x_re, x_im = n_re, n_im
            xs_ref[r0:r0 + pack_rows, re] = jnp.concatenate(ins_re, axis=0).astype(BF16)
            xs_ref[r0:r0 + pack_rows, im] = jnp.concatenate(ins_im, axis=0).astype(BF16)
        state_ref[:, re] = x_re
        state_ref[:, im] = x_im
        y2_ref[rows, m * fw:(m + 1) * fw] = (
            jnp.dot(xs_ref[rows, m * blk:(m + 1) * blk], cs_ref[m], preferred_element_type=F32)
            + jnp.dot(uf_ref[rows, m * fw:(m + 1) * fw], cd_ref[m], preferred_element_type=F32))

    def gate_offset(part, i):
        rows = rows_of(part)
        cols = [slice(m * fw + i * V7X_LANES, m * fw + (i + 1) * V7X_LANES) for m in range(N_SSM_BLOCKS)]
        y = jnp.concatenate([y2_ref[rows, c] for c in cols], axis=-1)
        u_i = jnp.concatenate([uf_ref[rows, c] for c in cols], axis=-1).astype(F32)
        y = _gelu_tanh(y + d_ref[...] * u_i)
        gate = jnp.dot(y.astype(BF16), gw_ref[...], preferred_element_type=F32) + gb_ref[...]
        y = (y * jax.nn.sigmoid(gate)).astype(BF16)
        for h in range(perms_per_part):
            yg_ref[part * perms_per_part + h, i * perm_out_rows:(i + 1) * perm_out_rows, :] = (
                y[h * perm_out_rows:(h + 1) * perm_out_rows])

    def unfold_out(part):
        for h in range(perms_per_part):
            t0 = part * part_steps + h * perm_steps
            y_bt = jnp.dot(pout_ref[...], yg_ref[part * perms_per_part + h], preferred_element_type=F32)
            z = zs_ref[:, t0:t0 + perm_steps, :].reshape(perm_in_rows, width).astype(F32)
            y_ref[:, t0:t0 + perm_steps, :] = (
                (y_bt * jax.nn.silu(z)).astype(BF16).reshape(n_batch, perm_steps, width))

    assert fold == N_SSM_BLOCKS
    for part in range(n_parts + 1):
        if part < n_parts:
            fold_in(part)
        for j in range(N_SSM_BLOCKS):
            if part < n_parts:
                state_block(part, j)
            if part >= 1:
                gate_offset(part - 1, j)
        if part >= 1:
            unfold_out(part - 1)


def _ssm_direct_weights(cl, bbc):
    fold, gpb, nb, p = SSM_FOLD, GROUPS_PER_BLOCK, N_SSM_BLOCKS, SSM_GROUP
    lanes = gpb * p
    hi = lax.Precision.HIGHEST
    lags = jnp.einsum('lgpn,gqn->lgqp', cl, bbc, precision=hi)
    lane = np.arange(lanes)
    spread = (np.arange(p)[:, None] == (lane % p)[None, :]).astype(np.float32)
    same_group = ((lane // p)[:, None] == (lane // p)[None, :]).astype(np.float32)
    bd = jnp.dot(lags.reshape(fold, nb, lanes, p), spread, precision=hi) * same_group
    zero = jnp.zeros_like(bd[0])
    rows = [jnp.concatenate([bd[i - k] if i >= k else zero for i in range(fold)], axis=-1) for k in range(fold)]
    return jnp.concatenate(rows, axis=-2).astype(BF16)


def _ssm_branch(u, zs, ab, ca, cl, bbc, al, d_skip, glu_w, glu_b):
    n_batch, s, _ = u.shape
    assert n_batch == V7X_SUBLANES, "one time group of all batches must fill one sublane tile"
    fold, steps, perm_steps = SSM_FOLD, SSM_STEPS, SSM_PERM_STEPS
    g_rows = (perm_steps // fold) * n_batch
    n_rows = perm_steps * n_batch
    r = np.arange(g_rows)
    src = (r % n_batch) * perm_steps + fold * (r // n_batch)
    pin = np.arange(n_rows)[None, None, :] == (src[None, :, None] + np.arange(fold)[:, None, None])
    pout = jnp.asarray(pin.reshape(fold * g_rows, n_rows).T, dtype=BF16)
    pin = jnp.asarray(pin, dtype=BF16)
    cd = _ssm_direct_weights(cl, bbc)
    row_spec = pl.BlockSpec((n_batch, steps, SSM_WIDTH), lambda i: (0, i, 0))
    const2 = lambda i: (0, 0)
    const3 = lambda i: (0, 0, 0)
    n_state = SSM_GROUPS * SSM_STATE
    f_rows = (steps // fold) * n_batch
    fw_all = fold * SSM_WIDTH
    return pl.pallas_call(
        _ssm_kernel,
        grid=(s // steps,),
        in_specs=[
            row_spec, row_spec,
            pl.BlockSpec(pin.shape, const3),
            pl.BlockSpec(pout.shape, const2),
            pl.BlockSpec(ab.shape, lambda i: (0, 0, 0, 0, 0)),
            pl.BlockSpec(ca.shape, lambda i: (0, 0, 0, 0, 0)),
            pl.BlockSpec(cd.shape, const3),
            pl.BlockSpec((1, n_state), const2),
            pl.BlockSpec((1, n_state), const2),
            pl.BlockSpec((1, SSM_WIDTH), const2),
            pl.BlockSpec((SSM_WIDTH, SSM_WIDTH), const2),
            pl.BlockSpec((1, SSM_WIDTH), const2),
        ],
        out_specs=row_spec,
        out_shape=jax.ShapeDtypeStruct((n_batch, s, SSM_WIDTH), BF16),
        scratch_shapes=[
            pltpu.VMEM((f_rows, fw_all), BF16),
            pltpu.VMEM((f_rows, 2 * n_state), F32),
            pltpu.VMEM((f_rows, 2 * n_state), BF16),
            pltpu.VMEM((f_rows, fw_all), F32),
            pltpu.VMEM((steps // perm_steps, fold * g_rows, SSM_WIDTH), BF16),
            pltpu.VMEM((V7X_SUBLANES, 2 * n_state), F32),
            pltpu.VMEM((N_SSM_BLOCKS, fold * V7X_LANES, 2 * BLOCK_STATE), BF16),
            pltpu.VMEM((N_SSM_BLOCKS, 2 * BLOCK_STATE, fold * V7X_LANES), BF16),
            pltpu.VMEM((fold * V7X_LANES, 2 * BLOCK_STATE), F32),
        ],
        compiler_params=_compiler_params(1),
        name="ssm_branch",
    )(u, zs, pin, pout, ab, ca, cd, al[0].reshape(1, n_state), al[1].reshape(1, n_state),
      d_skip.reshape(1, SSM_WIDTH), glu_w.astype(BF16), glu_b.reshape(1, SSM_WIDTH))


def _attn_kernel(lambda_init, qt_ref, k_ref, vt_ref, zat_ref, lq1_ref, lk1_ref, lq2_ref, lk2_ref,
                 subw_ref, y_ref, wq_ref, s_ref, mx_ref, acc_ref, m_ref):
    n_tiles, _, tq = qt_ref.shape
    tk = tq
    diag_slot = 2

    def load_queries(q):
        wq_ref[0, :HEAD_DIM] = qt_ref[q, :HEAD_DIM]
        wq_ref[1, HEAD_DIM:] = qt_ref[q, HEAD_DIM:]

    def reset_stats():
        m_ref[...] = jnp.full_like(m_ref, MASK_VALUE)
        acc_ref[...] = jnp.zeros_like(acc_ref)

    def scores(j, slot):
        kb = k_ref[pl.ds(pl.multiple_of(j * tk, tk), tk), :]
        for c in range(2):
            s = jnp.dot(kb, wq_ref[c], preferred_element_type=F32)
            s_ref[slot, c] = s
            mx_ref[slot, c] = jnp.max(s, axis=0, keepdims=True)

    def diagonal_scores(q, slot, fill_masked=False):
        h = tk // 2
        row0 = pl.multiple_of(q * tk, tk)
        k_lo = k_ref[pl.ds(row0, h), :]
        k_hi = k_ref[pl.ds(row0 + h, h), :]
        causal = (lax.broadcasted_iota(jnp.int32, (h, h), 0) <= lax.broadcasted_iota(jnp.int32, (h, h), 1))
        for c in range(2):
            s_lo = jnp.dot(k_lo, wq_ref[c], preferred_element_type=F32)
            s_hi = jnp.dot(k_hi, wq_ref[c, :, h:], preferred_element_type=F32)
            s_ll = jnp.where(causal, s_lo[:, :h], MASK_VALUE)
            s_hh = jnp.where(causal, s_hi, MASK_VALUE)
            s_ref[slot, c, :h, :h] = s_ll
            s_ref[slot, c, :h, h:] = s_lo[:, h:]
            if fill_masked:
                s_ref[slot, c, h:, :h] = jnp.full((h, h), MASK_VALUE, F32)
            s_ref[slot, c, h:, h:] = s_hh
            mx_ref[slot, c, :, :h] = jnp.max(s_ll, axis=0, keepdims=True)
            mx_ref[slot, c, :, h:] = jnp.maximum(jnp.max(s_lo[:, h:], axis=0, keepdims=True),
                                                 jnp.max(s_hh, axis=0, keepdims=True))

    def accumulate(j, slot):
        vb = jnp.concatenate([vt_ref[j], jnp.ones((SUM_ROWS, tk), BF16)], axis=0)
        for c in range(2):
            m_old = m_ref[c]
            m_new = jnp.maximum(m_old, mx_ref[slot, c])
            alpha = jnp.exp2(m_old - m_new)
            p = jnp.exp2(s_ref[slot, c] - m_new)
            acc_ref[c] = alpha * acc_ref[c] + jnp.dot(vb, p.astype(BF16), preferred_element_type=F32)
            m_ref[c] = m_new

    def accumulate_diagonal_first(q, slot):
        h = tk // 2
        vb = jnp.concatenate([vt_ref[q], jnp.ones((SUM_ROWS, tk), BF16)], axis=0)
        for c in range(2):
            m_new = mx_ref[slot, c]
            p_early = jnp.exp2(s_ref[slot, c, :h, :h] - m_new[:, :h])
            p_late = jnp.exp2(s_ref[slot, c, :, h:] - m_new[:, h:])
            acc_ref[c, :, :h] = jnp.dot(vb[:, :h], p_early.astype(BF16), preferred_element_type=F32)
            acc_ref[c, :, h:] = jnp.dot(vb, p_late.astype(BF16), preferred_element_type=F32)
            m_ref[c] = m_new

    def finalize(q):
        lam = (jnp.exp(jnp.sum(lq1_ref[...] * lk1_ref[...], axis=-1, keepdims=True))
               - jnp.exp(jnp.sum(lq2_ref[...] * lk2_ref[...], axis=-1, keepdims=True)) + lambda_init)
        l0 = acc_ref[0, V_HEAD_DIM:V_HEAD_DIM + 1]
        l1 = acc_ref[1, V_HEAD_DIM:V_HEAD_DIM + 1]
        out = acc_ref[0, :V_HEAD_DIM] * (1.0 / l0) - acc_ref[1, :V_HEAD_DIM] * (lam / l1)
        ms = jnp.mean(out * out, axis=0, keepdims=True)
        out = out * (lax.rsqrt(ms + EPS) * (1.0 - lambda_init)) * subw_ref[...]
        out = out * jax.nn.silu(zat_ref[q].astype(F32))
        y_ref[pl.ds(pl.multiple_of(q * tq, tq), tq), :] = out.T.astype(BF16)

    zeros = jnp.zeros((HEAD_DIM, tq), BF16)
    wq_ref[0, HEAD_DIM:] = zeros
    wq_ref[1, :HEAD_DIM] = zeros
    load_queries(0)
    reset_stats()
    diagonal_scores(0, 1, fill_masked=True)

    def enter_tile(q, pending_slot):
        load_queries(q)
        diagonal_scores(q, diag_slot)
        accumulate(jnp.maximum(q - 2, 0), pending_slot)
        finalize(q - 1)
        scores(0, 0)
        accumulate_diagonal_first(q, diag_slot)

    def pair(i, c):
        j = 2 * i
        scores(j + 1, 1)
        accumulate(j, 0)
        scores(j + 2, 0)
        accumulate(j + 1, 1)
        return c

    def two_tiles(k, carry):
        q_odd = 2 * k + 1
        enter_tile(q_odd, 1)
        lax.fori_loop(0, k, pair, 0)
        q_even = q_odd + 1
        enter_tile(q_even, 0)
        lax.fori_loop(0, k, pair, 0)
        scores(q_even - 1, 1)
        accumulate(q_even - 2, 0)
        return carry

    assert n_tiles % 2 == 0
    lax.fori_loop(0, (n_tiles - 2) // 2, two_tiles, 0)
    q_last = n_tiles - 1
    enter_tile(q_last, 1)
    lax.fori_loop(0, (q_last - 1) // 2, pair, 0)
    accumulate(q_last - 1, 0)
    finalize(q_last)


def _attention(qt, k, vt, zat, lq1, lk1, lq2, lk2, subln_w, lambda_init):
    b, _, nt, _, tq = qt.shape
    s = nt * tq
    assert nt >= 2
    row64 = pl.BlockSpec((1, HEAD_DIM), lambda bi, hi: (0, 0))
    head_spec = pl.BlockSpec((None, None, nt, V_HEAD_DIM, tq), lambda bi, hi: (bi, hi, 0, 0, 0))
    token_spec = pl.BlockSpec((None, s, V_HEAD_DIM), lambda bi, hi: (bi, 0, hi))
    return pl.pallas_call(
        functools.partial(_attn_kernel, lambda_init),
        grid=(b, N_HEADS),
        in_specs=[
            head_spec, token_spec, head_spec, head_spec,
            row64, row64, row64, row64,
            pl.BlockSpec((V_HEAD_DIM, 1), lambda bi, hi: (0, 0)),
        ],
        out_specs=token_spec,
        out_shape=jax.ShapeDtypeStruct((b, s, ATTN_WIDTH), BF16),
        scratch_shapes=[
            pltpu.VMEM((2, 2 * HEAD_DIM, tq), BF16),
            pltpu.VMEM((3, 2, tq, tq), F32),
            pltpu.VMEM((3, 2, 1, tq), F32),
            pltpu.VMEM((2, V_HEAD_DIM + SUM_ROWS, tq), F32),
            pltpu.VMEM((2, 1, tq), F32),
        ],
        compiler_params=_compiler_params(2),
        name="diff_attention",
    )(qt, k, vt, zat, lq1.reshape(1, HEAD_DIM), lk1.reshape(1, HEAD_DIM),
      lq2.reshape(1, HEAD_DIM), lk2.reshape(1, HEAD_DIM), subln_w.reshape(V_HEAD_DIM, 1))


def _out_kernel(x_ref, ys_ref, ya_ref, p_ref, wos_ref, woa_ref, wg_ref, wp_ref, o_ref):
    x2 = (x_ref[...]
          + jnp.dot(ys_ref[...], wos_ref[...], preferred_element_type=F32)
          + jnp.dot(ya_ref[...], woa_ref[...], preferred_element_type=F32))
    gate = jax.nn.sigmoid(jnp.dot(x2.astype(BF16), wg_ref[...], preferred_element_type=F32))
    ple = jnp.dot(p_ref[...].astype(BF16), wp_ref[...], preferred_element_type=F32)
    o_ref[...] = x2 + gate * ple


def _out_proj(x, ys, ya, p, w_out, w_proj, w_gate):
    b, s, d = x.shape
    tm = OUT_TOKEN_TILE
    const2 = lambda bi, ti: (0, 0)
    return pl.pallas_call(
        _out_kernel,
        grid=(b, s // tm),
        in_specs=[
            pl.BlockSpec((None, tm, d), lambda bi, ti: (bi, ti, 0)),
            pl.BlockSpec((None, tm, SSM_WIDTH), lambda bi, ti: (bi, ti, 0)),
            pl.BlockSpec((None, tm, ATTN_WIDTH), lambda bi, ti: (bi, ti, 0)),
            pl.BlockSpec((None, tm, PLE_DIM), lambda bi, ti: (bi, ti, 0)),
            pl.BlockSpec((SSM_WIDTH, d), const2),
            pl.BlockSpec((ATTN_WIDTH, d), const2),
            pl.BlockSpec((d, d), const2),
            pl.BlockSpec((PLE_DIM, d), const2),
        ],
        out_specs=pl.BlockSpec((None, tm, d), lambda bi, ti: (bi, ti, 0)),
        out_shape=jax.ShapeDtypeStruct((b, s, d), F32),
        compiler_params=_compiler_params(2),
        name="out_proj",
    )(x, ys, ya, p, w_out[:SSM_WIDTH].astype(BF16), w_out[SSM_WIDTH:].astype(BF16),
      w_gate.astype(BF16), w_proj.astype(BF16))


def kernel(x, p, positions, norm_w, w_in, ssm_lambda_re, ssm_lambda_im, ssm_log_dt, ssm_b_re, ssm_b_im, ssm_c_re, ssm_c_im, ssm_d, glu_w, glu_b, q_norm_w, k_norm_w, lambda_q1, lambda_k1, lambda_q2, lambda_k2, subln_w, w_out, ple_w_proj, ple_w_gate):
    depth = norm_w.shape[0]
    for i in range(depth):
        lambda_init = 0.8 - 0.6 * math.exp(-0.3 * i)
        ab, ca, cl, bbc, al = _ssm_params(ssm_lambda_re[i], ssm_lambda_im[i], ssm_log_dt[i],
                                          ssm_b_re[i], ssm_b_im[i], ssm_c_re[i], ssm_c_im[i])
        u, zs, qt, k, vt, zat = _in_proj(x, positions, norm_w[i], w_in[i], q_norm_w[i], k_norm_w[i])
        ys = _ssm_branch(u, zs, ab, ca, cl, bbc, al, ssm_d[i], glu_w[i], glu_b[i])
        ya = _attention(qt, k, vt, zat, lambda_q1[i], lambda_k1[i], lambda_q2[i], lambda_k2[i],
                        subln_w[i], lambda_init)
        x = _out_proj(x, ys, ya, p[i], w_out[i], ple_w_proj[i], ple_w_gate[i])
    return x
```
